```python
import math
import jax, jax.numpy as jnp
from jax import lax
import numpy as np

D_MODEL = 1024
BATCH = 8
SEQ = 4096
DEPTH = 2

BRANCH_WIDTH = 512
N_BRANCHES = 4
RET_HEADS = 4
RET_QK_DIM = 64
RET_V_DIM = 128
RET_CHUNK = 128
SWA_Q_HEADS = 8
SWA_KV_HEADS = 2
SWA_HEAD_DIM = 64
SWA_WINDOW = 128
SWA_BLOCK = 128
POOL_WINDOWS = (2, 4, 8, 16)
POOL_GROUP_DIM = BRANCH_WIDTH // 4
M_HEAD_DIM = 64
M_HEADS = BRANCH_WIDTH // M_HEAD_DIM
M_GROUPS = 2
M_STATE = 128
M_CONV = 4
M_CHUNK = 128
M_BC_W = M_GROUPS * M_STATE
M_CONV_DIM = BRANCH_WIDTH + 2 * M_BC_W
EPS = 1e-6

RET_QK_W = RET_HEADS * RET_QK_DIM
RET_V_W = RET_HEADS * RET_V_DIM
SWA_Q_W = SWA_Q_HEADS * SWA_HEAD_DIM
SWA_KV_W = SWA_KV_HEADS * SWA_HEAD_DIM
IN_SPLIT_SIZES = (RET_QK_W, RET_QK_W, RET_V_W, BRANCH_WIDTH,
                  SWA_Q_W, SWA_KV_W, SWA_KV_W, BRANCH_WIDTH,
                  BRANCH_WIDTH, BRANCH_WIDTH,
                  M_CONV_DIM, BRANCH_WIDTH, M_HEADS,
                  N_BRANCHES * D_MODEL)
D_IN = sum(IN_SPLIT_SIZES)

kernel_name = 'hybrid_gated_retention_swa_pool_ssd'


def _rms(x):
    xf = x.astype(jnp.float32)
    return xf * lax.rsqrt(jnp.mean(xf * xf, axis=-1, keepdims=True) + EPS)


def rms_norm(x, w):
    return (_rms(x) * w.astype(jnp.float32)).astype(x.dtype)


def alibi_slopes(n):
    return 2.0 ** (-8.0 * jnp.arange(1, n + 1, dtype=jnp.float32) / n)


def retention(q, k, v):
    b, s, h, dk = q.shape
    dv = v.shape[-1]
    L = RET_CHUNK
    nc = s // L
    log_g = jnp.log(1.0 - 2.0 ** (-5.0 - jnp.arange(h, dtype=jnp.float32)))
    pos = jnp.arange(L, dtype=jnp.float32)
    diff = pos[:, None] - pos[None, :]
    causal = diff >= 0
    inner_decay = jnp.where(causal, jnp.exp(log_g[:, None, None] * jnp.where(causal, diff, 0.0)), 0.0)
    q_decay = jnp.exp(log_g[:, None] * (pos + 1.0))
    k_decay = jnp.exp(log_g[:, None] * (L - 1.0 - pos))
    chunk_decay = jnp.exp(log_g * L)
    qc = q.reshape(b, nc, L, h, dk)
    kc = (k * dk ** -0.5).reshape(b, nc, L, h, dk)
    vc = v.reshape(b, nc, L, h, dv)
    scores = jnp.einsum('bclhd,bcshd->bchls', qc, kc) * inner_decay
    inner = jnp.einsum('bchls,bcshe->bclhe', scores, vc)

    def step(state, inp):
        q_i, k_i, v_i = inp
        cross = jnp.einsum('blhd,bhde,hl->blhe', q_i, state, q_decay)
        new = state * chunk_decay[None, :, None, None] + jnp.einsum('blhd,hl,blhe->bhde', k_i, k_decay, v_i)
        return new, cross

    init = jnp.zeros((b, h, dk, dv), jnp.float32)
    _, cross = lax.scan(step, init, (jnp.moveaxis(qc, 1, 0), jnp.moveaxis(kc, 1, 0), jnp.moveaxis(vc, 1, 0)))
    out = inner + jnp.moveaxis(cross, 0, 1)
    return out.reshape(b, s, h, dv)


def sliding_window_attention(q, k, v, sinks):
    b, s, hq, d = q.shape
    hkv = k.shape[2]
    g = hq // hkv
    L = SWA_BLOCK
    nb = s // L
    qb = (q * d ** -0.5).reshape(b, nb, L, hkv, g, d)

    def with_prev(t):
        tb = t.reshape(b, nb, L, hkv, d)
        prev = jnp.concatenate([jnp.zeros_like(tb[:, :1]), tb[:, :-1]], axis=1)
        return jnp.concatenate([prev, tb], axis=2)

    kb = with_prev(k)
    vb = with_prev(v)
    scores = jnp.einsum('bnqhgd,bnkhd->bnhgqk', qb, kb).astype(jnp.float32)
    qi = jnp.arange(L)
    kj = jnp.arange(2 * L)
    delta = L + qi[:, None] - kj[None, :]
    key_pos = (jnp.arange(nb)[:, None] - 1) * L + kj[None, :]
    valid = ((delta >= 0) & (delta < SWA_WINDOW))[None] & (key_pos >= 0)[:, None, :]
    slopes = alibi_slopes(hq).reshape(hkv, g)
    scores = scores - slopes[:, :, None, None] * delta.astype(jnp.float32)
    scores = jnp.where(valid[None, :, None, None], scores, -jnp.inf)
    sink = sinks.astype(jnp.float32).reshape(hkv, g)[None, None, :, :, None, None]
    m = jnp.maximum(scores.max(axis=-1, keepdims=True), sink)
    p = jnp.exp(scores - m)
    probs = p / (p.sum(axis=-1, keepdims=True) + jnp.exp(sink - m))
    out = jnp.einsum('bnhgqk,bnkhd->bnqhgd', probs.astype(v.dtype), vb)
    return out.reshape(b, s, hq * d)


def multiscale_pool(u, w_grp, scale):
    b, s, cdim = u.shape
    gd = cdim // len(POOL_WINDOWS)
    uf = u.astype(jnp.float32)
    cs = jnp.cumsum(uf, axis=1)
    t = jnp.arange(s)
    outs = []
    for i, w in enumerate(POOL_WINDOWS):
        cs_g = cs[..., i * gd:(i + 1) * gd]
        lag = jnp.pad(cs_g, ((0, 0), (w, 0), (0, 0)))[:, :s]
        count = jnp.minimum(t + 1, w).astype(jnp.float32)[None, :, None]
        outs.append((cs_g - lag) / count - uf[..., i * gd:(i + 1) * gd])
    diff = jnp.stack(outs, axis=2)
    y = jnp.einsum('bsgc,gce->bsge', diff, w_grp).reshape(b, s, cdim)
    return y * scale


def causal_depthwise_conv(u, w, bias):
    k, ch = w.shape
    out = lax.conv_general_dilated(u, w[:, None, :].astype(u.dtype), window_strides=(1,),
                                   padding=((k - 1, 0),), dimension_numbers=('NWC', 'WIO', 'NWC'),
                                   feature_group_count=ch)
    return out + bias


def ssd_scan(x, dt, a, bmat, cmat, d_skip):
    b, s, h, p = x.shape
    grp, n = bmat.shape[2], bmat.shape[3]
    r = h // grp
    L = M_CHUNK
    nc = s // L
    xc = x.reshape(b, nc, L, grp, r, p)
    dtc = dt.reshape(b, nc, L, grp, r)
    acs = jnp.cumsum(dtc * a.reshape(grp, r), axis=2)
    bc = bmat.reshape(b, nc, L, grp, n)
    cc = cmat.reshape(b, nc, L, grp, n)
    xdt = xc * dtc[..., None]
    acs_t = jnp.moveaxis(acs, 2, -1)
    causal = jnp.tril(jnp.ones((L, L), dtype=bool))
    seg = acs_t[..., :, None] - acs_t[..., None, :]
    lmat = jnp.exp(jnp.where(causal, seg, -jnp.inf))
    cb = jnp.einsum('bclgn,bcsgn->bcgls', cc, bc)
    y_diag = jnp.einsum('bcgls,bcgrls,bcsgrp->bclgrp', cb, lmat, xdt)
    decay_to_end = jnp.exp(acs[:, :, -1:] - acs)

    def step(state, inp):
        b_i, c_i, acs_i, dend_i, xdt_i = inp
        y_off = jnp.einsum('blgn,bgrpn,blgr->blgrp', c_i, state, jnp.exp(acs_i))
        new = state * jnp.exp(acs_i[:, -1])[..., None, None] + jnp.einsum('blgn,blgr,blgrp->bgrpn', b_i, dend_i, xdt_i)
        return new, y_off

    init = jnp.zeros((b, grp, r, p, n), jnp.float32)
    xs = (jnp.moveaxis(bc, 1, 0), jnp.moveaxis(cc, 1, 0), jnp.moveaxis(acs, 1, 0),
          jnp.moveaxis(decay_to_end, 1, 0), jnp.moveaxis(xdt, 1, 0))
    _, y_off = lax.scan(step, init, xs)
    y = y_diag + jnp.moveaxis(y_off, 0, 1) + xc * d_skip.reshape(grp, r)[:, :, None]
    return y.reshape(b, s, h * p)


def hybrid_layer(x, c, ada_w, ada_b, norm_w, w_in, sinks, pool_w, pool_scale,
                 conv_w, conv_b, dt_bias, a_log, d_skip, ssm_norm_w, w_up, w_out):
    b, s, _ = x.shape
    mod = jnp.einsum('bd,de->be', jax.nn.silu(c), ada_w) + ada_b
    shift, scale, gate = jnp.split(mod, 3, axis=-1)
    h = rms_norm(x, norm_w) * (1.0 + scale[:, None, :]) + shift[:, None, :]
    proj = jnp.einsum('bsd,de->bse', h, w_in)
    offsets = np.cumsum(IN_SPLIT_SIZES)[:-1].tolist()
    (rq, rk, rv, rg, sq, sk, sv, sg, px, pg, mxbc, mz, mdt, mg) = jnp.split(proj, offsets, axis=-1)

    ret = retention(rq.reshape(b, s, RET_HEADS, RET_QK_DIM), rk.reshape(b, s, RET_HEADS, RET_QK_DIM),
                    rv.reshape(b, s, RET_HEADS, RET_V_DIM))
    ret = _rms(ret).reshape(b, s, BRANCH_WIDTH) * jax.nn.silu(rg.astype(jnp.float32))

    att = sliding_window_attention(sq.reshape(b, s, SWA_Q_HEADS, SWA_HEAD_DIM),
                                   sk.reshape(b, s, SWA_KV_HEADS, SWA_HEAD_DIM),
                                   sv.reshape(b, s, SWA_KV_HEADS, SWA_HEAD_DIM), sinks)
    att = att * jax.nn.silu(sg)

    pool = multiscale_pool(px, pool_w, pool_scale) * jax.nn.silu(pg.astype(jnp.float32))

    xbc = jax.nn.silu(causal_depthwise_conv(mxbc, conv_w, conv_b))
    mx, mb, mc = jnp.split(xbc, [BRANCH_WIDTH, BRANCH_WIDTH + M_BC_W], axis=-1)
    dt = jax.nn.softplus(mdt.astype(jnp.float32) + dt_bias.astype(jnp.float32))
    a = -jnp.exp(a_log.astype(jnp.float32))
    y = ssd_scan(mx.reshape(b, s, M_HEADS, M_HEAD_DIM), dt, a,
                 mb.reshape(b, s, M_GROUPS, M_STATE), mc.reshape(b, s, M_GROUPS, M_STATE), d_skip)
    yz = (y * jax.nn.silu(mz.astype(jnp.float32))).reshape(b, s, M_GROUPS, BRANCH_WIDTH // M_GROUPS)
    ssm = _rms(yz).reshape(b, s, BRANCH_WIDTH) * ssm_norm_w.astype(jnp.float32)

    branches = jnp.stack([ret.astype(x.dtype), att.astype(x.dtype), pool.astype(x.dtype), ssm.astype(x.dtype)], axis=2)
    up = jnp.einsum('bsne,ned->bsnd', branches, w_up)
    gates = jax.nn.sigmoid(mg.astype(jnp.float32)).reshape(b, s, N_BRANCHES, D_MODEL)
    merged = jnp.sum(gates * up, axis=2).astype(x.dtype)
    out = jnp.einsum('bsd,de->bse', merged, w_out)
    return x + gate[:, None, :] * out


def setup_inputs(seed: int = 0) -> dict:
    key = jax.random.key(seed)
    ks = jax.random.split(key, 20)
    f32 = jnp.float32
    gd = POOL_GROUP_DIM
    dt_u = jax.random.uniform(ks[11], (DEPTH, M_HEADS), f32)
    dt0 = jnp.exp(dt_u * (math.log(0.1) - math.log(0.001)) + math.log(0.001))
    return {
        'x': jax.random.normal(ks[0], (BATCH, SEQ, D_MODEL), f32),
        'c': jax.random.normal(ks[1], (BATCH, D_MODEL), f32),
        'ada_w': jax.random.normal(ks[2], (DEPTH, D_MODEL, 3 * D_MODEL), f32) * D_MODEL ** -0.5,
        'ada_b': jax.random.normal(ks[3], (DEPTH, 3 * D_MODEL), f32) * 0.02,
        'norm_w': 1.0 + 0.02 * jax.random.normal(ks[4], (DEPTH, D_MODEL), f32),
        'w_in': jax.random.normal(ks[5], (DEPTH, D_MODEL, D_IN), f32) * D_MODEL ** -0.5,
        'swa_sinks': jax.random.normal(ks[6], (DEPTH, SWA_Q_HEADS), f32),
        'pool_w': jax.random.normal(ks[7], (DEPTH, len(POOL_WINDOWS), gd, gd), f32) * gd ** -0.5,
        'pool_scale': 1.0 + 0.02 * jax.random.normal(ks[8], (DEPTH, BRANCH_WIDTH), f32),
        'conv_w': jax.random.normal(ks[9], (DEPTH, M_CONV, M_CONV_DIM), f32) * M_CONV ** -0.5,
        'conv_b': jax.random.normal(ks[10], (DEPTH, M_CONV_DIM), f32) * 0.02,
        'dt_bias': dt0 + jnp.log(-jnp.expm1(-dt0)),
        'a_log': jnp.log(jax.random.uniform(ks[12], (DEPTH, M_HEADS), f32, minval=1.0, maxval=16.0)),
        'd_skip': 1.0 + 0.02 * jax.random.normal(ks[13], (DEPTH, M_HEADS), f32),
        'ssm_norm_w': 1.0 + 0.02 * jax.random.normal(ks[14], (DEPTH, BRANCH_WIDTH), f32),
        'w_up': jax.random.normal(ks[15], (DEPTH, N_BRANCHES, BRANCH_WIDTH, D_MODEL), f32) * BRANCH_WIDTH ** -0.5,
        'w_out': jax.random.normal(ks[16], (DEPTH, D_MODEL, D_MODEL), f32) * D_MODEL ** -0.5,
        'final_norm_w': 1.0 + 0.02 * jax.random.normal(ks[17], (D_MODEL,), f32),
    }


def reference(x, c, ada_w, ada_b, norm_w, w_in, swa_sinks, pool_w, pool_scale, conv_w, conv_b,
              dt_bias, a_log, d_skip, ssm_norm_w, w_up, w_out, final_norm_w):
    for l in range(DEPTH):
        x = hybrid_layer(x, c, ada_w[l], ada_b[l], norm_w[l], w_in[l], swa_sinks[l], pool_w[l],
                         pool_scale[l], conv_w[l], conv_b[l], dt_bias[l], a_log[l], d_skip[l],
                         ssm_norm_w[l], w_up[l], w_out[l])
    return rms_norm(x, final_norm_w)
```

```python
import functools
import math

import jax
import jax.numpy as jnp
import numpy as np
from jax import lax
from jax.experimental import pallas as pl
from jax.experimental.pallas import tpu as pltpu

D_MODEL = 1024
BRANCH = 512
N_BRANCHES = 4
EPS = 1e-6
CHUNK = 128

RET_HEADS, RET_DK, RET_DV = 4, 64, 128
SWA_HQ, SWA_HKV, SWA_D = 8, 2, 64
SWA_GROUP = SWA_HQ // SWA_HKV
POOL_WINDOWS = (2, 4, 8, 16)
POOL_GD = BRANCH // len(POOL_WINDOWS)
POOL_HIST = 16
M_HEADS, M_P, M_GROUPS, M_N, M_CONV = 8, 64, 2, 128, 4
M_R = M_HEADS // M_GROUPS
M_BC = M_GROUPS * M_N
CONV_HIST = 8
DT_PAD = 128

_SIZES = (256, 256, 512, 512, 512, 128, 128, 512, 512, 512, 1024, 512, M_HEADS, N_BRANCHES * D_MODEL)
_OFFS = np.concatenate([[0], np.cumsum(_SIZES)]).tolist()
RET_COLS = (_OFFS[0], _OFFS[4])
SWA_COLS = (_OFFS[4], _OFFS[8])
POOL_COLS = (_OFFS[8], _OFFS[10])
SSD_COLS = (_OFFS[10], _OFFS[12])
DT_COLS = (_OFFS[12], _OFFS[13])
MG_COLS = (_OFFS[13], _OFFS[14])

ROW_TILE = 512
V7X_VMEM_LIMIT = 56 * 1024 * 1024

_F32 = jnp.float32
_BF16 = jnp.bfloat16


def _sigmoid(x):
    return 1.0 / (1.0 + jnp.exp(-x))


def _silu(x):
    return x * _sigmoid(x)


def _softplus(x):
    return jnp.maximum(x, 0.0) + jnp.log(1.0 + jnp.exp(-jnp.abs(x)))


def _dot(a, b):
    return jnp.dot(a, b, preferred_element_type=_F32)


def _dot_nt(a, b):
    return lax.dot_general(a, b, (((1,), (1,)), ((), ())), preferred_element_type=_F32)


def _dot_tn(a, b):
    return lax.dot_general(a, b, (((0,), (0,)), ((), ())), preferred_element_type=_F32)


def _const_spec(shape):
    nd = len(shape)
    return pl.BlockSpec(shape, lambda b, t: (0,) * nd)


def _row_spec(width, tile=ROW_TILE):
    return pl.BlockSpec((1, tile, width), lambda b, t: (b, t, 0))


def _params():
    return pltpu.CompilerParams(dimension_semantics=("parallel", "arbitrary"),
                                vmem_limit_bytes=V7X_VMEM_LIMIT)


def _modulated_norm(x, mod_ref, nw_ref):
    shift = mod_ref[0, :, 0:D_MODEL]
    scale = mod_ref[0, :, D_MODEL:2 * D_MODEL]
    rs = lax.rsqrt(jnp.mean(x * x, axis=-1, keepdims=True) + EPS)
    return (x * rs) * nw_ref[...] * (1.0 + scale) + shift


def _mod_kernel(c_ref, w_ref, b_ref, o_ref):
    sc = _silu(c_ref[...]).astype(_BF16)
    o_ref[0] = _dot(sc, w_ref[0].astype(_BF16)) + b_ref[0]


def _modulation(c, ada_w, ada_b):
    depth, _, n = ada_w.shape
    bn = D_MODEL
    batch = c.shape[0]
    return pl.pallas_call(
        _mod_kernel,
        grid=(depth, n // bn),
        in_specs=[pl.BlockSpec((batch, D_MODEL), lambda l, j: (0, 0)),
                  pl.BlockSpec((1, D_MODEL, bn), lambda l, j: (l, 0, j)),
                  pl.BlockSpec((1, 1, bn), lambda l, j: (l, 0, j))],
        out_specs=pl.BlockSpec((1, batch, bn), lambda l, j: (l, 0, j)),
        out_shape=jax.ShapeDtypeStruct((depth, batch, n), _F32),
        name="adaln_modulation",
    )(c, ada_w, ada_b.reshape(depth, 1, n))


def _proj_kernel(x_ref, mod_ref, nw_ref, wr_ref, ws_ref, wp_ref, wm_ref, wdt_ref,
                 ret_ref, swa_ref, pool_ref, ssd_ref, dt_ref):
    h = _modulated_norm(x_ref[0], mod_ref, nw_ref).astype(_BF16)
    ret_ref[0] = _dot(h, wr_ref[...]).astype(_BF16)
    swa_ref[0] = _dot(h, ws_ref[...]).astype(_BF16)
    pool_ref[0] = _dot(h, wp_ref[...]).astype(_BF16)
    ssd_ref[0] = _dot(h, wm_ref[...]).astype(_BF16)
    dt_ref[0] = _dot(h, wdt_ref[...])


def _projection(x, mod, norm_w, wr, ws, wp, wm, wdt):
    batch, seq, _ = x.shape
    widths = (wr.shape[1], ws.shape[1], wp.shape[1], wm.shape[1])
    return pl.pallas_call(
        _proj_kernel,
        grid=(batch, seq // ROW_TILE),
        in_specs=[_row_spec(D_MODEL),
                  pl.BlockSpec((1, 1, 3 * D_MODEL), lambda b, t: (b, 0, 0)),
                  _const_spec((1, D_MODEL)),
                  _const_spec(wr.shape), _const_spec(ws.shape), _const_spec(wp.shape),
                  _const_spec(wm.shape), _const_spec(wdt.shape)],
        out_specs=[_row_spec(w) for w in widths] + [_row_spec(DT_PAD)],
        out_shape=[jax.ShapeDtypeStruct((batch, seq, w), _BF16) for w in widths]
        + [jax.ShapeDtypeStruct((batch, seq, DT_PAD), _F32)],
        compiler_params=_params(),
        name="norm_in_proj",
    )(x, mod, norm_w, wr, ws, wp, wm, wdt)


def _retention_tables():
    log_g = np.log(1.0 - 2.0 ** (-5.0 - np.arange(RET_HEADS, dtype=np.float64)))
    pos = np.arange(CHUNK, dtype=np.float64)
    diff = pos[:, None] - pos[None, :]
    inner = np.where(diff >= 0, np.exp(log_g[:, None, None] * np.where(diff >= 0, diff, 0.0)), 0.0)
    q_decay = np.exp(log_g[:, None] * (pos + 1.0))[:, :, None]
    k_decay = np.exp(log_g[:, None] * (CHUNK - 1.0 - pos))[:, :, None]
    chunk_decay = np.exp(log_g * CHUNK)
    return (jnp.asarray(inner, _F32), jnp.asarray(q_decay, _F32), jnp.asarray(k_decay, _F32),
            [float(v) for v in chunk_decay])


def _retention_kernel(chunk_decay, p_ref, inner_ref, qd_ref, kd_ref, o_ref, state_ref):
    @pl.when(pl.program_id(1) == 0)
    def _():
        state_ref[...] = jnp.zeros_like(state_ref)

    def chunk(ci, carry):
        r0 = pl.multiple_of(ci * CHUNK, CHUNK)
        rows = pl.ds(r0, CHUNK)
        for hd in range(RET_HEADS):
            q = p_ref[0, rows, hd * RET_DK:(hd + 1) * RET_DK]
            k = p_ref[0, rows, 256 + hd * RET_DK:256 + (hd + 1) * RET_DK].astype(_F32) * (RET_DK ** -0.5)
            v = p_ref[0, rows, 512 + hd * RET_DV:512 + (hd + 1) * RET_DV]
            gate = p_ref[0, rows, 1024 + hd * RET_DV:1024 + (hd + 1) * RET_DV].astype(_F32)
            state = state_ref[hd]
            scores = _dot_nt(q, k.astype(_BF16)) * inner_ref[hd]
            inner = _dot(scores.astype(_BF16), v)
            cross = _dot(q, state.astype(_BF16)) * qd_ref[hd]
            state_ref[hd] = state * chunk_decay[hd] + _dot_tn((k * kd_ref[hd]).astype(_BF16), v)
            out = inner + cross
            out = out * lax.rsqrt(jnp.mean(out * out, axis=-1, keepdims=True) + EPS)
            o_ref[0, rows, hd * RET_DV:(hd + 1) * RET_DV] = (out * _silu(gate)).astype(_BF16)
        return carry

    lax.fori_loop(0, ROW_TILE // CHUNK, chunk, 0)


def _retention(p):
    batch, seq, width = p.shape
    inner, q_decay, k_decay, chunk_decay = _retention_tables()
    return pl.pallas_call(
        functools.partial(_retention_kernel, chunk_decay),
        grid=(batch, seq // ROW_TILE),
        in_specs=[_row_spec(width), _const_spec(inner.shape), _const_spec(q_decay.shape),
                  _const_spec(k_decay.shape)],
        out_specs=_row_spec(BRANCH),
        out_shape=jax.ShapeDtypeStruct((batch, seq, BRANCH), _BF16),
        scratch_shapes=[pltpu.VMEM((RET_HEADS, RET_DK, RET_DV), _F32)],
        compiler_params=_params(),
        name="retention",
    )(p, inner, q_decay, k_decay)


def _swa_bias():
    slopes = 2.0 ** (-8.0 * np.arange(1, SWA_HQ + 1, dtype=np.float64) / SWA_HQ)
    qi = np.arange(CHUNK)
    kj = np.arange(2 * CHUNK)
    delta = CHUNK + qi[:, None] - kj[None, :]
    valid = (delta >= 0) & (delta < CHUNK)
    bias = np.where(valid[None], -slopes[:, None, None] * delta[None].astype(np.float64), -np.inf)
    return jnp.asarray(bias.reshape(SWA_HKV, SWA_GROUP * CHUNK, 2 * CHUNK), _F32)


def _swa_kernel(p_ref, bias_ref, sink_ref, o_ref, k_buf, v_buf, q_buf, att_buf):
    first_tile = pl.program_id(1) == 0
    kv_w = SWA_HKV * SWA_D

    @pl.when(first_tile)
    def _():
        k_buf[0:CHUNK, :] = jnp.zeros((CHUNK, kv_w), _BF16)
        v_buf[0:CHUNK, :] = jnp.zeros((CHUNK, kv_w), _BF16)

    k_buf[CHUNK:CHUNK + ROW_TILE, :] = p_ref[0, :, 512:512 + kv_w]
    v_buf[CHUNK:CHUNK + ROW_TILE, :] = p_ref[0, :, 512 + kv_w:512 + 2 * kv_w]
    key_in_prev_block = lax.broadcasted_iota(jnp.int32, (SWA_GROUP * CHUNK, 2 * CHUNK), 1) < CHUNK

    def block(bi, carry):
        r0 = pl.multiple_of(bi * CHUNK, CHUNK)
        rows = pl.ds(r0, CHUNK)
        keys = pl.ds(r0, 2 * CHUNK)
        no_prev = jnp.logical_and(first_tile, bi == 0)
        for j in range(SWA_HKV):
            for g in range(SWA_GROUP):
                hd = j * SWA_GROUP + g
                q = p_ref[0, rows, hd * SWA_D:(hd + 1) * SWA_D].astype(_F32) * (SWA_D ** -0.5)
                q_buf[g * CHUNK:(g + 1) * CHUNK, :] = q.astype(_BF16)
            k = k_buf[keys, j * SWA_D:(j + 1) * SWA_D]
            v = v_buf[keys, j * SWA_D:(j + 1) * SWA_D]
            s = _dot_nt(q_buf[...], k) + bias_ref[j]
            s = jnp.where(jnp.logical_and(no_prev, key_in_prev_block), -jnp.inf, s)
            sink = sink_ref[j]
            m = jnp.maximum(jnp.max(s, axis=-1, keepdims=True), sink)
            p = jnp.exp(s - m)
            denom = jnp.sum(p, axis=-1, keepdims=True) + jnp.exp(sink - m)
            o = _dot(p.astype(_BF16), v) / denom
            for g in range(SWA_GROUP):
                hd = j * SWA_GROUP + g
                att_buf[:, hd * SWA_D:(hd + 1) * SWA_D] = o[g * CHUNK:(g + 1) * CHUNK, :]
        gate = p_ref[0, rows, 768:768 + BRANCH].astype(_F32)
        o_ref[0, rows, :] = (att_buf[...] * _silu(gate)).astype(_BF16)
        return carry

    lax.fori_loop(0, ROW_TILE // CHUNK, block, 0)
    k_buf[0:CHUNK, :] = k_buf[ROW_TILE:ROW_TILE + CHUNK, :]
    v_buf[0:CHUNK, :] = v_buf[ROW_TILE:ROW_TILE + CHUNK, :]


def _swa(p, sinks):
    batch, seq, width = p.shape
    bias = _swa_bias()
    sink_col = jnp.repeat(sinks.astype(_F32), CHUNK).reshape(SWA_HKV, SWA_GROUP * CHUNK, 1)
    kv_w = SWA_HKV * SWA_D
    return pl.pallas_call(
        _swa_kernel,
        grid=(batch, seq // ROW_TILE),
        in_specs=[_row_spec(width), _const_spec(bias.shape), _const_spec(sink_col.shape)],
        out_specs=_row_spec(BRANCH),
        out_shape=jax.ShapeDtypeStruct((batch, seq, BRANCH), _BF16),
        scratch_shapes=[pltpu.VMEM((CHUNK + ROW_TILE, kv_w), _BF16),
                        pltpu.VMEM((CHUNK + ROW_TILE, kv_w), _BF16),
                        pltpu.VMEM((SWA_GROUP * CHUNK, SWA_D), _BF16),
                        pltpu.VMEM((CHUNK, BRANCH), _F32)],
        compiler_params=_params(),
        name="sliding_window_attention",
    )(p, bias, sink_col)


def _pool_kernel(p_ref, w_ref, scale_ref, o_ref, hist):
    tile = pl.program_id(1)

    @pl.when(tile == 0)
    def _():
        hist[0:POOL_HIST, :] = jnp.zeros((POOL_HIST, BRANCH), _F32)

    hist[POOL_HIST:POOL_HIST + ROW_TILE, :] = p_ref[0, :, 0:BRANCH].astype(_F32)
    pos = tile * ROW_TILE + lax.broadcasted_iota(jnp.int32, (ROW_TILE, 1), 0)
    for g, w in enumerate(POOL_WINDOWS):
        lanes = slice(g * POOL_GD, (g + 1) * POOL_GD)
        u = hist[POOL_HIST:POOL_HIST + ROW_TILE, lanes]
        acc = u
        for j in range(1, w):
            acc = acc + hist[POOL_HIST - j:POOL_HIST - j + ROW_TILE, lanes]
        count = jnp.minimum(pos + 1, w).astype(_F32)
        diff = acc / count - u
        y = _dot(diff.astype(_BF16), w_ref[g])
        gate = p_ref[0, :, BRANCH + g * POOL_GD:BRANCH + (g + 1) * POOL_GD].astype(_F32)
        o_ref[0, :, lanes] = (y * scale_ref[:, lanes] * _silu(gate)).astype(_BF16)
    hist[0:POOL_HIST, :] = hist[ROW_TILE:ROW_TILE + POOL_HIST, :]


def _pool(p, pool_w, pool_scale):
    batch, seq, width = p.shape
    return pl.pallas_call(
        _pool_kernel,
        grid=(batch, seq // ROW_TILE),
        in_specs=[_row_spec(width), _const_spec(pool_w.shape), _const_spec((1, BRANCH))],
        out_specs=_row_spec(BRANCH),
        out_shape=jax.ShapeDtypeStruct((batch, seq, BRANCH), _BF16),
        scratch_shapes=[pltpu.VMEM((POOL_HIST + ROW_TILE, BRANCH), _F32)],
        compiler_params=_params(),
        name="multiscale_pool",
    )(p, pool_w, pool_scale.reshape(1, BRANCH))


def _ssd_kernel(p_ref, dt_ref, cw_ref, cb_ref, dtb_ref, alog_ref, dskip_ref, nw_ref, tri_ref, o_ref,
                hist, xbc_buf, xd_buf, y_buf, state_ref):
    conv_w = M_P * M_HEADS + 2 * M_BC

    @pl.when(pl.program_id(1) == 0)
    def _():
        hist[0:CONV_HIST, :] = jnp.zeros((CONV_HIST, conv_w), _F32)
        state_ref[...] = jnp.zeros_like(state_ref)

    hist[CONV_HIST:CONV_HIST + ROW_TILE, :] = p_ref[0, :, 0:conv_w].astype(_F32)
    acc = cb_ref[...] + cw_ref[M_CONV - 1:M_CONV, :] * hist[CONV_HIST:CONV_HIST + ROW_TILE, :]
    for j in range(1, M_CONV):
        acc = acc + cw_ref[M_CONV - 1 - j:M_CONV - j, :] * hist[CONV_HIST - j:CONV_HIST - j + ROW_TILE, :]
    xbc_buf[...] = _silu(acc)
    hist[0:CONV_HIST, :] = hist[ROW_TILE:ROW_TILE + CONV_HIST, :]

    a_row = -jnp.exp(alog_ref[...])
    row = lax.broadcasted_iota(jnp.int32, (CHUNK, CHUNK), 0)
    col = lax.broadcasted_iota(jnp.int32, (CHUNK, CHUNK), 1)
    causal = row >= col
    bc0 = M_HEADS * M_P

    def chunk(ci, carry):
        r0 = pl.multiple_of(ci * CHUNK, CHUNK)
        rows = pl.ds(r0, CHUNK)
        dt = _softplus(dt_ref[0, rows, :] + dtb_ref[...])
        acs = jnp.dot(tri_ref[...], dt * a_row, preferred_element_type=_F32, precision=lax.Precision.HIGHEST)
        acs_t = acs.T
        exp_acs = jnp.exp(acs)
        last = acs[CHUNK - 1:CHUNK, :]
        dend = jnp.exp(last - acs)
        exp_last = jnp.exp(last)
        for g in range(M_GROUPS):
            bmat = xbc_buf[rows, bc0 + g * M_N:bc0 + (g + 1) * M_N].astype(_BF16)
            cmat = xbc_buf[rows, bc0 + M_BC + g * M_N:bc0 + M_BC + (g + 1) * M_N].astype(_BF16)
            cb = _dot_nt(cmat, bmat)
            state = state_ref[g]
            y_off = _dot(cmat, state.astype(_BF16))
            for r in range(M_R):
                hd = g * M_R + r
                lanes = slice(r * M_P, (r + 1) * M_P)
                xh = xbc_buf[rows, hd * M_P:(hd + 1) * M_P]
                xdt = xh * dt[:, hd:hd + 1]
                seg = acs[:, hd:hd + 1] - acs_t[hd:hd + 1, :]
                lmat = jnp.exp(jnp.where(causal, seg, -jnp.inf))
                y_diag = _dot((cb * lmat).astype(_BF16), xdt.astype(_BF16))
                y_buf[:, lanes] = y_diag + y_off[:, lanes] * exp_acs[:, hd:hd + 1] + xh * dskip_ref[:, hd:hd + 1]
                xd_buf[:, lanes] = (xdt * dend[:, hd:hd + 1]).astype(_BF16)
                state_ref[g, :, lanes] = state[:, lanes] * exp_last[:, hd:hd + 1]
            state_ref[g] = state_ref[g] + _dot_tn(bmat, xd_buf[...])
            gw = M_R * M_P
            z = p_ref[0, rows, conv_w + g * gw:conv_w + (g + 1) * gw].astype(_F32)
            yz = y_buf[...] * _silu(z)
            yz = yz * lax.rsqrt(jnp.mean(yz * yz, axis=-1, keepdims=True) + EPS)
            o_ref[0, rows, g * gw:(g + 1) * gw] = (yz * nw_ref[:, g * gw:(g + 1) * gw]).astype(_BF16)
        return carry

    lax.fori_loop(0, ROW_TILE // CHUNK, chunk, 0)


def _pad_lanes(v, width=DT_PAD):
    return jnp.pad(v.astype(_F32), (0, width - v.shape[0])).reshape(1, width)


def _ssd(p, dt, conv_w, conv_b, dt_bias, a_log, d_skip, ssm_norm_w):
    batch, seq, width = p.shape
    cw = conv_w.shape[1]
    tri = jnp.asarray(np.tril(np.ones((CHUNK, CHUNK), np.float32)))
    gw = M_R * M_P
    return pl.pallas_call(
        _ssd_kernel,
        grid=(batch, seq // ROW_TILE),
        in_specs=[_row_spec(width), _row_spec(DT_PAD), _const_spec((M_CONV, cw)), _const_spec((1, cw)),
                  _const_spec((1, DT_PAD)), _const_spec((1, DT_PAD)), _const_spec((1, DT_PAD)),
                  _const_spec((1, BRANCH)), _const_spec((CHUNK, CHUNK))],
        out_specs=_row_spec(BRANCH),
        out_shape=jax.ShapeDtypeStruct((batch, seq, BRANCH), _BF16),
        scratch_shapes=[pltpu.VMEM((CONV_HIST + ROW_TILE, cw), _F32),
                        pltpu.VMEM((ROW_TILE, cw), _F32),
                        pltpu.VMEM((CHUNK, gw), _BF16),
                        pltpu.VMEM((CHUNK, gw), _F32),
                        pltpu.VMEM((M_GROUPS, M_N, gw), _F32)],
        compiler_params=_params(),
        name="ssd_scan",
    )(p, dt, conv_w, conv_b.reshape(1, cw), _pad_lanes(dt_bias), _pad_lanes(a_log), _pad_lanes(d_skip),
      ssm_norm_w.reshape(1, BRANCH), tri)


def _merge_kernel(final, x_ref, mod_ref, nw_ref, wg_ref, ret_ref, att_ref, pool_ref, ssm_ref, wup_ref, wout_ref,
                  *rest):
    if final:
        fnw_ref, o_ref = rest
    else:
        (o_ref,) = rest
    x = x_ref[0]
    h = _modulated_norm(x, mod_ref, nw_ref).astype(_BF16)
    merged = None
    for i, br_ref in enumerate((ret_ref, att_ref, pool_ref, ssm_ref)):
        gate = _sigmoid(_dot(h, wg_ref[:, i * D_MODEL:(i + 1) * D_MODEL]))
        term = gate * _dot(br_ref[0], wup_ref[i])
        merged = term if merged is None else merged + term
    out = _dot(merged.astype(_BF16), wout_ref[...])
    y = x + mod_ref[0, :, 2 * D_MODEL:3 * D_MODEL] * out
    if final:
        y = y * lax.rsqrt(jnp.mean(y * y, axis=-1, keepdims=True) + EPS) * fnw_ref[...]
    o_ref[0] = y


def _merge(x, mod, norm_w, wg, branches, w_up, w_out, final_norm_w):
    batch, seq, _ = x.shape
    final = final_norm_w is not None
    in_specs = [_row_spec(D_MODEL),
                pl.BlockSpec((1, 1, 3 * D_MODEL), lambda b, t: (b, 0, 0)),
                _const_spec((1, D_MODEL)), _const_spec(wg.shape)]
    in_specs += [_row_spec(BRANCH)] * N_BRANCHES
    in_specs += [_const_spec(w_up.shape), _const_spec(w_out.shape)]
    args = [x, mod, norm_w, wg, *branches, w_up, w_out]
    if final:
        in_specs.append(_const_spec((1, D_MODEL)))
        args.append(final_norm_w.reshape(1, D_MODEL))
    return pl.pallas_call(
        functools.partial(_merge_kernel, final),
        grid=(batch, seq // ROW_TILE),
        in_specs=in_specs,
        out_specs=_row_spec(D_MODEL),
        out_shape=jax.ShapeDtypeStruct(x.shape, _F32),
        compiler_params=_params(),
        name="merge_out_proj",
    )(*args)


def kernel(x, c, ada_w, ada_b, norm_w, w_in, swa_sinks, pool_w, pool_scale, conv_w, conv_b, dt_bias, a_log,
           d_skip, ssm_norm_w, w_up, w_out, final_norm_w):
    depth = ada_w.shape[0]
    batch = x.shape[0]
    mod_all = _modulation(c, ada_w, ada_b)
    for l in range(depth):
        mod = mod_all[l].reshape(batch, 1, 3 * D_MODEL)
        nw = norm_w[l].reshape(1, D_MODEL)
        w = w_in[l]
        wr, ws, wp, wm = (w[:, a:b].astype(_BF16) for a, b in (RET_COLS, SWA_COLS, POOL_COLS, SSD_COLS))
        wdt = jnp.pad(w[:, DT_COLS[0]:DT_COLS[1]], ((0, 0), (0, DT_PAD - M_HEADS))).astype(_BF16)
        wg = w[:, MG_COLS[0]:MG_COLS[1]].astype(_BF16)
        ret_p, swa_p, pool_p, ssd_p, dt = _projection(x, mod, nw, wr, ws, wp, wm, wdt)
        ret = _retention(ret_p)
        att = _swa(swa_p, swa_sinks[l])
        pool = _pool(pool_p, pool_w[l].astype(_BF16), pool_scale[l])
        ssm = _ssd(ssd_p, dt, conv_w[l], conv_b[l], dt_bias[l], a_log[l], d_skip[l], ssm_norm_w[l])
        x = _merge(x, mod, nw, wg, (ret, att, pool, ssm), w_up[l].astype(_BF16), w_out[l].astype(_BF16),
                   final_norm_w if l == depth - 1 else None)
    return x
```

```python
import functools

import jax
import jax.numpy as jnp
import numpy as np
from jax import lax
from jax.experimental import pallas as pl
from jax.experimental.pallas import tpu as pltpu

D_MODEL = 1024
BRANCH = 512
N_BRANCHES = 4
EPS = 1e-6
CHUNK = 128
LANES = 128

RET_HEADS, RET_DK, RET_DV = 4, 64, 128
SWA_HQ, SWA_HKV, SWA_D = 8, 2, 64
SWA_GROUP = SWA_HQ // SWA_HKV
POOL_WINDOWS = (2, 4, 8, 16)
POOL_GD = BRANCH // len(POOL_WINDOWS)
POOL_HIST = 16
M_HEADS, M_P, M_GROUPS, M_N, M_CONV = 8, 64, 2, 128, 4
M_R = M_HEADS // M_GROUPS
M_GW = M_R * M_P
M_BC = M_GROUPS * M_N
CONV_HIST = 8
DT_PAD = LANES

_SIZES = (256, 256, 512, 512, 512, 128, 128, 512, 512, 512, 1024, 512, M_HEADS, N_BRANCHES * D_MODEL)
_OFFS = np.concatenate([[0], np.cumsum(_SIZES)]).tolist()
RET_COLS = (_OFFS[0], _OFFS[4])
SWA_COLS = (_OFFS[4], _OFFS[8])
POOL_COLS = (_OFFS[8], _OFFS[10])
SSD_COLS = (_OFFS[10], _OFFS[12])
DT_COLS = (_OFFS[12], _OFFS[13])
MG_COLS = (_OFFS[13], _OFFS[14])

ROW_TILE = 512
V7X_VMEM_LIMIT = 56 * 1024 * 1024

_F32 = jnp.float32
_BF16 = jnp.bfloat16


def _sigmoid(x):
    return 1.0 / (1.0 + jnp.exp(-x))


def _silu(x):
    return x * _sigmoid(x)


def _softplus(x):
    return jnp.maximum(x, 0.0) + jnp.log(1.0 + jnp.exp(-jnp.abs(x)))


def _dot(a, b):
    return jnp.dot(a, b, preferred_element_type=_F32)


def _dot_nt(a, b):
    return lax.dot_general(a, b, (((1,), (1,)), ((), ())), preferred_element_type=_F32)


def _dot_tn(a, b):
    return lax.dot_general(a, b, (((0,), (0,)), ((), ())), preferred_element_type=_F32)


def _const_spec(shape):
    nd = len(shape)
    return pl.BlockSpec(shape, lambda b, t: (0,) * nd)


def _row_spec(width, tile=ROW_TILE):
    return pl.BlockSpec((1, tile, width), lambda b, t: (b, t, 0))


def _params():
    return pltpu.CompilerParams(dimension_semantics=("parallel", "arbitrary"),
                                vmem_limit_bytes=V7X_VMEM_LIMIT)


def _modulated_norm(x, mod_ref, nw_ref):
    shift = mod_ref[0, :, 0:D_MODEL]
    scale = mod_ref[0, :, D_MODEL:2 * D_MODEL]
    rs = lax.rsqrt(jnp.mean(x * x, axis=-1, keepdims=True) + EPS)
    return (x * rs) * nw_ref[...] * (1.0 + scale) + shift


def _mod_kernel(c_ref, w_ref, b_ref, o_ref):
    sc = _silu(c_ref[...]).astype(_BF16)
    o_ref[0] = _dot(sc, w_ref[0].astype(_BF16)) + b_ref[0]


def _modulation(c, ada_w, ada_b):
    depth, _, n = ada_w.shape
    bn = D_MODEL
    batch = c.shape[0]
    return pl.pallas_call(
        _mod_kernel,
        grid=(depth, n // bn),
        in_specs=[pl.BlockSpec((batch, D_MODEL), lambda l, j: (0, 0)),
                  pl.BlockSpec((1, D_MODEL, bn), lambda l, j: (l, 0, j)),
                  pl.BlockSpec((1, 1, bn), lambda l, j: (l, 0, j))],
        out_specs=pl.BlockSpec((1, batch, bn), lambda l, j: (l, 0, j)),
        out_shape=jax.ShapeDtypeStruct((depth, batch, n), _F32),
        name="adaln_modulation",
    )(c, ada_w, ada_b.reshape(depth, 1, n))


def _proj_kernel(x_ref, mod_ref, nw_ref, wr_ref, ws_ref, wp_ref, wm_ref, wdt_ref,
                 ret_ref, swa_ref, pool_ref, ssd_ref, dt_ref):
    h = _modulated_norm(x_ref[0], mod_ref, nw_ref).astype(_BF16)
    ret_ref[0] = _dot(h, wr_ref[...]).astype(_BF16)
    swa_ref[0] = _dot(h, ws_ref[...]).astype(_BF16)
    pool_ref[0] = _dot(h, wp_ref[...]).astype(_BF16)
    ssd_ref[0] = _dot(h, wm_ref[...]).astype(_BF16)
    dt_ref[0] = _dot(h, wdt_ref[...])


def _projection(x, mod, norm_w, wr, ws, wp, wm, wdt):
    batch, seq, _ = x.shape
    widths = (wr.shape[1], ws.shape[1], wp.shape[1], wm.shape[1])
    return pl.pallas_call(
        _proj_kernel,
        grid=(batch, seq // ROW_TILE),
        in_specs=[_row_spec(D_MODEL),
                  pl.BlockSpec((1, 1, 3 * D_MODEL), lambda b, t: (b, 0, 0)),
                  _const_spec((1, D_MODEL)),
                  _const_spec(wr.shape), _const_spec(ws.shape), _const_spec(wp.shape),
                  _const_spec(wm.shape), _const_spec(wdt.shape)],
        out_specs=[_row_spec(w) for w in widths] + [_row_spec(DT_PAD)],
        out_shape=[jax.ShapeDtypeStruct((batch, seq, w), _BF16) for w in widths]
        + [jax.ShapeDtypeStruct((batch, seq, DT_PAD), _F32)],
        compiler_params=_params(),
        name="norm_in_proj",
    )(x, mod, norm_w, wr, ws, wp, wm, wdt)


def _retention_tables():
    log_g = np.log(1.0 - 2.0 ** (-5.0 - np.arange(RET_HEADS, dtype=np.float64)))
    pos = np.arange(CHUNK, dtype=np.float64)
    diff = pos[:, None] - pos[None, :]
    inner = np.where(diff >= 0, np.exp(log_g[:, None, None] * np.where(diff >= 0, diff, 0.0)), 0.0)
    q_decay = np.repeat(np.exp(log_g[:, None] * (pos + 1.0)).T, RET_DV, axis=1)
    k_decay = np.repeat(np.exp(log_g[:, None] * (CHUNK - 1.0 - pos)).T, RET_DK, axis=1)
    chunk_decay = np.repeat(np.exp(log_g * CHUNK), RET_DV)[None, :]
    block_diag = (np.arange(RET_HEADS * RET_DK)[:, None] // RET_DK
                  == np.arange(RET_HEADS * RET_DV)[None, :] // RET_DV)
    return tuple(jnp.asarray(t, _F32) for t in (inner, q_decay, k_decay, chunk_decay, block_diag))


def _retention_kernel(p_ref, inner_ref, qd_ref, kd_ref, cd_ref, bd_ref, o_ref, state_ref):
    qk_w = RET_HEADS * RET_DK
    v_w = RET_HEADS * RET_DV

    @pl.when(pl.program_id(1) == 0)
    def _():
        state_ref[...] = jnp.zeros_like(state_ref)

    head_of_lane = lax.broadcasted_iota(jnp.int32, (CHUNK, qk_w), 1) // RET_DK

    def chunk(ci, carry):
        rows = pl.ds(pl.multiple_of(ci * CHUNK, CHUNK), CHUNK)
        q = p_ref[0, rows, 0:qk_w]
        k = p_ref[0, rows, qk_w:2 * qk_w].astype(_F32) * (RET_DK ** -0.5)
        k_bf = k.astype(_BF16)
        v = p_ref[0, rows, 2 * qk_w:2 * qk_w + v_w]
        state = state_ref[...]
        cross = _dot(q, state.astype(_BF16)) * qd_ref[...]
        update = _dot_tn((k * kd_ref[...]).astype(_BF16), v)
        state_ref[...] = state * cd_ref[...] + update * bd_ref[...]
        for hd in range(RET_HEADS):
            lanes = slice(hd * RET_DV, (hd + 1) * RET_DV)
            q_h = jnp.where(head_of_lane == hd, q, jnp.zeros_like(q))
            scores = _dot_nt(q_h, k_bf) * inner_ref[hd]
            out = _dot(scores.astype(_BF16), v[:, lanes]) + cross[:, lanes]
            out = out * lax.rsqrt(jnp.mean(out * out, axis=-1, keepdims=True) + EPS)
            gate = p_ref[0, rows, 2 * qk_w + v_w + hd * RET_DV:2 * qk_w + v_w + (hd + 1) * RET_DV].astype(_F32)
            o_ref[0, rows, lanes] = (out * _silu(gate)).astype(_BF16)
        return carry

    lax.fori_loop(0, ROW_TILE // CHUNK, chunk, 0, unroll=True)


def _retention(p):
    batch, seq, width = p.shape
    tables = _retention_tables()
    return pl.pallas_call(
        _retention_kernel,
        grid=(batch, seq // ROW_TILE),
        in_specs=[_row_spec(width)] + [_const_spec(t.shape) for t in tables],
        out_specs=_row_spec(BRANCH),
        out_shape=jax.ShapeDtypeStruct((batch, seq, BRANCH), _BF16),
        scratch_shapes=[pltpu.VMEM((RET_HEADS * RET_DK, RET_HEADS * RET_DV), _F32)],
        compiler_params=_params(),
        name="retention",
    )(p, *tables)


def _swa_bias():
    slopes = 2.0 ** (-8.0 * np.arange(1, SWA_HQ + 1, dtype=np.float64) / SWA_HQ)
    qi = np.arange(CHUNK)
    kj = np.arange(2 * CHUNK)
    delta = CHUNK + qi[:, None] - kj[None, :]
    valid = (delta >= 0) & (delta < CHUNK)
    variants = []
    for ok in (valid, valid & (kj[None, :] >= CHUNK)):
        bias = np.where(ok[None], -slopes[:, None, None] * delta[None].astype(np.float64), -np.inf)
        variants.append(bias.reshape(SWA_HKV, SWA_GROUP * CHUNK, 2 * CHUNK))
    return jnp.asarray(np.stack(variants), _F32)


def _swa_kernel(p_ref, bias_ref, sink_ref, o_ref, kt_buf, v_buf):
    first_tile = pl.program_id(1) == 0
    kv_w = SWA_HKV * SWA_D
    q_w = SWA_HQ * SWA_D
    ext_w = 2 * LANES
    n_blocks = ROW_TILE // CHUNK

    @pl.when(first_tile)
    def _():
        kt_buf[:, :, 0:CHUNK] = jnp.zeros((SWA_HKV, LANES, CHUNK), _BF16)
        v_buf[0:CHUNK, :] = jnp.zeros((CHUNK, SWA_HKV * ext_w), _BF16)

    low_half = lax.broadcasted_iota(jnp.int32, (ROW_TILE, kv_w), 1) < SWA_D
    k = p_ref[0, :, q_w:q_w + kv_w].astype(_F32)
    k_swapped = pltpu.roll(k, SWA_D, axis=1)
    for j, dup in enumerate((jnp.where(low_half, k, k_swapped), jnp.where(low_half, k_swapped, k))):
        for bi in range(n_blocks):
            kt_buf[j, :, (bi + 1) * CHUNK:(bi + 2) * CHUNK] = dup[bi * CHUNK:(bi + 1) * CHUNK, :].T.astype(_BF16)
    v = p_ref[0, :, q_w + kv_w:q_w + 2 * kv_w].astype(_F32)
    v_swapped = pltpu.roll(v, SWA_D, axis=1)
    for j, dup in enumerate((jnp.where(low_half, v, v_swapped), jnp.where(low_half, v_swapped, v))):
        v_buf[CHUNK:CHUNK + ROW_TILE, j * ext_w:j * ext_w + LANES] = dup.astype(_BF16)
        v_buf[CHUNK:CHUNK + ROW_TILE, j * ext_w + LANES:(j + 1) * ext_w] = jnp.ones((ROW_TILE, LANES), _BF16)

    low_lanes = lax.broadcasted_iota(jnp.int32, (CHUNK, LANES), 1) < SWA_D
    high_lanes = jnp.logical_not(low_lanes)
    first_variant = first_tile.astype(jnp.int32)

    for bi in range(n_blocks):
        rows = slice(bi * CHUNK, (bi + 1) * CHUNK)
        keys = slice(bi * CHUNK, (bi + 2) * CHUNK)
        variant = first_variant if bi == 0 else 0
        for j in range(SWA_HKV):
            k_t = kt_buf[j, :, keys]
            v_ext = v_buf[keys, j * ext_w:(j + 1) * ext_w]
            for pair in range(SWA_GROUP // 2):
                c0 = (j * SWA_GROUP // 2 + pair) * LANES
                qp = p_ref[0, rows, c0:c0 + LANES].astype(_F32) * (SWA_D ** -0.5)
                halves = []
                for odd in range(2):
                    g_rows = slice((2 * pair + odd) * CHUNK, (2 * pair + odd + 1) * CHUNK)
                    q_h = jnp.where(high_lanes if odd else low_lanes, qp, 0.0).astype(_BF16)
                    s = _dot(q_h, k_t) + bias_ref[variant, j, g_rows, :]
                    s_prev, s_cur = s[:, 0:CHUNK], s[:, CHUNK:2 * CHUNK]
                    sink = sink_ref[j, g_rows, :]
                    m = jnp.maximum(jnp.max(jnp.maximum(s_prev, s_cur), axis=-1, keepdims=True), sink)
                    p = jnp.concatenate([jnp.exp(s_prev - m), jnp.exp(s_cur - m)], axis=1).astype(_BF16)
                    ov = _dot(p, v_ext)
                    halves.append(ov[:, 0:LANES] / (ov[:, LANES:2 * LANES] + jnp.exp(sink - m)))
                gate = p_ref[0, rows, q_w + 2 * kv_w + c0:q_w + 2 * kv_w + c0 + LANES].astype(_F32)
                o_ref[0, rows, c0:c0 + LANES] = (jnp.where(low_lanes, halves[0], halves[1])
                                                 * _silu(gate)).astype(_BF16)

    kt_buf[:, :, 0:CHUNK] = kt_buf[:, :, ROW_TILE:ROW_TILE + CHUNK]
    v_buf[0:CHUNK, :] = v_buf[ROW_TILE:ROW_TILE + CHUNK, :]


def _swa(p, sinks):
    batch, seq, width = p.shape
    bias = _swa_bias()
    sink_rows = jnp.broadcast_to(jnp.repeat(sinks.astype(_F32), CHUNK)[:, None],
                                 (SWA_HQ * CHUNK, LANES)).reshape(SWA_HKV, SWA_GROUP * CHUNK, LANES)
    return pl.pallas_call(
        _swa_kernel,
        grid=(batch, seq // ROW_TILE),
        in_specs=[_row_spec(width), _const_spec(bias.shape), _const_spec(sink_rows.shape)],
        out_specs=_row_spec(BRANCH),
        out_shape=jax.ShapeDtypeStruct((batch, seq, BRANCH), _BF16),
        scratch_shapes=[pltpu.VMEM((SWA_HKV, LANES, CHUNK + ROW_TILE), _BF16),
                        pltpu.VMEM((CHUNK + ROW_TILE, SWA_HKV * 2 * LANES), _BF16)],
        compiler_params=_params(),
        name="sliding_window_attention",
    )(p, bias, sink_rows)


def _pool_kernel(p_ref, w_ref, scale_ref, o_ref, hist):
    tile = pl.program_id(1)

    @pl.when(tile == 0)
    def _():
        hist[0:POOL_HIST, :] = jnp.zeros((POOL_HIST, BRANCH), _F32)

    hist[POOL_HIST:POOL_HIST + ROW_TILE, :] = p_ref[0, :, 0:BRANCH].astype(_F32)
    pos = tile * ROW_TILE + lax.broadcasted_iota(jnp.int32, (ROW_TILE, 1), 0)
    for g, w in enumerate(POOL_WINDOWS):
        lanes = slice(g * POOL_GD, (g + 1) * POOL_GD)
        u = hist[POOL_HIST:POOL_HIST + ROW_TILE, lanes]
        acc = u
        for j in range(1, w):
            acc = acc + hist[POOL_HIST - j:POOL_HIST - j + ROW_TILE, lanes]
        count = jnp.minimum(pos + 1, w).astype(_F32)
        diff = acc / count - u
        y = _dot(diff.astype(_BF16), w_ref[g])
        gate = p_ref[0, :, BRANCH + g * POOL_GD:BRANCH + (g + 1) * POOL_GD].astype(_F32)
        o_ref[0, :, lanes] = (y * scale_ref[:, lanes] * _silu(gate)).astype(_BF16)
    hist[0:POOL_HIST, :] = hist[ROW_TILE:ROW_TILE + POOL_HIST, :]


def _pool(p, pool_w, pool_scale):
    batch, seq, width = p.shape
    return pl.pallas_call(
        _pool_kernel,
        grid=(batch, seq // ROW_TILE),
        in_specs=[_row_spec(width), _const_spec(pool_w.shape), _const_spec((1, BRANCH))],
        out_specs=_row_spec(BRANCH),
        out_shape=jax.ShapeDtypeStruct((batch, seq, BRANCH), _BF16),
        scratch_shapes=[pltpu.VMEM((POOL_HIST + ROW_TILE, BRANCH), _F32)],
        compiler_params=_params(),
        name="multiscale_pool",
    )(p, pool_w, pool_scale.reshape(1, BRANCH))


def _expansion_matrix():
    e = np.zeros((2 * LANES, M_HEADS * M_P), np.float32)
    for hd in range(M_HEADS):
        e[hd, hd * M_P:(hd + 1) * M_P] = 1.0
        e[LANES + hd, hd * M_P:(hd + 1) * M_P] = 1.0
    return jnp.asarray(e, _BF16)


def _ssd_kernel(p_ref, dt_ref, cw_ref, cb_ref, dtb_ref, alog_ref, dskip_ref, nw_ref, tri_ref, exp_ref, o_ref,
                hist, xbc_buf, state_ref):
    conv_w = M_P * M_HEADS + 2 * M_BC

    @pl.when(pl.program_id(1) == 0)
    def _():
        hist[0:CONV_HIST, :] = jnp.zeros((CONV_HIST, conv_w), _F32)
        state_ref[...] = jnp.zeros_like(state_ref)

    hist[CONV_HIST:CONV_HIST + ROW_TILE, :] = p_ref[0, :, 0:conv_w].astype(_F32)
    acc = cb_ref[...] + cw_ref[M_CONV - 1:M_CONV, :] * hist[CONV_HIST:CONV_HIST + ROW_TILE, :]
    for j in range(1, M_CONV):
        acc = acc + cw_ref[M_CONV - 1 - j:M_CONV - j, :] * hist[CONV_HIST - j:CONV_HIST - j + ROW_TILE, :]
    xbc_buf[...] = _silu(acc)
    hist[0:CONV_HIST, :] = hist[ROW_TILE:ROW_TILE + CONV_HIST, :]

    a_row = -jnp.exp(alog_ref[...])
    causal = (lax.broadcasted_iota(jnp.int32, (CHUNK, CHUNK), 0)
              >= lax.broadcasted_iota(jnp.int32, (CHUNK, CHUNK), 1))
    head_of_lane = lax.broadcasted_iota(jnp.int32, (CHUNK, M_GW), 1) // M_P
    x_w = M_HEADS * M_P

    def expand(vals):
        stacked = jnp.concatenate(vals, axis=0)
        hi = stacked.astype(_BF16)
        lo = (stacked - hi.astype(_F32)).astype(_BF16)
        wide = _dot(jnp.concatenate([hi, lo], axis=1), exp_ref[...])
        return [wide[i * CHUNK:(i + 1) * CHUNK, :] for i in range(len(vals))]

    def chunk(ci, carry):
        rows = pl.ds(pl.multiple_of(ci * CHUNK, CHUNK), CHUNK)
        dt = _softplus(dt_ref[0, rows, :] + dtb_ref[...])
        acs = jnp.dot(tri_ref[...], dt * a_row, preferred_element_type=_F32, precision=lax.Precision.HIGHEST)
        acs_t = acs.T
        last = acs[CHUNK - 1:CHUNK, :]
        dt_x, w_x, ex_x = expand([dt, dt * jnp.exp(last - acs), jnp.exp(acs)])
        x = xbc_buf[rows, 0:x_w]
        xdt = (x * dt_x).astype(_BF16)
        xd = (x * w_x).astype(_BF16)
        for g in range(M_GROUPS):
            lanes = slice(g * M_GW, (g + 1) * M_GW)
            bmat = xbc_buf[rows, x_w + g * M_N:x_w + (g + 1) * M_N].astype(_BF16)
            cmat = xbc_buf[rows, x_w + M_BC + g * M_N:x_w + M_BC + (g + 1) * M_N].astype(_BF16)
            cb = _dot_nt(cmat, bmat)
            state = state_ref[g]
            ex_g = ex_x[:, lanes]
            y = _dot(cmat, state.astype(_BF16)) * ex_g + x[:, lanes] * dskip_ref[:, lanes]
            xdt_g = xdt[:, lanes]
            for r in range(M_R):
                hd = g * M_R + r
                seg = acs[:, hd:hd + 1] - acs_t[hd:hd + 1, :]
                lmat = jnp.exp(jnp.where(causal, seg, -jnp.inf))
                x_h = jnp.where(head_of_lane == r, xdt_g, jnp.zeros_like(xdt_g))
                y = y + _dot((cb * lmat).astype(_BF16), x_h)
            state_ref[g] = state * ex_g[CHUNK - 1:CHUNK, :] + _dot_tn(bmat, xd[:, lanes])
            z = p_ref[0, rows, conv_w + g * M_GW:conv_w + (g + 1) * M_GW].astype(_F32)
            yz = y * _silu(z)
            yz = yz * lax.rsqrt(jnp.mean(yz * yz, axis=-1, keepdims=True) + EPS)
            o_ref[0, rows, lanes] = (yz * nw_ref[:, lanes]).astype(_BF16)
        return carry

    lax.fori_loop(0, ROW_TILE // CHUNK, chunk, 0, unroll=True)


def _pad_lanes(v, width=DT_PAD):
    return jnp.pad(v.astype(_F32), (0, width - v.shape[0])).reshape(1, width)


def _ssd(p, dt, conv_w, conv_b, dt_bias, a_log, d_skip, ssm_norm_w):
    batch, seq, width = p.shape
    cw = conv_w.shape[1]
    tri = jnp.asarray(np.tril(np.ones((CHUNK, CHUNK), np.float32)))
    expansion = _expansion_matrix()
    d_skip_x = jnp.repeat(d_skip.astype(_F32), M_P).reshape(1, BRANCH)
    return pl.pallas_call(
        _ssd_kernel,
        grid=(batch, seq // ROW_TILE),
        in_specs=[_row_spec(width), _row_spec(DT_PAD), _const_spec((M_CONV, cw)), _const_spec((1, cw)),
                  _const_spec((1, DT_PAD)), _const_spec((1, DT_PAD)), _const_spec((1, BRANCH)),
                  _const_spec((1, BRANCH)), _const_spec((CHUNK, CHUNK)), _const_spec(expansion.shape)],
        out_specs=_row_spec(BRANCH),
        out_shape=jax.ShapeDtypeStruct((batch, seq, BRANCH), _BF16),
        scratch_shapes=[pltpu.VMEM((CONV_HIST + ROW_TILE, cw), _F32),
                        pltpu.VMEM((ROW_TILE, cw), _F32),
                        pltpu.VMEM((M_GROUPS, M_N, M_GW), _F32)],
        compiler_params=_params(),
        name="ssd_scan",
    )(p, dt, conv_w, conv_b.reshape(1, cw), _pad_lanes(dt_bias), _pad_lanes(a_log), d_skip_x,
      ssm_norm_w.reshape(1, BRANCH), tri, expansion)


def _merge_kernel(final, x_ref, mod_ref, nw_ref, wg_ref, ret_ref, att_ref, pool_ref, ssm_ref, wup_ref, wout_ref,
                  *rest):
    if final:
        fnw_ref, o_ref = rest
    else:
        (o_ref,) = rest
    x = x_ref[0]
    h = _modulated_norm(x, mod_ref, nw_ref).astype(_BF16)
    merged = None
    for i, br_ref in enumerate((ret_ref, att_ref, pool_ref, ssm_ref)):
        gate = _sigmoid(_dot(h, wg_ref[:, i * D_MODEL:(i + 1) * D_MODEL]))
        term = gate * _dot(br_ref[0], wup_ref[i])
        merged = term if merged is None else merged + term
    out = _dot(merged.astype(_BF16), wout_ref[...])
    y = x + mod_ref[0, :, 2 * D_MODEL:3 * D_MODEL] * out
    if final:
        y = y * lax.rsqrt(jnp.mean(y * y, axis=-1, keepdims=True) + EPS) * fnw_ref[...]
    o_ref[0] = y


def _merge(x, mod, norm_w, wg, branches, w_up, w_out, final_norm_w):
    batch, seq, _ = x.shape
    final = final_norm_w is not None
    in_specs = [_row_spec(D_MODEL),
                pl.BlockSpec((1, 1, 3 * D_MODEL), lambda b, t: (b, 0, 0)),
                _const_spec((1, D_MODEL)), _const_spec(wg.shape)]
    in_specs += [_row_spec(BRANCH)] * N_BRANCHES
    in_specs += [_const_spec(w_up.shape), _const_spec(w_out.shape)]
    args = [x, mod, norm_w, wg, *branches, w_up, w_out]
    if final:
        in_specs.append(_const_spec((1, D_MODEL)))
        args.append(final_norm_w.reshape(1, D_MODEL))
    return pl.pallas_call(
        functools.partial(_merge_kernel, final),
        grid=(batch, seq // ROW_TILE),
        in_specs=in_specs,
        out_specs=_row_spec(D_MODEL),
        out_shape=jax.ShapeDtypeStruct(x.shape, _F32),
        compiler_params=_params(),
        name="merge_out_proj",
    )(*args)


def kernel(x, c, ada_w, ada_b, norm_w, w_in, swa_sinks, pool_w, pool_scale, conv_w, conv_b, dt_bias, a_log,
           d_skip, ssm_norm_w, w_up, w_out, final_norm_w):
    depth = ada_w.shape[0]
    batch = x.shape[0]
    mod_all = _modulation(c, ada_w, ada_b)
    for l in range(depth):
        mod = mod_all[l].reshape(batch, 1, 3 * D_MODEL)
        nw = norm_w[l].reshape(1, D_MODEL)
        w = w_in[l]
        wr, ws, wp, wm = (w[:, a:b].astype(_BF16) for a, b in (RET_COLS, SWA_COLS, POOL_COLS, SSD_COLS))
        wdt = jnp.pad(w[:, DT_COLS[0]:DT_COLS[1]], ((0, 0), (0, DT_PAD - M_HEADS))).astype(_BF16)
        wg = w[:, MG_COLS[0]:MG_COLS[1]].astype(_BF16)
        ret_p, swa_p, pool_p, ssd_p, dt = _projection(x, mod, nw, wr, ws, wp, wm, wdt)
        ret = _retention(ret_p)
        att = _swa(swa_p, swa_sinks[l])
        pool = _pool(pool_p, pool_w[l].astype(_BF16), pool_scale[l])
        ssm = _ssd(ssd_p, dt, conv_w[l], conv_b[l], dt_bias[l], a_log[l], d_skip[l], ssm_norm_w[l])
        x = _merge(x, mod, nw, wg, (ret, att, pool, ssm), w_up[l].astype(_BF16), w_out[l].astype(_BF16),
                   final_norm_w if l == depth - 1 else None)
    return x
```

```python
import functools

import jax
import jax.numpy as jnp
import numpy as np
from jax import lax
from jax.experimental import pallas as pl
from jax.experimental.pallas import tpu as pltpu

D_MODEL = 1024
BRANCH = 512
N_BRANCHES = 4
EPS = 1e-6
CHUNK = 128
LANES = 128

RET_HEADS, RET_DK, RET_DV = 4, 64, 128
RET_QK_W = RET_HEADS * RET_DK
RET_V_W = RET_HEADS * RET_DV
SWA_HQ, SWA_HKV, SWA_D = 8, 2, 64
SWA_GROUP = SWA_HQ // SWA_HKV
SWA_Q_W = SWA_HQ * SWA_D
SWA_KV_W = SWA_HKV * SWA_D
SWA_EXT_W = 2 * LANES
POOL_WINDOWS = (2, 4, 8, 16)
POOL_GD = BRANCH // len(POOL_WINDOWS)
M_HEADS, M_P, M_GROUPS, M_N, M_CONV = 8, 64, 2, 128, 4
M_R = M_HEADS // M_GROUPS
M_GW = M_R * M_P
M_BC = M_GROUPS * M_N
M_X_W = M_HEADS * M_P
M_CONV_W = M_X_W + 2 * M_BC
CONV_HIST = 8
DT_PAD = LANES

_SIZES = (256, 256, 512, 512, 512, 128, 128, 512, 512, 512, 1024, 512, M_HEADS, N_BRANCHES * D_MODEL)
_OFFS = np.concatenate([[0], np.cumsum(_SIZES)]).tolist()
RET_COLS = (_OFFS[0], _OFFS[4])
SWA_COLS = (_OFFS[4], _OFFS[8])
POOL_COLS = (_OFFS[8], _OFFS[10])
SSD_COLS = (_OFFS[10], _OFFS[12])
DT_COLS = (_OFFS[12], _OFFS[13])
MG_COLS = (_OFFS[13], _OFFS[14])
MIXER_WIDTHS = tuple(b - a for a, b in (RET_COLS, SWA_COLS, POOL_COLS, SSD_COLS))

ROW_TILE = 512
N_CHUNKS = ROW_TILE // CHUNK
V7X_VMEM_LIMIT = 56 * 1024 * 1024

_F32 = jnp.float32
_BF16 = jnp.bfloat16


def _sigmoid(x):
    return 0.5 * jnp.tanh(0.5 * x) + 0.5


def _silu(x):
    half = 0.5 * x
    return half + half * jnp.tanh(half)


def _softplus(x):
    return jnp.maximum(x, 0.0) + jnp.log(1.0 + jnp.exp(-jnp.abs(x)))


def _dot(a, b):
    return jnp.dot(a, b, preferred_element_type=_F32)


def _dot_nt(a, b):
    return lax.dot_general(a, b, (((1,), (1,)), ((), ())), preferred_element_type=_F32)


def _dot_tn(a, b):
    return lax.dot_general(a, b, (((0,), (0,)), ((), ())), preferred_element_type=_F32)


def _const_spec(shape):
    nd = len(shape)
    return pl.BlockSpec(shape, lambda *_: (0,) * nd, pipeline_mode=pl.Buffered(1))


def _row_spec(width):
    return pl.BlockSpec((1, ROW_TILE, width), lambda b, t: (b, t, 0))


def _mod_spec():
    return pl.BlockSpec((1, 1, 3 * D_MODEL), lambda b, t: (b, 0, 0))


def _params():
    return pltpu.CompilerParams(dimension_semantics=("parallel", "arbitrary"),
                                vmem_limit_bytes=V7X_VMEM_LIMIT)


def _modulated_norm(x, mod_ref, nw_ref):
    shift = mod_ref[0, :, 0:D_MODEL]
    scale = mod_ref[0, :, D_MODEL:2 * D_MODEL]
    rs = lax.rsqrt(jnp.mean(x * x, axis=-1, keepdims=True) + EPS)
    return (x * rs) * nw_ref[...] * (1.0 + scale) + shift


def _mod_kernel(c_ref, w_ref, b_ref, o_ref):
    sc = _silu(c_ref[...]).astype(_BF16)
    o_ref[0] = _dot(sc, w_ref[0].astype(_BF16)) + b_ref[0]


def _modulation(c, ada_w, ada_b):
    depth, _, n = ada_w.shape
    bn = D_MODEL
    batch = c.shape[0]
    return pl.pallas_call(
        _mod_kernel,
        grid=(depth, n // bn),
        in_specs=[pl.BlockSpec((batch, D_MODEL), lambda l, j: (0, 0)),
                  pl.BlockSpec((1, D_MODEL, bn), lambda l, j: (l, 0, j)),
                  pl.BlockSpec((1, 1, bn), lambda l, j: (l, 0, j))],
        out_specs=pl.BlockSpec((1, batch, bn), lambda l, j: (l, 0, j)),
        out_shape=jax.ShapeDtypeStruct((depth, batch, n), _F32),
        name="adaln_modulation",
    )(c, ada_w, ada_b.reshape(depth, 1, n))


def _retention_tables():
    log_g = np.log(1.0 - 2.0 ** (-5.0 - np.arange(RET_HEADS, dtype=np.float64)))
    pos = np.arange(CHUNK, dtype=np.float64)
    diff = pos[:, None] - pos[None, :]
    inner = np.where(diff >= 0, np.exp(log_g[:, None, None] * np.where(diff >= 0, diff, 0.0)), 0.0)
    q_decay = np.repeat(np.exp(log_g[:, None] * (pos + 1.0)).T, RET_DV, axis=1)
    k_decay = np.repeat(np.exp(log_g[:, None] * (CHUNK - 1.0 - pos)).T, RET_DK, axis=1)
    chunk_decay = np.repeat(np.exp(log_g * CHUNK), RET_DV)[None, :]
    block_diag = (np.arange(RET_QK_W)[:, None] // RET_DK == np.arange(RET_V_W)[None, :] // RET_DV)
    return tuple(jnp.asarray(t, _F32) for t in (inner, q_decay, k_decay, chunk_decay, block_diag))


def _retention_tile(p, o, inner_ref, qd_ref, kd_ref, cd_ref, bd_ref, state_ref):
    head_of_lane = lax.broadcasted_iota(jnp.int32, (CHUNK, RET_QK_W), 1) // RET_DK
    for ci in range(N_CHUNKS):
        rows = slice(ci * CHUNK, (ci + 1) * CHUNK)
        q = p[rows, 0:RET_QK_W]
        k = p[rows, RET_QK_W:2 * RET_QK_W].astype(_F32) * (RET_DK ** -0.5)
        k_bf = k.astype(_BF16)
        v = p[rows, 2 * RET_QK_W:2 * RET_QK_W + RET_V_W]
        state = state_ref[...]
        cross = _dot(q, state.astype(_BF16)) * qd_ref[...]
        update = _dot_tn((k * kd_ref[...]).astype(_BF16), v)
        state_ref[...] = state * cd_ref[...] + update * bd_ref[...]
        for hd in range(RET_HEADS):
            lanes = slice(hd * RET_DV, (hd + 1) * RET_DV)
            q_h = jnp.where(head_of_lane == hd, q, jnp.zeros_like(q))
            scores = _dot_nt(q_h, k_bf) * inner_ref[hd]
            out = _dot(scores.astype(_BF16), v[:, lanes]) + cross[:, lanes]
            out = out * lax.rsqrt(jnp.mean(out * out, axis=-1, keepdims=True) + EPS)
            g0 = 2 * RET_QK_W + RET_V_W + hd * RET_DV
            gate = p[rows, g0:g0 + RET_DV].astype(_F32)
            o[rows, lanes] = (out * _silu(gate)).astype(_BF16)


def _swa_bias():
    slopes = 2.0 ** (-8.0 * np.arange(1, SWA_HQ + 1, dtype=np.float64) / SWA_HQ)
    qi = np.arange(CHUNK)
    kj = np.arange(2 * CHUNK)
    delta = CHUNK + qi[:, None] - kj[None, :]
    valid = (delta >= 0) & (delta < CHUNK)
    variants = []
    for ok in (valid, valid & (kj[None, :] >= CHUNK)):
        bias = np.where(ok[None], -slopes[:, None, None] * delta[None].astype(np.float64), -np.inf)
        variants.append(bias.reshape(SWA_HKV, SWA_GROUP * CHUNK, 2 * CHUNK))
    return jnp.asarray(np.stack(variants), _F32)


def _swa_sink_rows(sinks):
    rows = jnp.repeat(sinks.astype(_F32), CHUNK)[:, None]
    return jnp.broadcast_to(rows, (SWA_HQ * CHUNK, LANES)).reshape(SWA_HKV, SWA_GROUP * CHUNK, LANES)


def _swa_tile(first_tile, p, o, bias_ref, sink_ref, kt_buf, v_buf):
    low_half = lax.broadcasted_iota(jnp.int32, (ROW_TILE, SWA_KV_W), 1) < SWA_D
    k = p[:, SWA_Q_W:SWA_Q_W + SWA_KV_W].astype(_F32)
    k_swapped = pltpu.roll(k, SWA_D, axis=1)
    for j, dup in enumerate((jnp.where(low_half, k, k_swapped), jnp.where(low_half, k_swapped, k))):
        for bi in range(N_CHUNKS):
            kt_buf[j, :, (bi + 1) * CHUNK:(bi + 2) * CHUNK] = dup[bi * CHUNK:(bi + 1) * CHUNK, :].T.astype(_BF16)
    v = p[:, SWA_Q_W + SWA_KV_W:SWA_Q_W + 2 * SWA_KV_W].astype(_F32)
    v_swapped = pltpu.roll(v, SWA_D, axis=1)
    for j, dup in enumerate((jnp.where(low_half, v, v_swapped), jnp.where(low_half, v_swapped, v))):
        v_buf[CHUNK:CHUNK + ROW_TILE, j * SWA_EXT_W:j * SWA_EXT_W + LANES] = dup.astype(_BF16)
        v_buf[CHUNK:CHUNK + ROW_TILE, j * SWA_EXT_W + LANES:(j + 1) * SWA_EXT_W] = jnp.ones((ROW_TILE, LANES), _BF16)

    low_lanes = lax.broadcasted_iota(jnp.int32, (CHUNK, LANES), 1) < SWA_D
    high_lanes = jnp.logical_not(low_lanes)
    first_variant = first_tile.astype(jnp.int32)
    gate0 = SWA_Q_W + 2 * SWA_KV_W

    for bi in range(N_CHUNKS):
        rows = slice(bi * CHUNK, (bi + 1) * CHUNK)
        keys = slice(bi * CHUNK, (bi + 2) * CHUNK)
        variant = first_variant if bi == 0 else 0
        for j in range(SWA_HKV):
            k_t = kt_buf[j, :, keys]
            v_ext = v_buf[keys, j * SWA_EXT_W:(j + 1) * SWA_EXT_W]
            for pair in range(SWA_GROUP // 2):
                c0 = (j * SWA_GROUP // 2 + pair) * LANES
                qp = p[rows, c0:c0 + LANES].astype(_F32) * (SWA_D ** -0.5)
                halves = []
                for odd in range(2):
                    g_rows = slice((2 * pair + odd) * CHUNK, (2 * pair + odd + 1) * CHUNK)
                    q_h = jnp.where(high_lanes if odd else low_lanes, qp, 0.0).astype(_BF16)
                    s = _dot(q_h, k_t) + bias_ref[variant, j, g_rows, :]
                    s_prev, s_cur = s[:, 0:CHUNK], s[:, CHUNK:2 * CHUNK]
                    sink = sink_ref[j, g_rows, :]
                    m = jnp.maximum(jnp.max(jnp.maximum(s_prev, s_cur), axis=-1, keepdims=True), sink)
                    e = jnp.concatenate([jnp.exp(s_prev - m), jnp.exp(s_cur - m)], axis=1).astype(_BF16)
                    ov = _dot(e, v_ext)
                    halves.append(ov[:, 0:LANES] / (ov[:, LANES:2 * LANES] + jnp.exp(sink - m)))
                gate = p[rows, gate0 + c0:gate0 + c0 + LANES].astype(_F32)
                o[rows, c0:c0 + LANES] = (jnp.where(low_lanes, halves[0], halves[1]) * _silu(gate)).astype(_BF16)

    kt_buf[:, :, 0:CHUNK] = kt_buf[:, :, ROW_TILE:ROW_TILE + CHUNK]
    v_buf[0:CHUNK, :] = v_buf[ROW_TILE:ROW_TILE + CHUNK, :]


def _pool_inverse_counts():
    pos = np.arange(ROW_TILE, dtype=np.float64)[:, None]
    windows = np.repeat(np.asarray(POOL_WINDOWS, np.float64), POOL_GD)[None, :]
    steady = np.broadcast_to(1.0 / windows, (ROW_TILE, BRANCH))
    start = 1.0 / np.minimum(pos + 1.0, windows)
    return jnp.asarray(np.stack([steady, start]), _F32)


def _pool_tile(first_tile, p, o, inv_ref, w_ref, scale_ref, hist):
    hist[CHUNK:CHUNK + ROW_TILE, :] = p[:, 0:BRANCH].astype(_F32)
    inv = inv_ref[first_tile.astype(jnp.int32)]
    for g, w in enumerate(POOL_WINDOWS):
        lanes = slice(g * POOL_GD, (g + 1) * POOL_GD)
        u = hist[CHUNK:CHUNK + ROW_TILE, lanes]
        acc = u
        for j in range(1, w):
            acc = acc + hist[CHUNK - j:CHUNK - j + ROW_TILE, lanes]
        diff = acc * inv[:, lanes] - u
        y = _dot(diff.astype(_BF16), w_ref[g])
        gate = p[:, BRANCH + g * POOL_GD:BRANCH + (g + 1) * POOL_GD].astype(_F32)
        o[:, lanes] = (y * scale_ref[:, lanes] * _silu(gate)).astype(_BF16)
    hist[0:CHUNK, :] = hist[ROW_TILE:ROW_TILE + CHUNK, :]


def _expansion_matrix():
    e = np.zeros((2 * LANES, M_X_W), np.float32)
    for hd in range(M_HEADS):
        e[hd, hd * M_P:(hd + 1) * M_P] = 1.0
        e[LANES + hd, hd * M_P:(hd + 1) * M_P] = 1.0
    return jnp.asarray(e, _BF16)


def _pad_lanes(v, width=DT_PAD):
    return jnp.pad(v.astype(_F32), (0, width - v.shape[0])).reshape(1, width)


def _ssd_tile(p, dt_raw, o, cw_ref, cb_ref, dtb_ref, alog_ref, dskip_ref, nw_ref, tri_ref, exp_ref,
              hist, xbc_buf, state_ref):
    hist[CONV_HIST:CONV_HIST + ROW_TILE, :] = p[:, 0:M_CONV_W].astype(_F32)
    acc = cb_ref[...] + cw_ref[M_CONV - 1:M_CONV, :] * hist[CONV_HIST:CONV_HIST + ROW_TILE, :]
    for j in range(1, M_CONV):
        acc = acc + cw_ref[M_CONV - 1 - j:M_CONV - j, :] * hist[CONV_HIST - j:CONV_HIST - j + ROW_TILE, :]
    xbc_buf[...] = _silu(acc)
    hist[0:CONV_HIST, :] = hist[ROW_TILE:ROW_TILE + CONV_HIST, :]

    a_row = -jnp.exp(alog_ref[...])
    causal = (lax.broadcasted_iota(jnp.int32, (CHUNK, CHUNK), 0)
              >= lax.broadcasted_iota(jnp.int32, (CHUNK, CHUNK), 1))
    head_of_lane = lax.broadcasted_iota(jnp.int32, (CHUNK, M_GW), 1) // M_P

    def expand(vals):
        stacked = jnp.concatenate(vals, axis=0)
        hi = stacked.astype(_BF16)
        lo = (stacked - hi.astype(_F32)).astype(_BF16)
        wide = _dot(jnp.concatenate([hi, lo], axis=1), exp_ref[...])
        return [wide[i * CHUNK:(i + 1) * CHUNK, :] for i in range(len(vals))]

    for ci in range(N_CHUNKS):
        rows = slice(ci * CHUNK, (ci + 1) * CHUNK)
        dt = _softplus(dt_raw[rows, :] + dtb_ref[...])
        acs = jnp.dot(tri_ref[...], dt * a_row, preferred_element_type=_F32, precision=lax.Precision.HIGHEST)
        acs_t = acs.T
        last = acs[CHUNK - 1:CHUNK, :]
        dt_x, w_x, ex_x = expand([dt, dt * jnp.exp(last - acs), jnp.exp(acs)])
        x = xbc_buf[rows, 0:M_X_W]
        xdt = (x * dt_x).astype(_BF16)
        xd = (x * w_x).astype(_BF16)
        for g in range(M_GROUPS):
            lanes = slice(g * M_GW, (g + 1) * M_GW)
            bmat = xbc_buf[rows, M_X_W + g * M_N:M_X_W + (g + 1) * M_N].astype(_BF16)
            cmat = xbc_buf[rows, M_X_W + M_BC + g * M_N:M_X_W + M_BC + (g + 1) * M_N].astype(_BF16)
            cb = _dot_nt(cmat, bmat)
            state = state_ref[g]
            ex_g = ex_x[:, lanes]
            y = _dot(cmat, state.astype(_BF16)) * ex_g + x[:, lanes] * dskip_ref[:, lanes]
            xdt_g = xdt[:, lanes]
            for r in range(M_R):
                hd = g * M_R + r
                seg = acs[:, hd:hd + 1] - acs_t[hd:hd + 1, :]
                lmat = jnp.exp(jnp.where(causal, seg, -jnp.inf))
                x_h = jnp.where(head_of_lane == r, xdt_g, jnp.zeros_like(xdt_g))
                y = y + _dot((cb * lmat).astype(_BF16), x_h)
            state_ref[g] = state * ex_g[CHUNK - 1:CHUNK, :] + _dot_tn(bmat, xd[:, lanes])
            z = p[rows, M_CONV_W + g * M_GW:M_CONV_W + (g + 1) * M_GW].astype(_F32)
            yz = y * _silu(z)
            yz = yz * lax.rsqrt(jnp.mean(yz * yz, axis=-1, keepdims=True) + EPS)
            o[rows, lanes] = (yz * nw_ref[:, lanes]).astype(_BF16)


N_RET_T, N_SWA_T, N_POOL_T, N_SSD_T = 5, 2, 3, 8


def _mixers_kernel(x_ref, mod_ref, nw_ref, wr_ref, ws_ref, wp_ref, wm_ref, wdt_ref, *rest):
    consts, rest = rest[:N_RET_T + N_SWA_T + N_POOL_T + N_SSD_T], rest[N_RET_T + N_SWA_T + N_POOL_T + N_SSD_T:]
    ret_c, consts = consts[:N_RET_T], consts[N_RET_T:]
    swa_c, consts = consts[:N_SWA_T], consts[N_SWA_T:]
    pool_c, ssd_c = consts[:N_POOL_T], consts[N_POOL_T:]
    ret_o, att_o, pool_o, ssm_o = (r.at[0] for r in rest[:N_BRANCHES])
    proj = rest[N_BRANCHES:N_BRANCHES + 5]
    ret_state, kt_buf, v_buf, pool_hist, conv_hist, xbc_buf, ssd_state = rest[N_BRANCHES + 5:]
    first_tile = pl.program_id(1) == 0

    @pl.when(first_tile)
    def _():
        ret_state[...] = jnp.zeros_like(ret_state)
        ssd_state[...] = jnp.zeros_like(ssd_state)
        kt_buf[:, :, 0:CHUNK] = jnp.zeros((SWA_HKV, LANES, CHUNK), _BF16)
        v_buf[0:CHUNK, :] = jnp.zeros((CHUNK, SWA_HKV * SWA_EXT_W), _BF16)
        pool_hist[0:CHUNK, :] = jnp.zeros((CHUNK, BRANCH), _F32)
        conv_hist[0:CONV_HIST, :] = jnp.zeros((CONV_HIST, M_CONV_W), _F32)

    h = _modulated_norm(x_ref[0], mod_ref, nw_ref).astype(_BF16)
    for ref, w_ref in zip(proj, (wr_ref, ws_ref, wp_ref, wm_ref, wdt_ref)):
        ref[...] = _dot(h, w_ref[...]).astype(ref.dtype)

    ret_p, swa_p, pool_p, ssd_p, dt_p = proj
    _retention_tile(ret_p, ret_o, *ret_c, ret_state)
    _swa_tile(first_tile, swa_p, att_o, *swa_c, kt_buf, v_buf)
    _pool_tile(first_tile, pool_p, pool_o, *pool_c, pool_hist)
    _ssd_tile(ssd_p, dt_p, ssm_o, *ssd_c, conv_hist, xbc_buf, ssd_state)


def _mixers(x, mod, norm_w, weights, swa_sinks, pool_w, pool_scale, conv_w, conv_b, dt_bias, a_log, d_skip,
            ssm_norm_w):
    batch, seq, _ = x.shape
    consts = list(_retention_tables())
    consts += [_swa_bias(), _swa_sink_rows(swa_sinks)]
    consts += [_pool_inverse_counts(), pool_w.astype(_BF16), pool_scale.reshape(1, BRANCH)]
    consts += [conv_w, conv_b.reshape(1, M_CONV_W), _pad_lanes(dt_bias), _pad_lanes(a_log),
               jnp.repeat(d_skip.astype(_F32), M_P).reshape(1, BRANCH), ssm_norm_w.reshape(1, BRANCH),
               jnp.asarray(np.tril(np.ones((CHUNK, CHUNK), np.float32))), _expansion_matrix()]
    assert len(consts) == N_RET_T + N_SWA_T + N_POOL_T + N_SSD_T
    operands = [x, mod, norm_w, *weights, *consts]
    in_specs = [_row_spec(D_MODEL), _mod_spec()] + [_const_spec(a.shape) for a in operands[2:]]
    scratch = [pltpu.VMEM((ROW_TILE, w), _BF16) for w in MIXER_WIDTHS] + [pltpu.VMEM((ROW_TILE, DT_PAD), _F32)]
    scratch += [pltpu.VMEM((RET_QK_W, RET_V_W), _F32),
                pltpu.VMEM((SWA_HKV, LANES, CHUNK + ROW_TILE), _BF16),
                pltpu.VMEM((CHUNK + ROW_TILE, SWA_HKV * SWA_EXT_W), _BF16),
                pltpu.VMEM((CHUNK + ROW_TILE, BRANCH), _F32),
                pltpu.VMEM((CONV_HIST + ROW_TILE, M_CONV_W), _F32),
                pltpu.VMEM((ROW_TILE, M_CONV_W), _F32),
                pltpu.VMEM((M_GROUPS, M_N, M_GW), _F32)]
    return pl.pallas_call(
        _mixers_kernel,
        grid=(batch, seq // ROW_TILE),
        in_specs=in_specs,
        out_specs=[_row_spec(BRANCH)] * N_BRANCHES,
        out_shape=[jax.ShapeDtypeStruct((batch, seq, BRANCH), _BF16)] * N_BRANCHES,
        scratch_shapes=scratch,
        compiler_params=_params(),
        name="norm_proj_mixers",
    )(*operands)


def _merge_kernel(final, x_ref, mod_ref, nw_ref, wg_ref, ret_ref, att_ref, pool_ref, ssm_ref, wup_ref, wout_ref,
                  *rest):
    if final:
        fnw_ref, o_ref = rest
    else:
        (o_ref,) = rest
    x = x_ref[0]
    h = _modulated_norm(x, mod_ref, nw_ref).astype(_BF16)
    merged = None
    for i, br_ref in enumerate((ret_ref, att_ref, pool_ref, ssm_ref)):
        gate = _sigmoid(_dot(h, wg_ref[:, i * D_MODEL:(i + 1) * D_MODEL]))
        term = gate * _dot(br_ref[0], wup_ref[i])
        merged = term if merged is None else merged + term
    out = _dot(merged.astype(_BF16), wout_ref[...])
    y = x + mod_ref[0, :, 2 * D_MODEL:3 * D_MODEL] * out
    if final:
        y = y * lax.rsqrt(jnp.mean(y * y, axis=-1, keepdims=True) + EPS) * fnw_ref[...]
    o_ref[0] = y


def _merge(x, mod, norm_w, wg, branches, w_up, w_out, final_norm_w):
    batch, seq, _ = x.shape
    final = final_norm_w is not None
    in_specs = [_row_spec(D_MODEL), _mod_spec(), _const_spec((1, D_MODEL)), _const_spec(wg.shape)]
    in_specs += [_row_spec(BRANCH)] * N_BRANCHES
    in_specs += [_const_spec(w_up.shape), _const_spec(w_out.shape)]
    args = [x, mod, norm_w, wg, *branches, w_up, w_out]
    if final:
        in_specs.append(_const_spec((1, D_MODEL)))
        args.append(final_norm_w.reshape(1, D_MODEL))
    return pl.pallas_call(
        functools.partial(_merge_kernel, final),
        grid=(batch, seq // ROW_TILE),
        in_specs=in_specs,
        out_specs=_row_spec(D_MODEL),
        out_shape=jax.ShapeDtypeStruct(x.shape, _F32),
        compiler_params=_params(),
        name="merge_out_proj",
    )(*args)


def kernel(x, c, ada_w, ada_b, norm_w, w_in, swa_sinks, pool_w, pool_scale, conv_w, conv_b, dt_bias, a_log,
           d_skip, ssm_norm_w, w_up, w_out, final_norm_w):
    depth = ada_w.shape[0]
    batch = x.shape[0]
    mod_all = _modulation(c, ada_w, ada_b)
    for l in range(depth):
        mod = mod_all[l].reshape(batch, 1, 3 * D_MODEL)
        nw = norm_w[l].reshape(1, D_MODEL)
        w = w_in[l]
        weights = [w[:, a:b].astype(_BF16) for a, b in (RET_COLS, SWA_COLS, POOL_COLS, SSD_COLS)]
        weights.append(jnp.pad(w[:, DT_COLS[0]:DT_COLS[1]], ((0, 0), (0, DT_PAD - M_HEADS))).astype(_BF16))
        wg = w[:, MG_COLS[0]:MG_COLS[1]].astype(_BF16)
        branches = _mixers(x, mod, nw, weights, swa_sinks[l], pool_w[l], pool_scale[l], conv_w[l], conv_b[l],
                           dt_bias[l], a_log[l], d_skip[l], ssm_norm_w[l])
        x = _merge(x, mod, nw, wg, branches, w_up[l].astype(_BF16), w_out[l].astype(_BF16),
                   final_norm_w if l == depth - 1 else None)
    return x
```

```python
import functools

import jax
import jax.numpy as jnp
import numpy as np
from jax import lax
from jax.experimental import pallas as pl
from jax.experimental.pallas import tpu as pltpu

D_MODEL = 1024
BRANCH = 512
N_BRANCHES = 4
EPS = 1e-6
CHUNK = 128
LANES = 128

RET_HEADS, RET_DK, RET_DV = 4, 64, 128
RET_QK_W = RET_HEADS * RET_DK
RET_V_W = RET_HEADS * RET_DV
SWA_HQ, SWA_HKV, SWA_D = 8, 2, 64
SWA_GROUP = SWA_HQ // SWA_HKV
SWA_Q_W = SWA_HQ * SWA_D
SWA_KV_W = SWA_HKV * SWA_D
SWA_EXT_W = 2 * LANES
POOL_WINDOWS = (2, 4, 8, 16)
POOL_GD = BRANCH // len(POOL_WINDOWS)
M_HEADS, M_P, M_GROUPS, M_N, M_CONV = 8, 64, 2, 128, 4
M_R = M_HEADS // M_GROUPS
M_GW = M_R * M_P
M_BC = M_GROUPS * M_N
M_X_W = M_HEADS * M_P
M_CONV_W = M_X_W + 2 * M_BC
CONV_HIST = 8
DT_PAD = LANES

_SIZES = (256, 256, 512, 512, 512, 128, 128, 512, 512, 512, 1024, 512, M_HEADS, N_BRANCHES * D_MODEL)
_OFFS = np.concatenate([[0], np.cumsum(_SIZES)]).tolist()
RET_COLS = (_OFFS[0], _OFFS[4])
SWA_COLS = (_OFFS[4], _OFFS[8])
POOL_COLS = (_OFFS[8], _OFFS[10])
SSD_COLS = (_OFFS[10], _OFFS[12])
DT_COLS = (_OFFS[12], _OFFS[13])
MG_COLS = (_OFFS[13], _OFFS[14])
MIXER_WIDTHS = tuple(b - a for a, b in (RET_COLS, SWA_COLS, POOL_COLS, SSD_COLS))

ROW_TILE = 512
N_CHUNKS = ROW_TILE // CHUNK
V7X_VMEM_LIMIT = 56 * 1024 * 1024

_F32 = jnp.float32
_BF16 = jnp.bfloat16


def _sigmoid(x):
    return 0.5 * jnp.tanh(0.5 * x) + 0.5


def _silu(x):
    half = 0.5 * x
    return half + half * jnp.tanh(half)


def _softplus(x):
    return jnp.maximum(x, 0.0) + jnp.log(1.0 + jnp.exp(-jnp.abs(x)))


def _dot(a, b):
    return jnp.dot(a, b, preferred_element_type=_F32)


def _dot_nt(a, b):
    return lax.dot_general(a, b, (((1,), (1,)), ((), ())), preferred_element_type=_F32)


def _dot_tn(a, b):
    return lax.dot_general(a, b, (((0,), (0,)), ((), ())), preferred_element_type=_F32)


def _const_spec(shape):
    nd = len(shape)
    return pl.BlockSpec(shape, lambda *_: (0,) * nd, pipeline_mode=pl.Buffered(1))


def _row_spec(width):
    return pl.BlockSpec((1, ROW_TILE, width), lambda b, t: (b, t, 0))


def _mod_spec():
    return pl.BlockSpec((1, 1, 3 * D_MODEL), lambda b, t: (b, 0, 0))


def _params():
    return pltpu.CompilerParams(dimension_semantics=("parallel", "arbitrary"),
                                vmem_limit_bytes=V7X_VMEM_LIMIT)


def _modulated_norm(x, mod_ref, nw_ref):
    shift = mod_ref[0, :, 0:D_MODEL]
    scale = mod_ref[0, :, D_MODEL:2 * D_MODEL]
    rs = lax.rsqrt(jnp.mean(x * x, axis=-1, keepdims=True) + EPS)
    return (x * rs) * nw_ref[...] * (1.0 + scale) + shift


def _mod_kernel(c_ref, w_ref, b_ref, o_ref):
    sc = _silu(c_ref[...]).astype(_BF16)
    o_ref[0] = _dot(sc, w_ref[0].astype(_BF16)) + b_ref[0]


def _modulation(c, ada_w, ada_b):
    depth, _, n = ada_w.shape
    bn = D_MODEL
    batch = c.shape[0]
    return pl.pallas_call(
        _mod_kernel,
        grid=(depth, n // bn),
        in_specs=[pl.BlockSpec((batch, D_MODEL), lambda l, j: (0, 0)),
                  pl.BlockSpec((1, D_MODEL, bn), lambda l, j: (l, 0, j)),
                  pl.BlockSpec((1, 1, bn), lambda l, j: (l, 0, j))],
        out_specs=pl.BlockSpec((1, batch, bn), lambda l, j: (l, 0, j)),
        out_shape=jax.ShapeDtypeStruct((depth, batch, n), _F32),
        name="adaln_modulation",
    )(c, ada_w, ada_b.reshape(depth, 1, n))


def _retention_tables():
    log_g = np.log(1.0 - 2.0 ** (-5.0 - np.arange(RET_HEADS, dtype=np.float64)))
    pos = np.arange(CHUNK, dtype=np.float64)
    diff = pos[:, None] - pos[None, :]
    inner = np.where(diff >= 0, np.exp(log_g[:, None, None] * np.where(diff >= 0, diff, 0.0)), 0.0)
    q_decay = np.repeat(np.exp(log_g[:, None] * (pos + 1.0)).T, RET_DV, axis=1)
    k_decay = np.repeat(np.exp(log_g[:, None] * (CHUNK - 1.0 - pos)).T, RET_DK, axis=1)
    chunk_decay = np.repeat(np.exp(log_g * CHUNK), RET_DV)[None, :]
    block_diag = (np.arange(RET_QK_W)[:, None] // RET_DK == np.arange(RET_V_W)[None, :] // RET_DV)
    return tuple(jnp.asarray(t, _F32) for t in (inner, q_decay, k_decay, chunk_decay, block_diag))


def _retention_tile(p, o, inner_ref, qd_ref, kd_ref, cd_ref, bd_ref, state_ref):
    head_of_lane = lax.broadcasted_iota(jnp.int32, (CHUNK, RET_QK_W), 1) // RET_DK
    for ci in range(N_CHUNKS):
        rows = slice(ci * CHUNK, (ci + 1) * CHUNK)
        q = p[rows, 0:RET_QK_W]
        k = p[rows, RET_QK_W:2 * RET_QK_W].astype(_F32) * (RET_DK ** -0.5)
        k_bf = k.astype(_BF16)
        v = p[rows, 2 * RET_QK_W:2 * RET_QK_W + RET_V_W]
        state = state_ref[...]
        cross = _dot(q, state.astype(_BF16)) * qd_ref[...]
        update = _dot_tn((k * kd_ref[...]).astype(_BF16), v)
        state_ref[...] = state * cd_ref[...] + update * bd_ref[...]
        for hd in range(RET_HEADS):
            lanes = slice(hd * RET_DV, (hd + 1) * RET_DV)
            q_h = jnp.where(head_of_lane == hd, q, jnp.zeros_like(q))
            scores = _dot_nt(q_h, k_bf) * inner_ref[hd]
            out = _dot(scores.astype(_BF16), v[:, lanes]) + cross[:, lanes]
            out = out * lax.rsqrt(jnp.mean(out * out, axis=-1, keepdims=True) + EPS)
            g0 = 2 * RET_QK_W + RET_V_W + hd * RET_DV
            gate = p[rows, g0:g0 + RET_DV].astype(_F32)
            o[rows, lanes] = (out * _silu(gate)).astype(_BF16)


def _swa_bias():
    slopes = 2.0 ** (-8.0 * np.arange(1, SWA_HQ + 1, dtype=np.float64) / SWA_HQ)
    qi = np.arange(CHUNK)
    kj = np.arange(2 * CHUNK)
    delta = CHUNK + qi[:, None] - kj[None, :]
    valid = (delta >= 0) & (delta < CHUNK)
    variants = []
    for ok in (valid, valid & (kj[None, :] >= CHUNK)):
        bias = np.where(ok[None], -slopes[:, None, None] * delta[None].astype(np.float64), -np.inf)
        variants.append(bias.reshape(SWA_HKV, SWA_GROUP * CHUNK, 2 * CHUNK))
    return jnp.asarray(np.stack(variants), _F32)


def _swa_sink_rows(sinks):
    rows = jnp.repeat(sinks.astype(_F32), CHUNK)[:, None]
    return jnp.broadcast_to(rows, (SWA_HQ * CHUNK, LANES)).reshape(SWA_HKV, SWA_GROUP * CHUNK, LANES)


def _swa_tile(first_tile, p, o, bias_ref, sink_ref, kt_buf, v_buf):
    low_half = lax.broadcasted_iota(jnp.int32, (ROW_TILE, SWA_KV_W), 1) < SWA_D
    k = p[:, SWA_Q_W:SWA_Q_W + SWA_KV_W].astype(_F32)
    k_swapped = pltpu.roll(k, SWA_D, axis=1)
    for j, dup in enumerate((jnp.where(low_half, k, k_swapped), jnp.where(low_half, k_swapped, k))):
        for bi in range(N_CHUNKS):
            kt_buf[j, :, (bi + 1) * CHUNK:(bi + 2) * CHUNK] = dup[bi * CHUNK:(bi + 1) * CHUNK, :].T.astype(_BF16)
    v = p[:, SWA_Q_W + SWA_KV_W:SWA_Q_W + 2 * SWA_KV_W].astype(_F32)
    v_swapped = pltpu.roll(v, SWA_D, axis=1)
    for j, dup in enumerate((jnp.where(low_half, v, v_swapped), jnp.where(low_half, v_swapped, v))):
        v_buf[CHUNK:CHUNK + ROW_TILE, j * SWA_EXT_W:j * SWA_EXT_W + LANES] = dup.astype(_BF16)
        v_buf[CHUNK:CHUNK + ROW_TILE, j * SWA_EXT_W + LANES:(j + 1) * SWA_EXT_W] = jnp.ones((ROW_TILE, LANES), _BF16)

    low_lanes = lax.broadcasted_iota(jnp.int32, (CHUNK, LANES), 1) < SWA_D
    high_lanes = jnp.logical_not(low_lanes)
    first_variant = first_tile.astype(jnp.int32)
    gate0 = SWA_Q_W + 2 * SWA_KV_W

    for bi in range(N_CHUNKS):
        rows = slice(bi * CHUNK, (bi + 1) * CHUNK)
        keys = slice(bi * CHUNK, (bi + 2) * CHUNK)
        variant = first_variant if bi == 0 else 0
        for j in range(SWA_HKV):
            k_t = kt_buf[j, :, keys]
            v_ext = v_buf[keys, j * SWA_EXT_W:(j + 1) * SWA_EXT_W]
            for pair in range(SWA_GROUP // 2):
                c0 = (j * SWA_GROUP // 2 + pair) * LANES
                qp = p[rows, c0:c0 + LANES].astype(_F32) * (SWA_D ** -0.5)
                halves = []
                for odd in range(2):
                    g_rows = slice((2 * pair + odd) * CHUNK, (2 * pair + odd + 1) * CHUNK)
                    q_h = jnp.where(high_lanes if odd else low_lanes, qp, 0.0).astype(_BF16)
                    s = _dot(q_h, k_t) + bias_ref[variant, j, g_rows, :]
                    s_prev, s_cur = s[:, 0:CHUNK], s[:, CHUNK:2 * CHUNK]
                    sink = sink_ref[j, g_rows, :]
                    m = jnp.maximum(jnp.max(jnp.maximum(s_prev, s_cur), axis=-1, keepdims=True), sink)
                    e = jnp.concatenate([jnp.exp(s_prev - m), jnp.exp(s_cur - m)], axis=1).astype(_BF16)
                    ov = _dot(e, v_ext)
                    halves.append(ov[:, 0:LANES] / (ov[:, LANES:2 * LANES] + jnp.exp(sink - m)))
                gate = p[rows, gate0 + c0:gate0 + c0 + LANES].astype(_F32)
                o[rows, c0:c0 + LANES] = (jnp.where(low_lanes, halves[0], halves[1]) * _silu(gate)).astype(_BF16)

    kt_buf[:, :, 0:CHUNK] = kt_buf[:, :, ROW_TILE:ROW_TILE + CHUNK]
    v_buf[0:CHUNK, :] = v_buf[ROW_TILE:ROW_TILE + CHUNK, :]


def _pool_inverse_counts():
    pos = np.arange(ROW_TILE, dtype=np.float64)[:, None]
    windows = np.repeat(np.asarray(POOL_WINDOWS, np.float64), POOL_GD)[None, :]
    steady = np.broadcast_to(1.0 / windows, (ROW_TILE, BRANCH))
    start = 1.0 / np.minimum(pos + 1.0, windows)
    return jnp.asarray(np.stack([steady, start]), _F32)


def _pool_tile(first_tile, p, o, inv_ref, w_ref, scale_ref, hist):
    hist[CHUNK:CHUNK + ROW_TILE, :] = p[:, 0:BRANCH].astype(_F32)
    inv = inv_ref[first_tile.astype(jnp.int32)]
    for g, w in enumerate(POOL_WINDOWS):
        lanes = slice(g * POOL_GD, (g + 1) * POOL_GD)
        u = hist[CHUNK:CHUNK + ROW_TILE, lanes]
        acc = u
        for j in range(1, w):
            acc = acc + hist[CHUNK - j:CHUNK - j + ROW_TILE, lanes]
        diff = acc * inv[:, lanes] - u
        y = _dot(diff.astype(_BF16), w_ref[g])
        gate = p[:, BRANCH + g * POOL_GD:BRANCH + (g + 1) * POOL_GD].astype(_F32)
        o[:, lanes] = (y * scale_ref[:, lanes] * _silu(gate)).astype(_BF16)
    hist[0:CHUNK, :] = hist[ROW_TILE:ROW_TILE + CHUNK, :]


def _expansion_matrix():
    e = np.zeros((2 * LANES, M_X_W), np.float32)
    for hd in range(M_HEADS):
        e[hd, hd * M_P:(hd + 1) * M_P] = 1.0
        e[LANES + hd, hd * M_P:(hd + 1) * M_P] = 1.0
    return jnp.asarray(e, _BF16)


def _pad_lanes(v, width=DT_PAD):
    return jnp.pad(v.astype(_F32), (0, width - v.shape[0])).reshape(1, width)


def _ssd_tile(p, dt_raw, o, cw_ref, cb_ref, dtb_ref, alog_ref, dskip_ref, nw_ref, tri_ref, exp_ref,
              hist, xbc_buf, state_ref):
    hist[CONV_HIST:CONV_HIST + ROW_TILE, :] = p[:, 0:M_CONV_W].astype(_F32)
    acc = cb_ref[...] + cw_ref[M_CONV - 1:M_CONV, :] * hist[CONV_HIST:CONV_HIST + ROW_TILE, :]
    for j in range(1, M_CONV):
        acc = acc + cw_ref[M_CONV - 1 - j:M_CONV - j, :] * hist[CONV_HIST - j:CONV_HIST - j + ROW_TILE, :]
    xbc_buf[...] = _silu(acc)
    hist[0:CONV_HIST, :] = hist[ROW_TILE:ROW_TILE + CONV_HIST, :]

    a_row = -jnp.exp(alog_ref[...])
    causal = (lax.broadcasted_iota(jnp.int32, (CHUNK, CHUNK), 0)
              >= lax.broadcasted_iota(jnp.int32, (CHUNK, CHUNK), 1))
    head_of_lane = lax.broadcasted_iota(jnp.int32, (CHUNK, M_GW), 1) // M_P

    def expand(vals):
        stacked = jnp.concatenate(vals, axis=0)
        hi = stacked.astype(_BF16)
        lo = (stacked - hi.astype(_F32)).astype(_BF16)
        wide = _dot(jnp.concatenate([hi, lo], axis=1), exp_ref[...])
        return [wide[i * CHUNK:(i + 1) * CHUNK, :] for i in range(len(vals))]

    for ci in range(N_CHUNKS):
        rows = slice(ci * CHUNK, (ci + 1) * CHUNK)
        dt = _softplus(dt_raw[rows, :] + dtb_ref[...])
        acs = jnp.dot(tri_ref[...], dt * a_row, preferred_element_type=_F32, precision=lax.Precision.HIGHEST)
        acs_t = acs.T
        last = acs[CHUNK - 1:CHUNK, :]
        dt_x, w_x, ex_x = expand([dt, dt * jnp.exp(last - acs), jnp.exp(acs)])
        x = xbc_buf[rows, 0:M_X_W]
        xdt = (x * dt_x).astype(_BF16)
        xd = (x * w_x).astype(_BF16)
        for g in range(M_GROUPS):
            lanes = slice(g * M_GW, (g + 1) * M_GW)
            bmat = xbc_buf[rows, M_X_W + g * M_N:M_X_W + (g + 1) * M_N].astype(_BF16)
            cmat = xbc_buf[rows, M_X_W + M_BC + g * M_N:M_X_W + M_BC + (g + 1) * M_N].astype(_BF16)
            cb = _dot_nt(cmat, bmat)
            state = state_ref[g]
            ex_g = ex_x[:, lanes]
            y = _dot(cmat, state.astype(_BF16)) * ex_g + x[:, lanes] * dskip_ref[:, lanes]
            xdt_g = xdt[:, lanes]
            for r in range(M_R):
                hd = g * M_R + r
                seg = acs[:, hd:hd + 1] - acs_t[hd:hd + 1, :]
                lmat = jnp.exp(jnp.where(causal, seg, -jnp.inf))
                x_h = jnp.where(head_of_lane == r, xdt_g, jnp.zeros_like(xdt_g))
                y = y + _dot((cb * lmat).astype(_BF16), x_h)
            state_ref[g] = state * ex_g[CHUNK - 1:CHUNK, :] + _dot_tn(bmat, xd[:, lanes])
            z = p[rows, M_CONV_W + g * M_GW:M_CONV_W + (g + 1) * M_GW].astype(_F32)
            yz = y * _silu(z)
            yz = yz * lax.rsqrt(jnp.mean(yz * yz, axis=-1, keepdims=True) + EPS)
            o[rows, lanes] = (yz * nw_ref[:, lanes]).astype(_BF16)


N_RET_T, N_SWA_T, N_POOL_T, N_SSD_T = 5, 2, 3, 8


def _mixers_kernel(x_ref, mod_ref, nw_ref, wr_ref, ws_ref, wp_ref, wm_ref, wdt_ref, *rest):
    consts, rest = rest[:N_RET_T + N_SWA_T + N_POOL_T + N_SSD_T], rest[N_RET_T + N_SWA_T + N_POOL_T + N_SSD_T:]
    ret_c, consts = consts[:N_RET_T], consts[N_RET_T:]
    swa_c, consts = consts[:N_SWA_T], consts[N_SWA_T:]
    pool_c, ssd_c = consts[:N_POOL_T], consts[N_POOL_T:]
    ret_o, att_o, pool_o, ssm_o, h_o = (r.at[0] for r in rest[:N_BRANCHES + 1])
    proj = rest[N_BRANCHES + 1:N_BRANCHES + 6]
    ret_state, kt_buf, v_buf, pool_hist, conv_hist, xbc_buf, ssd_state = rest[N_BRANCHES + 6:]
    first_tile = pl.program_id(1) == 0

    @pl.when(first_tile)
    def _():
        ret_state[...] = jnp.zeros_like(ret_state)
        ssd_state[...] = jnp.zeros_like(ssd_state)
        kt_buf[:, :, 0:CHUNK] = jnp.zeros((SWA_HKV, LANES, CHUNK), _BF16)
        v_buf[0:CHUNK, :] = jnp.zeros((CHUNK, SWA_HKV * SWA_EXT_W), _BF16)
        pool_hist[0:CHUNK, :] = jnp.zeros((CHUNK, BRANCH), _F32)
        conv_hist[0:CONV_HIST, :] = jnp.zeros((CONV_HIST, M_CONV_W), _F32)

    h = _modulated_norm(x_ref[0], mod_ref, nw_ref).astype(_BF16)
    h_o[...] = h
    for ref, w_ref in zip(proj, (wr_ref, ws_ref, wp_ref, wm_ref, wdt_ref)):
        ref[...] = _dot(h, w_ref[...]).astype(ref.dtype)

    ret_p, swa_p, pool_p, ssd_p, dt_p = proj
    _retention_tile(ret_p, ret_o, *ret_c, ret_state)
    _swa_tile(first_tile, swa_p, att_o, *swa_c, kt_buf, v_buf)
    _pool_tile(first_tile, pool_p, pool_o, *pool_c, pool_hist)
    _ssd_tile(ssd_p, dt_p, ssm_o, *ssd_c, conv_hist, xbc_buf, ssd_state)


def _mixers(x, mod, norm_w, weights, swa_sinks, pool_w, pool_scale, conv_w, conv_b, dt_bias, a_log, d_skip,
            ssm_norm_w):
    batch, seq, _ = x.shape
    consts = list(_retention_tables())
    consts += [_swa_bias(), _swa_sink_rows(swa_sinks)]
    consts += [_pool_inverse_counts(), pool_w.astype(_BF16), pool_scale.reshape(1, BRANCH)]
    consts += [conv_w, conv_b.reshape(1, M_CONV_W), _pad_lanes(dt_bias), _pad_lanes(a_log),
               jnp.repeat(d_skip.astype(_F32), M_P).reshape(1, BRANCH), ssm_norm_w.reshape(1, BRANCH),
               jnp.asarray(np.tril(np.ones((CHUNK, CHUNK), np.float32))), _expansion_matrix()]
    assert len(consts) == N_RET_T + N_SWA_T + N_POOL_T + N_SSD_T
    operands = [x, mod, norm_w, *weights, *consts]
    in_specs = [_row_spec(D_MODEL), _mod_spec()] + [_const_spec(a.shape) for a in operands[2:]]
    scratch = [pltpu.VMEM((ROW_TILE, w), _BF16) for w in MIXER_WIDTHS] + [pltpu.VMEM((ROW_TILE, DT_PAD), _F32)]
    scratch += [pltpu.VMEM((RET_QK_W, RET_V_W), _F32),
                pltpu.VMEM((SWA_HKV, LANES, CHUNK + ROW_TILE), _BF16),
                pltpu.VMEM((CHUNK + ROW_TILE, SWA_HKV * SWA_EXT_W), _BF16),
                pltpu.VMEM((CHUNK + ROW_TILE, BRANCH), _F32),
                pltpu.VMEM((CONV_HIST + ROW_TILE, M_CONV_W), _F32),
                pltpu.VMEM((ROW_TILE, M_CONV_W), _F32),
                pltpu.VMEM((M_GROUPS, M_N, M_GW), _F32)]
    return pl.pallas_call(
        _mixers_kernel,
        grid=(batch, seq // ROW_TILE),
        in_specs=in_specs,
        out_specs=[_row_spec(BRANCH)] * N_BRANCHES + [_row_spec(D_MODEL)],
        out_shape=[jax.ShapeDtypeStruct((batch, seq, BRANCH), _BF16)] * N_BRANCHES
        + [jax.ShapeDtypeStruct((batch, seq, D_MODEL), _BF16)],
        scratch_shapes=scratch,
        compiler_params=_params(),
        name="norm_proj_mixers",
    )(*operands)


def _merge_kernel(final, x_ref, mod_ref, h_ref, wg_ref, ret_ref, att_ref, pool_ref, ssm_ref, wup_ref, wout_ref,
                  *rest):
    if final:
        fnw_ref, o_ref = rest
    else:
        (o_ref,) = rest
    h = h_ref[0]
    merged = None
    for i, br_ref in enumerate((ret_ref, att_ref, pool_ref, ssm_ref)):
        gate = _sigmoid(_dot(h, wg_ref[:, i * D_MODEL:(i + 1) * D_MODEL]))
        term = gate * _dot(br_ref[0], wup_ref[i])
        merged = term if merged is None else merged + term
    out = _dot(merged.astype(_BF16), wout_ref[...])
    y = x_ref[0] + mod_ref[0, :, 2 * D_MODEL:3 * D_MODEL] * out
    if final:
        y = y * lax.rsqrt(jnp.mean(y * y, axis=-1, keepdims=True) + EPS) * fnw_ref[...]
    o_ref[0] = y


def _merge(x, mod, h, wg, branches, w_up, w_out, final_norm_w):
    batch, seq, _ = x.shape
    final = final_norm_w is not None
    in_specs = [_row_spec(D_MODEL), _mod_spec(), _row_spec(D_MODEL), _const_spec(wg.shape)]
    in_specs += [_row_spec(BRANCH)] * N_BRANCHES
    in_specs += [_const_spec(w_up.shape), _const_spec(w_out.shape)]
    args = [x, mod, h, wg, *branches, w_up, w_out]
    if final:
        in_specs.append(_const_spec((1, D_MODEL)))
        args.append(final_norm_w.reshape(1, D_MODEL))
    return pl.pallas_call(
        functools.partial(_merge_kernel, final),
        grid=(batch, seq // ROW_TILE),
        in_specs=in_specs,
        out_specs=_row_spec(D_MODEL),
        out_shape=jax.ShapeDtypeStruct(x.shape, _F32),
        compiler_params=_params(),
        name="merge_out_proj",
    )(*args)


def kernel(x, c, ada_w, ada_b, norm_w, w_in, swa_sinks, pool_w, pool_scale, conv_w, conv_b, dt_bias, a_log,
           d_skip, ssm_norm_w, w_up, w_out, final_norm_w):
    depth = ada_w.shape[0]
    batch = x.shape[0]
    mod_all = _modulation(c, ada_w, ada_b)
    w_in_bf = w_in.astype(_BF16)
    for l in range(depth):
        mod = mod_all[l].reshape(batch, 1, 3 * D_MODEL)
        nw = norm_w[l].reshape(1, D_MODEL)
        w = w_in_bf[l]
        weights = [w[:, a:b] for a, b in (RET_COLS, SWA_COLS, POOL_COLS, SSD_COLS)]
        weights.append(jnp.pad(w[:, DT_COLS[0]:DT_COLS[1]], ((0, 0), (0, DT_PAD - M_HEADS))))
        wg = w[:, MG_COLS[0]:MG_COLS[1]]
        *branches, h = _mixers(x, mod, nw, weights, swa_sinks[l], pool_w[l], pool_scale[l], conv_w[l], conv_b[l],
                               dt_bias[l], a_log[l], d_skip[l], ssm_norm_w[l])
        x = _merge(x, mod, h, wg, branches, w_up[l].astype(_BF16), w_out[l].astype(_BF16),
                   final_norm_w if l == depth - 1 else None)
    return x
```

```python
import functools

import jax
import jax.numpy as jnp
import numpy as np
from jax import lax
from jax.experimental import pallas as pl
from jax.experimental.pallas import tpu as pltpu

D_MODEL = 1024
BRANCH = 512
N_BRANCHES = 4
EPS = 1e-6
CHUNK = 128
LANES = 128

RET_HEADS, RET_DK, RET_DV = 4, 64, 128
RET_QK_W = RET_HEADS * RET_DK
RET_V_W = RET_HEADS * RET_DV
SWA_HQ, SWA_HKV, SWA_D = 8, 2, 64
SWA_GROUP = SWA_HQ // SWA_HKV
SWA_Q_W = SWA_HQ * SWA_D
SWA_KV_W = SWA_HKV * SWA_D
SWA_EXT_W = 2 * LANES
POOL_WINDOWS = (2, 4, 8, 16)
POOL_GD = BRANCH // len(POOL_WINDOWS)
M_HEADS, M_P, M_GROUPS, M_N, M_CONV = 8, 64, 2, 128, 4
M_R = M_HEADS // M_GROUPS
M_GW = M_R * M_P
M_BC = M_GROUPS * M_N
M_X_W = M_HEADS * M_P
M_CONV_W = M_X_W + 2 * M_BC
CONV_HIST = 8
DT_PAD = LANES

_SIZES = (256, 256, 512, 512, 512, 128, 128, 512, 512, 512, 1024, 512, M_HEADS, N_BRANCHES * D_MODEL)
_OFFS = np.concatenate([[0], np.cumsum(_SIZES)]).tolist()
RET_COLS = (_OFFS[0], _OFFS[4])
SWA_COLS = (_OFFS[4], _OFFS[8])
POOL_COLS = (_OFFS[8], _OFFS[10])
SSD_COLS = (_OFFS[10], _OFFS[12])
DT_COLS = (_OFFS[12], _OFFS[13])
MG_COLS = (_OFFS[13], _OFFS[14])
MIXER_WIDTHS = tuple(b - a for a, b in (RET_COLS, SWA_COLS, POOL_COLS, SSD_COLS))

ROW_TILE = 512
N_CHUNKS = ROW_TILE // CHUNK
V7X_VMEM_LIMIT = 56 * 1024 * 1024

_F32 = jnp.float32
_BF16 = jnp.bfloat16


def _sigmoid(x):
    return 0.5 * jnp.tanh(0.5 * x) + 0.5


def _silu(x):
    half = 0.5 * x
    return half + half * jnp.tanh(half)


def _softplus(x):
    return jnp.maximum(x, 0.0) + jnp.log(1.0 + jnp.exp(-jnp.abs(x)))


def _dot(a, b):
    return jnp.dot(a, b, preferred_element_type=_F32)


def _dot_nt(a, b):
    return lax.dot_general(a, b, (((1,), (1,)), ((), ())), preferred_element_type=_F32)


def _dot_tn(a, b):
    return lax.dot_general(a, b, (((0,), (0,)), ((), ())), preferred_element_type=_F32)


def _const_spec(shape):
    nd = len(shape)
    return pl.BlockSpec(shape, lambda *_: (0,) * nd, pipeline_mode=pl.Buffered(1))


def _weight_window_spec(layer, cols):
    return pl.BlockSpec((pl.Element(1), pl.Element(D_MODEL), pl.Element(cols[1] - cols[0])),
                        lambda *_: (layer, 0, cols[0]), pipeline_mode=pl.Buffered(1))


def _row_spec(width):
    return pl.BlockSpec((1, ROW_TILE, width), lambda b, t: (b, t, 0))


def _mod_spec():
    return pl.BlockSpec((1, 1, 3 * D_MODEL), lambda b, t: (b, 0, 0))


def _params():
    return pltpu.CompilerParams(dimension_semantics=("parallel", "arbitrary"),
                                vmem_limit_bytes=V7X_VMEM_LIMIT)


def _modulated_norm(x, mod_ref, nw_ref):
    shift = mod_ref[0, :, 0:D_MODEL]
    scale = mod_ref[0, :, D_MODEL:2 * D_MODEL]
    rs = lax.rsqrt(jnp.mean(x * x, axis=-1, keepdims=True) + EPS)
    return (x * rs) * nw_ref[...] * (1.0 + scale) + shift


def _mod_kernel(c_ref, w_ref, b_ref, o_ref):
    sc = _silu(c_ref[...]).astype(_BF16)
    o_ref[0] = _dot(sc, w_ref[0].astype(_BF16)) + b_ref[0]


def _modulation(c, ada_w, ada_b):
    depth, _, n = ada_w.shape
    bn = D_MODEL
    batch = c.shape[0]
    return pl.pallas_call(
        _mod_kernel,
        grid=(depth, n // bn),
        in_specs=[pl.BlockSpec((batch, D_MODEL), lambda l, j: (0, 0)),
                  pl.BlockSpec((1, D_MODEL, bn), lambda l, j: (l, 0, j)),
                  pl.BlockSpec((1, 1, bn), lambda l, j: (l, 0, j))],
        out_specs=pl.BlockSpec((1, batch, bn), lambda l, j: (l, 0, j)),
        out_shape=jax.ShapeDtypeStruct((depth, batch, n), _F32),
        name="adaln_modulation",
    )(c, ada_w, ada_b.reshape(depth, 1, n))


def _retention_tables():
    log_g = np.log(1.0 - 2.0 ** (-5.0 - np.arange(RET_HEADS, dtype=np.float64)))
    pos = np.arange(CHUNK, dtype=np.float64)
    diff = pos[:, None] - pos[None, :]
    inner = np.where(diff >= 0, np.exp(log_g[:, None, None] * np.where(diff >= 0, diff, 0.0)), 0.0)
    q_decay = np.repeat(np.exp(log_g[:, None] * (pos + 1.0)).T, RET_DV, axis=1)
    k_decay = np.repeat(np.exp(log_g[:, None] * (CHUNK - 1.0 - pos)).T, RET_DK, axis=1)
    chunk_decay = np.repeat(np.exp(log_g * CHUNK), RET_DV)[None, :]
    block_diag = (np.arange(RET_QK_W)[:, None] // RET_DK == np.arange(RET_V_W)[None, :] // RET_DV)
    return tuple(jnp.asarray(t, _F32) for t in (inner, q_decay, k_decay, chunk_decay, block_diag))


def _retention_tile(p, o, inner_ref, qd_ref, kd_ref, cd_ref, bd_ref, state_ref):
    head_of_lane = lax.broadcasted_iota(jnp.int32, (CHUNK, RET_QK_W), 1) // RET_DK
    for ci in range(N_CHUNKS):
        rows = slice(ci * CHUNK, (ci + 1) * CHUNK)
        q = p[rows, 0:RET_QK_W]
        k = p[rows, RET_QK_W:2 * RET_QK_W].astype(_F32) * (RET_DK ** -0.5)
        k_bf = k.astype(_BF16)
        v = p[rows, 2 * RET_QK_W:2 * RET_QK_W + RET_V_W]
        state = state_ref[...]
        cross = _dot(q, state.astype(_BF16)) * qd_ref[...]
        update = _dot_tn((k * kd_ref[...]).astype(_BF16), v)
        state_ref[...] = state * cd_ref[...] + update * bd_ref[...]
        for hd in range(RET_HEADS):
            lanes = slice(hd * RET_DV, (hd + 1) * RET_DV)
            q_h = jnp.where(head_of_lane == hd, q, jnp.zeros_like(q))
            scores = _dot_nt(q_h, k_bf) * inner_ref[hd]
            out = _dot(scores.astype(_BF16), v[:, lanes]) + cross[:, lanes]
            out = out * lax.rsqrt(jnp.mean(out * out, axis=-1, keepdims=True) + EPS)
            g0 = 2 * RET_QK_W + RET_V_W + hd * RET_DV
            gate = p[rows, g0:g0 + RET_DV].astype(_F32)
            o[rows, lanes] = (out * _silu(gate)).astype(_BF16)


def _swa_bias():
    slopes = 2.0 ** (-8.0 * np.arange(1, SWA_HQ + 1, dtype=np.float64) / SWA_HQ)
    qi = np.arange(CHUNK)
    kj = np.arange(2 * CHUNK)
    delta = CHUNK + qi[:, None] - kj[None, :]
    valid = (delta >= 0) & (delta < CHUNK)
    variants = []
    for ok in (valid, valid & (kj[None, :] >= CHUNK)):
        bias = np.where(ok[None], -slopes[:, None, None] * delta[None].astype(np.float64), -np.inf)
        variants.append(bias.reshape(SWA_HKV, SWA_GROUP * CHUNK, 2 * CHUNK))
    return jnp.asarray(np.stack(variants), _F32)


def _swa_sink_rows(sinks):
    rows = jnp.repeat(sinks.astype(_F32), CHUNK)[:, None]
    return jnp.broadcast_to(rows, (SWA_HQ * CHUNK, LANES)).reshape(SWA_HKV, SWA_GROUP * CHUNK, LANES)


def _swa_tile(first_tile, p, o, bias_ref, sink_ref, kt_buf, v_buf):
    low_half = lax.broadcasted_iota(jnp.int32, (ROW_TILE, SWA_KV_W), 1) < SWA_D
    k = p[:, SWA_Q_W:SWA_Q_W + SWA_KV_W].astype(_F32)
    k_swapped = pltpu.roll(k, SWA_D, axis=1)
    for j, dup in enumerate((jnp.where(low_half, k, k_swapped), jnp.where(low_half, k_swapped, k))):
        for bi in range(N_CHUNKS):
            kt_buf[j, :, (bi + 1) * CHUNK:(bi + 2) * CHUNK] = dup[bi * CHUNK:(bi + 1) * CHUNK, :].T.astype(_BF16)
    v = p[:, SWA_Q_W + SWA_KV_W:SWA_Q_W + 2 * SWA_KV_W].astype(_F32)
    v_swapped = pltpu.roll(v, SWA_D, axis=1)
    for j, dup in enumerate((jnp.where(low_half, v, v_swapped), jnp.where(low_half, v_swapped, v))):
        v_buf[CHUNK:CHUNK + ROW_TILE, j * SWA_EXT_W:j * SWA_EXT_W + LANES] = dup.astype(_BF16)
        v_buf[CHUNK:CHUNK + ROW_TILE, j * SWA_EXT_W + LANES:(j + 1) * SWA_EXT_W] = jnp.ones((ROW_TILE, LANES), _BF16)

    low_lanes = lax.broadcasted_iota(jnp.int32, (CHUNK, LANES), 1) < SWA_D
    high_lanes = jnp.logical_not(low_lanes)
    first_variant = first_tile.astype(jnp.int32)
    gate0 = SWA_Q_W + 2 * SWA_KV_W

    for bi in range(N_CHUNKS):
        rows = slice(bi * CHUNK, (bi + 1) * CHUNK)
        keys = slice(bi * CHUNK, (bi + 2) * CHUNK)
        variant = first_variant if bi == 0 else 0
        for j in range(SWA_HKV):
            k_t = kt_buf[j, :, keys]
            v_ext = v_buf[keys, j * SWA_EXT_W:(j + 1) * SWA_EXT_W]
            for pair in range(SWA_GROUP // 2):
                c0 = (j * SWA_GROUP // 2 + pair) * LANES
                qp = p[rows, c0:c0 + LANES].astype(_F32) * (SWA_D ** -0.5)
                halves = []
                for odd in range(2):
                    g_rows = slice((2 * pair + odd) * CHUNK, (2 * pair + odd + 1) * CHUNK)
                    q_h = jnp.where(high_lanes if odd else low_lanes, qp, 0.0).astype(_BF16)
                    s = _dot(q_h, k_t) + bias_ref[variant, j, g_rows, :]
                    s_prev, s_cur = s[:, 0:CHUNK], s[:, CHUNK:2 * CHUNK]
                    sink = sink_ref[j, g_rows, :]
                    m = jnp.maximum(jnp.max(jnp.maximum(s_prev, s_cur), axis=-1, keepdims=True), sink)
                    e = jnp.concatenate([jnp.exp(s_prev - m), jnp.exp(s_cur - m)], axis=1).astype(_BF16)
                    ov = _dot(e, v_ext)
                    halves.append(ov[:, 0:LANES] / (ov[:, LANES:2 * LANES] + jnp.exp(sink - m)))
                gate = p[rows, gate0 + c0:gate0 + c0 + LANES].astype(_F32)
                o[rows, c0:c0 + LANES] = (jnp.where(low_lanes, halves[0], halves[1]) * _silu(gate)).astype(_BF16)

    kt_buf[:, :, 0:CHUNK] = kt_buf[:, :, ROW_TILE:ROW_TILE + CHUNK]
    v_buf[0:CHUNK, :] = v_buf[ROW_TILE:ROW_TILE + CHUNK, :]


def _pool_inverse_counts():
    pos = np.arange(ROW_TILE, dtype=np.float64)[:, None]
    windows = np.repeat(np.asarray(POOL_WINDOWS, np.float64), POOL_GD)[None, :]
    steady = np.broadcast_to(1.0 / windows, (ROW_TILE, BRANCH))
    start = 1.0 / np.minimum(pos + 1.0, windows)
    return jnp.asarray(np.stack([steady, start]), _F32)


def _pool_tile(first_tile, p, o, inv_ref, w_ref, scale_ref, hist):
    hist[CHUNK:CHUNK + ROW_TILE, :] = p[:, 0:BRANCH].astype(_F32)
    inv = inv_ref[first_tile.astype(jnp.int32)]
    for g, w in enumerate(POOL_WINDOWS):
        lanes = slice(g * POOL_GD, (g + 1) * POOL_GD)
        u = hist[CHUNK:CHUNK + ROW_TILE, lanes]
        acc = u
        for j in range(1, w):
            acc = acc + hist[CHUNK - j:CHUNK - j + ROW_TILE, lanes]
        diff = acc * inv[:, lanes] - u
        y = _dot(diff.astype(_BF16), w_ref[g])
        gate = p[:, BRANCH + g * POOL_GD:BRANCH + (g + 1) * POOL_GD].astype(_F32)
        o[:, lanes] = (y * scale_ref[:, lanes] * _silu(gate)).astype(_BF16)
    hist[0:CHUNK, :] = hist[ROW_TILE:ROW_TILE + CHUNK, :]


def _expansion_matrix():
    e = np.zeros((2 * LANES, M_X_W), np.float32)
    for hd in range(M_HEADS):
        e[hd, hd * M_P:(hd + 1) * M_P] = 1.0
        e[LANES + hd, hd * M_P:(hd + 1) * M_P] = 1.0
    return jnp.asarray(e, _BF16)


def _pad_lanes(v, width=DT_PAD):
    return jnp.pad(v.astype(_F32), (0, width - v.shape[0])).reshape(1, width)


def _ssd_tile(p, dt_raw, o, cw_ref, cb_ref, dtb_ref, alog_ref, dskip_ref, nw_ref, tri_ref, exp_ref,
              hist, xbc_buf, state_ref):
    hist[CONV_HIST:CONV_HIST + ROW_TILE, :] = p[:, 0:M_CONV_W].astype(_F32)
    acc = cb_ref[...] + cw_ref[M_CONV - 1:M_CONV, :] * hist[CONV_HIST:CONV_HIST + ROW_TILE, :]
    for j in range(1, M_CONV):
        acc = acc + cw_ref[M_CONV - 1 - j:M_CONV - j, :] * hist[CONV_HIST - j:CONV_HIST - j + ROW_TILE, :]
    xbc_buf[...] = _silu(acc)
    hist[0:CONV_HIST, :] = hist[ROW_TILE:ROW_TILE + CONV_HIST, :]

    a_row = -jnp.exp(alog_ref[...])
    causal = (lax.broadcasted_iota(jnp.int32, (CHUNK, CHUNK), 0)
              >= lax.broadcasted_iota(jnp.int32, (CHUNK, CHUNK), 1))
    head_of_lane = lax.broadcasted_iota(jnp.int32, (CHUNK, M_GW), 1) // M_P

    def expand(vals):
        stacked = jnp.concatenate(vals, axis=0)
        hi = stacked.astype(_BF16)
        lo = (stacked - hi.astype(_F32)).astype(_BF16)
        wide = _dot(jnp.concatenate([hi, lo], axis=1), exp_ref[...])
        return [wide[i * CHUNK:(i + 1) * CHUNK, :] for i in range(len(vals))]

    for ci in range(N_CHUNKS):
        rows = slice(ci * CHUNK, (ci + 1) * CHUNK)
        dt = _softplus(dt_raw[rows, :] + dtb_ref[...])
        acs = jnp.dot(tri_ref[...], dt * a_row, preferred_element_type=_F32, precision=lax.Precision.HIGHEST)
        acs_t = acs.T
        last = acs[CHUNK - 1:CHUNK, :]
        dt_x, w_x, ex_x = expand([dt, dt * jnp.exp(last - acs), jnp.exp(acs)])
        x = xbc_buf[rows, 0:M_X_W]
        xdt = (x * dt_x).astype(_BF16)
        xd = (x * w_x).astype(_BF16)
        for g in range(M_GROUPS):
            lanes = slice(g * M_GW, (g + 1) * M_GW)
            bmat = xbc_buf[rows, M_X_W + g * M_N:M_X_W + (g + 1) * M_N].astype(_BF16)
            cmat = xbc_buf[rows, M_X_W + M_BC + g * M_N:M_X_W + M_BC + (g + 1) * M_N].astype(_BF16)
            cb = _dot_nt(cmat, bmat)
            state = state_ref[g]
            ex_g = ex_x[:, lanes]
            y = _dot(cmat, state.astype(_BF16)) * ex_g + x[:, lanes] * dskip_ref[:, lanes]
            xdt_g = xdt[:, lanes]
            for r in range(M_R):
                hd = g * M_R + r
                seg = acs[:, hd:hd + 1] - acs_t[hd:hd + 1, :]
                lmat = jnp.exp(jnp.where(causal, seg, -jnp.inf))
                x_h = jnp.where(head_of_lane == r, xdt_g, jnp.zeros_like(xdt_g))
                y = y + _dot((cb * lmat).astype(_BF16), x_h)
            state_ref[g] = state * ex_g[CHUNK - 1:CHUNK, :] + _dot_tn(bmat, xd[:, lanes])
            z = p[rows, M_CONV_W + g * M_GW:M_CONV_W + (g + 1) * M_GW].astype(_F32)
            yz = y * _silu(z)
            yz = yz * lax.rsqrt(jnp.mean(yz * yz, axis=-1, keepdims=True) + EPS)
            o[rows, lanes] = (yz * nw_ref[:, lanes]).astype(_BF16)


N_RET_T, N_SWA_T, N_POOL_T, N_SSD_T = 5, 2, 3, 8


def _mixers_kernel(x_ref, mod_ref, nw_ref, wr_ref, ws_ref, wp_ref, wm_ref, wdt_ref, *rest):
    consts, rest = rest[:N_RET_T + N_SWA_T + N_POOL_T + N_SSD_T], rest[N_RET_T + N_SWA_T + N_POOL_T + N_SSD_T:]
    ret_c, consts = consts[:N_RET_T], consts[N_RET_T:]
    swa_c, consts = consts[:N_SWA_T], consts[N_SWA_T:]
    pool_c, ssd_c = consts[:N_POOL_T], consts[N_POOL_T:]
    ret_o, att_o, pool_o, ssm_o, h_o = (r.at[0] for r in rest[:N_BRANCHES + 1])
    proj = rest[N_BRANCHES + 1:N_BRANCHES + 6]
    ret_state, kt_buf, v_buf, pool_hist, conv_hist, xbc_buf, ssd_state = rest[N_BRANCHES + 6:]
    first_tile = pl.program_id(1) == 0

    @pl.when(first_tile)
    def _():
        ret_state[...] = jnp.zeros_like(ret_state)
        ssd_state[...] = jnp.zeros_like(ssd_state)
        kt_buf[:, :, 0:CHUNK] = jnp.zeros((SWA_HKV, LANES, CHUNK), _BF16)
        v_buf[0:CHUNK, :] = jnp.zeros((CHUNK, SWA_HKV * SWA_EXT_W), _BF16)
        pool_hist[0:CHUNK, :] = jnp.zeros((CHUNK, BRANCH), _F32)
        conv_hist[0:CONV_HIST, :] = jnp.zeros((CONV_HIST, M_CONV_W), _F32)

    h = _modulated_norm(x_ref[0], mod_ref, nw_ref).astype(_BF16)
    h_o[...] = h
    for ref, w_ref in zip(proj, (wr_ref.at[0], ws_ref.at[0], wp_ref.at[0], wm_ref.at[0], wdt_ref)):
        ref[...] = _dot(h, w_ref[...]).astype(ref.dtype)

    ret_p, swa_p, pool_p, ssd_p, dt_p = proj
    _retention_tile(ret_p, ret_o, *ret_c, ret_state)
    _swa_tile(first_tile, swa_p, att_o, *swa_c, kt_buf, v_buf)
    _pool_tile(first_tile, pool_p, pool_o, *pool_c, pool_hist)
    _ssd_tile(ssd_p, dt_p, ssm_o, *ssd_c, conv_hist, xbc_buf, ssd_state)


def _mixers(layer, x, mod, norm_w, w_in, w_dt, swa_sinks, pool_w, pool_scale, conv_w, conv_b, dt_bias, a_log,
            d_skip, ssm_norm_w):
    batch, seq, _ = x.shape
    windows = (RET_COLS, SWA_COLS, POOL_COLS, SSD_COLS)
    consts = list(_retention_tables())
    consts += [_swa_bias(), _swa_sink_rows(swa_sinks)]
    consts += [_pool_inverse_counts(), pool_w.astype(_BF16), pool_scale.reshape(1, BRANCH)]
    consts += [conv_w, conv_b.reshape(1, M_CONV_W), _pad_lanes(dt_bias), _pad_lanes(a_log),
               jnp.repeat(d_skip.astype(_F32), M_P).reshape(1, BRANCH), ssm_norm_w.reshape(1, BRANCH),
               jnp.asarray(np.tril(np.ones((CHUNK, CHUNK), np.float32))), _expansion_matrix()]
    assert len(consts) == N_RET_T + N_SWA_T + N_POOL_T + N_SSD_T
    operands = [x, mod, norm_w] + [w_in] * len(windows) + [w_dt, *consts]
    in_specs = [_row_spec(D_MODEL), _mod_spec(), _const_spec(norm_w.shape)]
    in_specs += [_weight_window_spec(layer, cols) for cols in windows]
    in_specs += [_const_spec(a.shape) for a in [w_dt, *consts]]
    scratch = [pltpu.VMEM((ROW_TILE, w), _BF16) for w in MIXER_WIDTHS] + [pltpu.VMEM((ROW_TILE, DT_PAD), _F32)]
    scratch += [pltpu.VMEM((RET_QK_W, RET_V_W), _F32),
                pltpu.VMEM((SWA_HKV, LANES, CHUNK + ROW_TILE), _BF16),
                pltpu.VMEM((CHUNK + ROW_TILE, SWA_HKV * SWA_EXT_W), _BF16),
                pltpu.VMEM((CHUNK + ROW_TILE, BRANCH), _F32),
                pltpu.VMEM((CONV_HIST + ROW_TILE, M_CONV_W), _F32),
                pltpu.VMEM((ROW_TILE, M_CONV_W), _F32),
                pltpu.VMEM((M_GROUPS, M_N, M_GW), _F32)]
    return pl.pallas_call(
        _mixers_kernel,
        grid=(batch, seq // ROW_TILE),
        in_specs=in_specs,
        out_specs=[_row_spec(BRANCH)] * N_BRANCHES + [_row_spec(D_MODEL)],
        out_shape=[jax.ShapeDtypeStruct((batch, seq, BRANCH), _BF16)] * N_BRANCHES
        + [jax.ShapeDtypeStruct((batch, seq, D_MODEL), _BF16)],
        scratch_shapes=scratch,
        compiler_params=_params(),
        name="norm_proj_mixers",
    )(*operands)


def _merge_kernel(final, x_ref, mod_ref, h_ref, wg_ref, ret_ref, att_ref, pool_ref, ssm_ref, wup_ref, wout_ref,
                  *rest):
    if final:
        fnw_ref, o_ref = rest
    else:
        (o_ref,) = rest
    h = h_ref[0]
    merged = None
    for i, br_ref in enumerate((ret_ref, att_ref, pool_ref, ssm_ref)):
        gate = _sigmoid(_dot(h, wg_ref[:, i * D_MODEL:(i + 1) * D_MODEL]))
        term = gate * _dot(br_ref[0], wup_ref[i])
        merged = term if merged is None else merged + term
    out = _dot(merged.astype(_BF16), wout_ref[...])
    y = x_ref[0] + mod_ref[0, :, 2 * D_MODEL:3 * D_MODEL] * out
    if final:
        y = y * lax.rsqrt(jnp.mean(y * y, axis=-1, keepdims=True) + EPS) * fnw_ref[...]
    o_ref[0] = y


def _merge(x, mod, h, wg, branches, w_up, w_out, final_norm_w):
    batch, seq, _ = x.shape
    final = final_norm_w is not None
    in_specs = [_row_spec(D_MODEL), _mod_spec(), _row_spec(D_MODEL), _const_spec(wg.shape)]
    in_specs += [_row_spec(BRANCH)] * N_BRANCHES
    in_specs += [_const_spec(w_up.shape), _const_spec(w_out.shape)]
    args = [x, mod, h, wg, *branches, w_up, w_out]
    if final:
        in_specs.append(_const_spec((1, D_MODEL)))
        args.append(final_norm_w.reshape(1, D_MODEL))
    return pl.pallas_call(
        functools.partial(_merge_kernel, final),
        grid=(batch, seq // ROW_TILE),
        in_specs=in_specs,
        out_specs=_row_spec(D_MODEL),
        out_shape=jax.ShapeDtypeStruct(x.shape, _F32),
        compiler_params=_params(),
        name="merge_out_proj",
    )(*args)


def kernel(x, c, ada_w, ada_b, norm_w, w_in, swa_sinks, pool_w, pool_scale, conv_w, conv_b, dt_bias, a_log,
           d_skip, ssm_norm_w, w_up, w_out, final_norm_w):
    depth = ada_w.shape[0]
    batch = x.shape[0]
    mod_all = _modulation(c, ada_w, ada_b)
    w_in_bf = w_in.astype(_BF16)
    for l in range(depth):
        mod = mod_all[l].reshape(batch, 1, 3 * D_MODEL)
        nw = norm_w[l].reshape(1, D_MODEL)
        w_dt = jnp.pad(w_in_bf[l, :, DT_COLS[0]:DT_COLS[1]], ((0, 0), (0, DT_PAD - M_HEADS)))
        *branches, h = _mixers(l, x, mod, nw, w_in_bf, w_dt, swa_sinks[l], pool_w[l], pool_scale[l], conv_w[l],
                               conv_b[l], dt_bias[l], a_log[l], d_skip[l], ssm_norm_w[l])
        wg = w_in_bf[l, :, MG_COLS[0]:MG_COLS[1]]
        x = _merge(x, mod, h, wg, branches, w_up[l].astype(_BF16), w_out[l].astype(_BF16),
                   final_norm_w if l == depth - 1 else None)
    return x
```

```python
import functools

import jax
import jax.numpy as jnp
import numpy as np
from jax import lax
from jax.experimental import pallas as pl
from jax.experimental.pallas import tpu as pltpu

D_MODEL = 1024
BRANCH = 512
N_BRANCHES = 4
EPS = 1e-6
CHUNK = 128
LANES = 128

RET_HEADS, RET_DK, RET_DV = 4, 64, 128
RET_QK_W = RET_HEADS * RET_DK
RET_V_W = RET_HEADS * RET_DV
SWA_HQ, SWA_HKV, SWA_D = 8, 2, 64
SWA_GROUP = SWA_HQ // SWA_HKV
SWA_Q_W = SWA_HQ * SWA_D
SWA_KV_W = SWA_HKV * SWA_D
SWA_EXT_W = 2 * LANES
LOG2_E = 1.4426950408889634
POOL_WINDOWS = (2, 4, 8, 16)
POOL_GD = BRANCH // len(POOL_WINDOWS)
POOL_EXT = 16
POOL_PAD = 8
M_HEADS, M_P, M_GROUPS, M_N, M_CONV = 8, 64, 2, 128, 4
M_R = M_HEADS // M_GROUPS
M_GW = M_R * M_P
M_BC = M_GROUPS * M_N
M_X_W = M_HEADS * M_P
M_CONV_W = M_X_W + 2 * M_BC
CONV_HIST = 8
DT_PAD = LANES

_SIZES = (256, 256, 512, 512, 512, 128, 128, 512, 512, 512, 1024, 512, M_HEADS, N_BRANCHES * D_MODEL)
_OFFS = np.concatenate([[0], np.cumsum(_SIZES)]).tolist()
RET_COLS = (_OFFS[0], _OFFS[4])
SWA_COLS = (_OFFS[4], _OFFS[8])
POOL_COLS = (_OFFS[8], _OFFS[10])
SSD_COLS = (_OFFS[10], _OFFS[12])
DT_COLS = (_OFFS[12], _OFFS[13])
MG_COLS = (_OFFS[13], _OFFS[14])
MIXER_WIDTHS = tuple(b - a for a, b in (RET_COLS, SWA_COLS, POOL_COLS, SSD_COLS))

ROW_TILE = 512
MERGE_TILE = 1024
N_CHUNKS = ROW_TILE // CHUNK
V7X_VMEM_LIMIT = 56 * 1024 * 1024

_F32 = jnp.float32
_BF16 = jnp.bfloat16


def _sigmoid(x):
    return 0.5 * jnp.tanh(0.5 * x) + 0.5


def _silu(x):
    half = 0.5 * x
    return half + half * jnp.tanh(half)


def _softplus(x):
    return jnp.maximum(x, 0.0) + jnp.log(1.0 + jnp.exp(-jnp.abs(x)))


def _dot(a, b):
    return jnp.dot(a, b, preferred_element_type=_F32)


def _dot_nt(a, b):
    return lax.dot_general(a, b, (((1,), (1,)), ((), ())), preferred_element_type=_F32)


def _dot_tn(a, b):
    return lax.dot_general(a, b, (((0,), (0,)), ((), ())), preferred_element_type=_F32)


def _const_spec(shape):
    nd = len(shape)
    return pl.BlockSpec(shape, lambda *_: (0,) * nd, pipeline_mode=pl.Buffered(1))


def _weight_window_spec(layer, cols):
    return pl.BlockSpec((pl.Element(1), pl.Element(D_MODEL), pl.Element(cols[1] - cols[0])),
                        lambda *_: (layer, 0, cols[0]), pipeline_mode=pl.Buffered(1))


def _row_spec(width, tile=None):
    return pl.BlockSpec((1, tile or ROW_TILE, width), lambda b, t: (b, t, 0))


def _mod_spec():
    return pl.BlockSpec((1, 1, 3 * D_MODEL), lambda b, t: (b, 0, 0))


def _params():
    return pltpu.CompilerParams(dimension_semantics=("parallel", "arbitrary"),
                                vmem_limit_bytes=V7X_VMEM_LIMIT)


def _modulated_norm(x, mod_ref, nw_ref):
    shift = mod_ref[0, :, 0:D_MODEL]
    scale = mod_ref[0, :, D_MODEL:2 * D_MODEL]
    rs = lax.rsqrt(jnp.mean(x * x, axis=-1, keepdims=True) + EPS)
    return (x * rs) * nw_ref[...] * (1.0 + scale) + shift


def _mod_kernel(c_ref, w_ref, b_ref, o_ref):
    sc = _silu(c_ref[...]).astype(_BF16)
    o_ref[0] = _dot(sc, w_ref[0].astype(_BF16)) + b_ref[0]


def _modulation(c, ada_w, ada_b):
    depth, _, n = ada_w.shape
    bn = D_MODEL
    batch = c.shape[0]
    return pl.pallas_call(
        _mod_kernel,
        grid=(depth, n // bn),
        in_specs=[pl.BlockSpec((batch, D_MODEL), lambda l, j: (0, 0)),
                  pl.BlockSpec((1, D_MODEL, bn), lambda l, j: (l, 0, j)),
                  pl.BlockSpec((1, 1, bn), lambda l, j: (l, 0, j))],
        out_specs=pl.BlockSpec((1, batch, bn), lambda l, j: (l, 0, j)),
        out_shape=jax.ShapeDtypeStruct((depth, batch, n), _F32),
        name="adaln_modulation",
    )(c, ada_w, ada_b.reshape(depth, 1, n))


def _retention_tables():
    log_g = np.log(1.0 - 2.0 ** (-5.0 - np.arange(RET_HEADS, dtype=np.float64)))
    pos = np.arange(CHUNK, dtype=np.float64)
    diff = pos[:, None] - pos[None, :]
    inner = np.where(diff >= 0, np.exp(log_g[:, None, None] * np.where(diff >= 0, diff, 0.0)), 0.0)
    q_decay = np.repeat(np.exp(log_g[:, None] * (pos + 1.0)).T, RET_DV, axis=1)
    k_decay = np.repeat(np.exp(log_g[:, None] * (CHUNK - 1.0 - pos)).T, RET_DK, axis=1)
    chunk_decay = np.repeat(np.exp(log_g * CHUNK), RET_DV)[None, :]
    block_diag = (np.arange(RET_QK_W)[:, None] // RET_DK == np.arange(RET_V_W)[None, :] // RET_DV)
    return tuple(jnp.asarray(t, _F32) for t in (inner, q_decay, k_decay, chunk_decay, block_diag))


def _retention_tile(p, o, inner_ref, qd_ref, kd_ref, cd_ref, bd_ref, state_ref):
    head_of_lane = lax.broadcasted_iota(jnp.int32, (CHUNK, RET_QK_W), 1) // RET_DK
    for ci in range(N_CHUNKS):
        rows = slice(ci * CHUNK, (ci + 1) * CHUNK)
        q = p[rows, 0:RET_QK_W]
        k = p[rows, RET_QK_W:2 * RET_QK_W].astype(_F32) * (RET_DK ** -0.5)
        k_bf = k.astype(_BF16)
        v = p[rows, 2 * RET_QK_W:2 * RET_QK_W + RET_V_W]
        state = state_ref[...]
        cross = _dot(q, state.astype(_BF16)) * qd_ref[...]
        update = _dot_tn((k * kd_ref[...]).astype(_BF16), v)
        state_ref[...] = state * cd_ref[...] + update * bd_ref[...]
        for hd in range(RET_HEADS):
            lanes = slice(hd * RET_DV, (hd + 1) * RET_DV)
            q_h = jnp.where(head_of_lane == hd, q, jnp.zeros_like(q))
            scores = _dot_nt(q_h, k_bf) * inner_ref[hd]
            out = _dot(scores.astype(_BF16), v[:, lanes]) + cross[:, lanes]
            out = out * lax.rsqrt(jnp.mean(out * out, axis=-1, keepdims=True) + EPS)
            g0 = 2 * RET_QK_W + RET_V_W + hd * RET_DV
            gate = p[rows, g0:g0 + RET_DV].astype(_F32)
            o[rows, lanes] = (out * _silu(gate)).astype(_BF16)


def _swa_bias():
    slopes = 2.0 ** (-8.0 * np.arange(1, SWA_HQ + 1, dtype=np.float64) / SWA_HQ)
    qi = np.arange(CHUNK)
    kj = np.arange(2 * CHUNK)
    delta = CHUNK + qi[:, None] - kj[None, :]
    valid = (delta >= 0) & (delta < CHUNK)
    variants = []
    for ok in (valid, valid & (kj[None, :] >= CHUNK)):
        bias = np.where(ok[None], -slopes[:, None, None] * delta[None].astype(np.float64), -np.inf)
        variants.append(bias.reshape(SWA_HKV, SWA_GROUP * CHUNK, 2 * CHUNK))
    return jnp.asarray(np.stack(variants) * LOG2_E, _F32)


def _swa_sink_rows(sinks):
    rows = jnp.repeat(sinks.astype(_F32) * LOG2_E, CHUNK)[:, None]
    return jnp.broadcast_to(rows, (SWA_HQ * CHUNK, LANES)).reshape(SWA_HKV, SWA_GROUP * CHUNK, LANES)


def _swa_tile(first_tile, p, o, bias_ref, sink_ref, kt_buf, v_buf):
    low_half = lax.broadcasted_iota(jnp.int32, (ROW_TILE, SWA_KV_W), 1) < SWA_D
    k = p[:, SWA_Q_W:SWA_Q_W + SWA_KV_W].astype(_F32)
    k_swapped = pltpu.roll(k, SWA_D, axis=1)
    for j, dup in enumerate((jnp.where(low_half, k, k_swapped), jnp.where(low_half, k_swapped, k))):
        for bi in range(N_CHUNKS):
            kt_buf[j, :, (bi + 1) * CHUNK:(bi + 2) * CHUNK] = dup[bi * CHUNK:(bi + 1) * CHUNK, :].T.astype(_BF16)
    v = p[:, SWA_Q_W + SWA_KV_W:SWA_Q_W + 2 * SWA_KV_W].astype(_F32)
    v_swapped = pltpu.roll(v, SWA_D, axis=1)
    for j, dup in enumerate((jnp.where(low_half, v, v_swapped), jnp.where(low_half, v_swapped, v))):
        v_buf[CHUNK:CHUNK + ROW_TILE, j * SWA_EXT_W:j * SWA_EXT_W + LANES] = dup.astype(_BF16)
        v_buf[CHUNK:CHUNK + ROW_TILE, j * SWA_EXT_W + LANES:(j + 1) * SWA_EXT_W] = jnp.ones((ROW_TILE, LANES), _BF16)

    low_lanes = lax.broadcasted_iota(jnp.int32, (CHUNK, LANES), 1) < SWA_D
    high_lanes = jnp.logical_not(low_lanes)
    first_variant = first_tile.astype(jnp.int32)
    gate0 = SWA_Q_W + 2 * SWA_KV_W

    for bi in range(N_CHUNKS):
        rows = slice(bi * CHUNK, (bi + 1) * CHUNK)
        keys = slice(bi * CHUNK, (bi + 2) * CHUNK)
        variant = first_variant if bi == 0 else 0
        for j in range(SWA_HKV):
            k_t = kt_buf[j, :, keys]
            v_ext = v_buf[keys, j * SWA_EXT_W:(j + 1) * SWA_EXT_W]
            for pair in range(SWA_GROUP // 2):
                c0 = (j * SWA_GROUP // 2 + pair) * LANES
                qp = p[rows, c0:c0 + LANES].astype(_F32) * (SWA_D ** -0.5 * LOG2_E)
                halves = []
                for odd in range(2):
                    g_rows = slice((2 * pair + odd) * CHUNK, (2 * pair + odd + 1) * CHUNK)
                    q_h = jnp.where(high_lanes if odd else low_lanes, qp, 0.0).astype(_BF16)
                    s = _dot(q_h, k_t) + bias_ref[variant, j, g_rows, :]
                    s_prev, s_cur = s[:, 0:CHUNK], s[:, CHUNK:2 * CHUNK]
                    sink = sink_ref[j, g_rows, :]
                    m = jnp.maximum(jnp.max(jnp.maximum(s_prev, s_cur), axis=-1, keepdims=True), sink)
                    e = jnp.concatenate([jnp.exp2(s_prev - m), jnp.exp2(s_cur - m)], axis=1).astype(_BF16)
                    ov = _dot(e, v_ext)
                    halves.append(ov[:, 0:LANES] / (ov[:, LANES:2 * LANES] + jnp.exp2(sink - m)))
                gate = p[rows, gate0 + c0:gate0 + c0 + LANES].astype(_F32)
                o[rows, c0:c0 + LANES] = (jnp.where(low_lanes, halves[0], halves[1]) * _silu(gate)).astype(_BF16)

    kt_buf[:, :, 0:CHUNK] = kt_buf[:, :, ROW_TILE:ROW_TILE + CHUNK]
    v_buf[0:CHUNK, :] = v_buf[ROW_TILE:ROW_TILE + CHUNK, :]


def _pool_inverse_counts():
    pos = np.arange(ROW_TILE, dtype=np.float64)[:, None]
    windows = np.repeat(np.asarray(POOL_WINDOWS, np.float64), POOL_GD)[None, :]
    steady = np.broadcast_to(1.0 / windows, (ROW_TILE, BRANCH))
    start = 1.0 / np.minimum(pos + 1.0, windows)
    return jnp.asarray(np.stack([steady, start]), _F32)


def _pool_tile(first_tile, p, o, inv_ref, w_ref, scale_ref, hist, levels):
    hist[CHUNK:CHUNK + ROW_TILE, :] = p[:, 0:BRANCH].astype(_F32)
    inv = inv_ref[first_tile.astype(jnp.int32)]
    ext_rows = POOL_EXT + ROW_TILE
    base = CHUNK - POOL_EXT
    sums = []
    for g, w in enumerate(POOL_WINDOWS):
        c0 = g * POOL_GD
        shift = w // 2
        if g == 0:
            total = hist[base:base + ext_rows, :] + hist[base - shift:base - shift + ext_rows, :]
        else:
            prev = levels[g - 1]
            total = (prev[POOL_PAD:POOL_PAD + ext_rows, c0:]
                     + prev[POOL_PAD - shift:POOL_PAD - shift + ext_rows, c0:])
        if g + 1 < len(POOL_WINDOWS):
            levels[g][POOL_PAD:POOL_PAD + ext_rows, c0:] = total
        sums.append(total)
    for g in range(len(POOL_WINDOWS)):
        lanes = slice(g * POOL_GD, (g + 1) * POOL_GD)
        u = hist[CHUNK:CHUNK + ROW_TILE, lanes]
        acc = sums[g][POOL_EXT:, 0:POOL_GD]
        diff = acc * inv[:, lanes] - u
        y = _dot(diff.astype(_BF16), w_ref[g])
        gate = p[:, BRANCH + g * POOL_GD:BRANCH + (g + 1) * POOL_GD].astype(_F32)
        o[:, lanes] = (y * scale_ref[:, lanes] * _silu(gate)).astype(_BF16)
    hist[0:CHUNK, :] = hist[ROW_TILE:ROW_TILE + CHUNK, :]


def _expansion_matrix():
    e = np.zeros((2 * LANES, M_X_W), np.float32)
    for hd in range(M_HEADS):
        e[hd, hd * M_P:(hd + 1) * M_P] = 1.0
        e[LANES + hd, hd * M_P:(hd + 1) * M_P] = 1.0
    return jnp.asarray(e, _BF16)


def _pad_lanes(v, width=DT_PAD):
    return jnp.pad(v.astype(_F32), (0, width - v.shape[0])).reshape(1, width)


def _ssd_tile(p, dt_raw, o, cw_ref, cb_ref, dtb_ref, alog_ref, dskip_ref, nw_ref, tri_ref, exp_ref,
              hist, xbc_buf, state_ref):
    hist[CONV_HIST:CONV_HIST + ROW_TILE, :] = p[:, 0:M_CONV_W].astype(_F32)
    acc = cb_ref[...] + cw_ref[M_CONV - 1:M_CONV, :] * hist[CONV_HIST:CONV_HIST + ROW_TILE, :]
    for j in range(1, M_CONV):
        acc = acc + cw_ref[M_CONV - 1 - j:M_CONV - j, :] * hist[CONV_HIST - j:CONV_HIST - j + ROW_TILE, :]
    xbc_buf[...] = _silu(acc)
    hist[0:CONV_HIST, :] = hist[ROW_TILE:ROW_TILE + CONV_HIST, :]

    a_row = -jnp.exp(alog_ref[...])
    causal = (lax.broadcasted_iota(jnp.int32, (CHUNK, CHUNK), 0)
              >= lax.broadcasted_iota(jnp.int32, (CHUNK, CHUNK), 1))
    head_of_lane = lax.broadcasted_iota(jnp.int32, (CHUNK, M_GW), 1) // M_P

    def expand(vals):
        stacked = jnp.concatenate(vals, axis=0)
        hi = stacked.astype(_BF16)
        lo = (stacked - hi.astype(_F32)).astype(_BF16)
        wide = _dot(jnp.concatenate([hi, lo], axis=1), exp_ref[...])
        return [wide[i * CHUNK:(i + 1) * CHUNK, :] for i in range(len(vals))]

    for ci in range(N_CHUNKS):
        rows = slice(ci * CHUNK, (ci + 1) * CHUNK)
        dt = _softplus(dt_raw[rows, :] + dtb_ref[...])
        acs = jnp.dot(tri_ref[...], dt * a_row, preferred_element_type=_F32, precision=lax.Precision.HIGHEST)
        acs_t = acs.T
        last = acs[CHUNK - 1:CHUNK, :]
        dt_x, w_x, ex_x = expand([dt, dt * jnp.exp(last - acs), jnp.exp(acs)])
        x = xbc_buf[rows, 0:M_X_W]
        xdt = (x * dt_x).astype(_BF16)
        xd = (x * w_x).astype(_BF16)
        for g in range(M_GROUPS):
            lanes = slice(g * M_GW, (g + 1) * M_GW)
            bmat = xbc_buf[rows, M_X_W + g * M_N:M_X_W + (g + 1) * M_N].astype(_BF16)
            cmat = xbc_buf[rows, M_X_W + M_BC + g * M_N:M_X_W + M_BC + (g + 1) * M_N].astype(_BF16)
            cb = _dot_nt(cmat, bmat)
            state = state_ref[g]
            ex_g = ex_x[:, lanes]
            y = _dot(cmat, state.astype(_BF16)) * ex_g + x[:, lanes] * dskip_ref[:, lanes]
            xdt_g = xdt[:, lanes]
            for r in range(M_R):
                hd = g * M_R + r
                seg = acs[:, hd:hd + 1] - acs_t[hd:hd + 1, :]
                lmat = jnp.exp(jnp.where(causal, seg, -jnp.inf))
                x_h = jnp.where(head_of_lane == r, xdt_g, jnp.zeros_like(xdt_g))
                y = y + _dot((cb * lmat).astype(_BF16), x_h)
            state_ref[g] = state * ex_g[CHUNK - 1:CHUNK, :] + _dot_tn(bmat, xd[:, lanes])
            z = p[rows, M_CONV_W + g * M_GW:M_CONV_W + (g + 1) * M_GW].astype(_F32)
            yz = y * _silu(z)
            yz = yz * lax.rsqrt(jnp.mean(yz * yz, axis=-1, keepdims=True) + EPS)
            o[rows, lanes] = (yz * nw_ref[:, lanes]).astype(_BF16)


N_RET_T, N_SWA_T, N_POOL_T, N_SSD_T = 5, 2, 3, 8


def _mixers_kernel(x_ref, mod_ref, nw_ref, wr_ref, ws_ref, wp_ref, wm_ref, wdt_ref, *rest):
    consts, rest = rest[:N_RET_T + N_SWA_T + N_POOL_T + N_SSD_T], rest[N_RET_T + N_SWA_T + N_POOL_T + N_SSD_T:]
    ret_c, consts = consts[:N_RET_T], consts[N_RET_T:]
    swa_c, consts = consts[:N_SWA_T], consts[N_SWA_T:]
    pool_c, ssd_c = consts[:N_POOL_T], consts[N_POOL_T:]
    ret_o, att_o, pool_o, ssm_o, h_o = (r.at[0] for r in rest[:N_BRANCHES + 1])
    proj = rest[N_BRANCHES + 1:N_BRANCHES + 6]
    ret_state, kt_buf, v_buf, pool_hist, conv_hist, xbc_buf, ssd_state = rest[N_BRANCHES + 6:N_BRANCHES + 13]
    pool_levels = rest[N_BRANCHES + 13:]
    first_tile = pl.program_id(1) == 0

    @pl.when(first_tile)
    def _():
        ret_state[...] = jnp.zeros_like(ret_state)
        ssd_state[...] = jnp.zeros_like(ssd_state)
        kt_buf[:, :, 0:CHUNK] = jnp.zeros((SWA_HKV, LANES, CHUNK), _BF16)
        v_buf[0:CHUNK, :] = jnp.zeros((CHUNK, SWA_HKV * SWA_EXT_W), _BF16)
        pool_hist[0:CHUNK, :] = jnp.zeros((CHUNK, BRANCH), _F32)
        for level in pool_levels:
            level[0:POOL_PAD, :] = jnp.zeros((POOL_PAD, BRANCH), _F32)
        conv_hist[0:CONV_HIST, :] = jnp.zeros((CONV_HIST, M_CONV_W), _F32)

    h = _modulated_norm(x_ref[0], mod_ref, nw_ref).astype(_BF16)
    h_o[...] = h
    for ref, w_ref in zip(proj, (wr_ref.at[0], ws_ref.at[0], wp_ref.at[0], wm_ref.at[0], wdt_ref)):
        ref[...] = _dot(h, w_ref[...]).astype(ref.dtype)

    ret_p, swa_p, pool_p, ssd_p, dt_p = proj
    _retention_tile(ret_p, ret_o, *ret_c, ret_state)
    _swa_tile(first_tile, swa_p, att_o, *swa_c, kt_buf, v_buf)
    _pool_tile(first_tile, pool_p, pool_o, *pool_c, pool_hist, pool_levels)
    _ssd_tile(ssd_p, dt_p, ssm_o, *ssd_c, conv_hist, xbc_buf, ssd_state)


def _mixers(layer, x, mod, norm_w, w_in, w_dt, swa_sinks, pool_w, pool_scale, conv_w, conv_b, dt_bias, a_log,
            d_skip, ssm_norm_w):
    batch, seq, _ = x.shape
    windows = (RET_COLS, SWA_COLS, POOL_COLS, SSD_COLS)
    consts = list(_retention_tables())
    consts += [_swa_bias(), _swa_sink_rows(swa_sinks)]
    consts += [_pool_inverse_counts(), pool_w.astype(_BF16), pool_scale.reshape(1, BRANCH)]
    consts += [conv_w, conv_b.reshape(1, M_CONV_W), _pad_lanes(dt_bias), _pad_lanes(a_log),
               jnp.repeat(d_skip.astype(_F32), M_P).reshape(1, BRANCH), ssm_norm_w.reshape(1, BRANCH),
               jnp.asarray(np.tril(np.ones((CHUNK, CHUNK), np.float32))), _expansion_matrix()]
    assert len(consts) == N_RET_T + N_SWA_T + N_POOL_T + N_SSD_T
    operands = [x, mod, norm_w] + [w_in] * len(windows) + [w_dt, *consts]
    in_specs = [_row_spec(D_MODEL), _mod_spec(), _const_spec(norm_w.shape)]
    in_specs += [_weight_window_spec(layer, cols) for cols in windows]
    in_specs += [_const_spec(a.shape) for a in [w_dt, *consts]]
    scratch = [pltpu.VMEM((ROW_TILE, w), _BF16) for w in MIXER_WIDTHS] + [pltpu.VMEM((ROW_TILE, DT_PAD), _F32)]
    scratch += [pltpu.VMEM((RET_QK_W, RET_V_W), _F32),
                pltpu.VMEM((SWA_HKV, LANES, CHUNK + ROW_TILE), _BF16),
                pltpu.VMEM((CHUNK + ROW_TILE, SWA_HKV * SWA_EXT_W), _BF16),
                pltpu.VMEM((CHUNK + ROW_TILE, BRANCH), _F32),
                pltpu.VMEM((CONV_HIST + ROW_TILE, M_CONV_W), _F32),
                pltpu.VMEM((ROW_TILE, M_CONV_W), _F32),
                pltpu.VMEM((M_GROUPS, M_N, M_GW), _F32)]
    scratch += [pltpu.VMEM((POOL_PAD + POOL_EXT + ROW_TILE, BRANCH), _F32)] * (len(POOL_WINDOWS) - 1)
    return pl.pallas_call(
        _mixers_kernel,
        grid=(batch, seq // ROW_TILE),
        in_specs=in_specs,
        out_specs=[_row_spec(BRANCH)] * N_BRANCHES + [_row_spec(D_MODEL)],
        out_shape=[jax.ShapeDtypeStruct((batch, seq, BRANCH), _BF16)] * N_BRANCHES
        + [jax.ShapeDtypeStruct((batch, seq, D_MODEL), _BF16)],
        scratch_shapes=scratch,
        compiler_params=_params(),
        name="norm_proj_mixers",
    )(*operands)


def _merge_kernel(final, x_ref, mod_ref, h_ref, wg_ref, ret_ref, att_ref, pool_ref, ssm_ref, wup_ref, wout_ref,
                  *rest):
    if final:
        fnw_ref, o_ref = rest
    else:
        (o_ref,) = rest
    h = h_ref[0]
    merged = None
    for i, br_ref in enumerate((ret_ref, att_ref, pool_ref, ssm_ref)):
        gate = _sigmoid(_dot(h, wg_ref[:, i * D_MODEL:(i + 1) * D_MODEL]))
        term = gate * _dot(br_ref[0], wup_ref[i])
        merged = term if merged is None else merged + term
    out = _dot(merged.astype(_BF16), wout_ref[...])
    y = x_ref[0] + mod_ref[0, :, 2 * D_MODEL:3 * D_MODEL] * out
    if final:
        y = y * lax.rsqrt(jnp.mean(y * y, axis=-1, keepdims=True) + EPS) * fnw_ref[...]
    o_ref[0] = y


def _merge(x, mod, h, wg, branches, w_up, w_out, final_norm_w):
    batch, seq, _ = x.shape
    final = final_norm_w is not None
    in_specs = [_row_spec(D_MODEL, MERGE_TILE), _mod_spec(), _row_spec(D_MODEL, MERGE_TILE), _const_spec(wg.shape)]
    in_specs += [_row_spec(BRANCH, MERGE_TILE)] * N_BRANCHES
    in_specs += [_const_spec(w_up.shape), _const_spec(w_out.shape)]
    args = [x, mod, h, wg, *branches, w_up, w_out]
    if final:
        in_specs.append(_const_spec((1, D_MODEL)))
        args.append(final_norm_w.reshape(1, D_MODEL))
    return pl.pallas_call(
        functools.partial(_merge_kernel, final),
        grid=(batch, seq // MERGE_TILE),
        in_specs=in_specs,
        out_specs=_row_spec(D_MODEL, MERGE_TILE),
        out_shape=jax.ShapeDtypeStruct(x.shape, _F32),
        compiler_params=_params(),
        name="merge_out_proj",
    )(*args)


def kernel(x, c, ada_w, ada_b, norm_w, w_in, swa_sinks, pool_w, pool_scale, conv_w, conv_b, dt_bias, a_log,
           d_skip, ssm_norm_w, w_up, w_out, final_norm_w):
    depth = ada_w.shape[0]
    batch = x.shape[0]
    mod_all = _modulation(c, ada_w, ada_b)
    w_in_bf = w_in.astype(_BF16)
    for l in range(depth):
        mod = mod_all[l].reshape(batch, 1, 3 * D_MODEL)
        nw = norm_w[l].reshape(1, D_MODEL)
        w_dt = jnp.pad(w_in_bf[l, :, DT_COLS[0]:DT_COLS[1]], ((0, 0), (0, DT_PAD - M_HEADS)))
        *branches, h = _mixers(l, x, mod, nw, w_in_bf, w_dt, swa_sinks[l], pool_w[l], pool_scale[l], conv_w[l],
                               conv_b[l], dt_bias[l], a_log[l], d_skip[l], ssm_norm_w[l])
        wg = w_in_bf[l, :, MG_COLS[0]:MG_COLS[1]]
        x = _merge(x, mod, h, wg, branches, w_up[l].astype(_BF16), w_out[l].astype(_BF16),
                   final_norm_w if l == depth - 1 else None)
    return x
```

```python
import functools

import jax
import jax.numpy as jnp
import numpy as np
from jax import lax
from jax.experimental import pallas as pl
from jax.experimental.pallas import tpu as pltpu

D_MODEL = 1024
BRANCH = 512
N_BRANCHES = 4
EPS = 1e-6
CHUNK = 128
LANES = 128

RET_HEADS, RET_DK, RET_DV = 4, 64, 128
RET_QK_W = RET_HEADS * RET_DK
RET_V_W = RET_HEADS * RET_DV
SWA_HQ, SWA_HKV, SWA_D = 8, 2, 64
SWA_GROUP = SWA_HQ // SWA_HKV
SWA_Q_W = SWA_HQ * SWA_D
SWA_KV_W = SWA_HKV * SWA_D
SWA_EXT_W = 2 * LANES
LOG2_E = 1.4426950408889634
POOL_WINDOWS = (2, 4, 8, 16)
POOL_GD = BRANCH // len(POOL_WINDOWS)
POOL_EXT = 16
POOL_PAD = 8
M_HEADS, M_P, M_GROUPS, M_N, M_CONV = 8, 64, 2, 128, 4
M_R = M_HEADS // M_GROUPS
M_GW = M_R * M_P
M_BC = M_GROUPS * M_N
M_X_W = M_HEADS * M_P
M_CONV_W = M_X_W + 2 * M_BC
CONV_HIST = 8
DT_PAD = LANES

_SIZES = (256, 256, 512, 512, 512, 128, 128, 512, 512, 512, 1024, 512, M_HEADS, N_BRANCHES * D_MODEL)
_OFFS = np.concatenate([[0], np.cumsum(_SIZES)]).tolist()
RET_COLS = (_OFFS[0], _OFFS[4])
SWA_COLS = (_OFFS[4], _OFFS[8])
POOL_COLS = (_OFFS[8], _OFFS[10])
SSD_COLS = (_OFFS[10], _OFFS[12])
DT_COLS = (_OFFS[12], _OFFS[13])
MG_COLS = (_OFFS[13], _OFFS[14])
MIXER_WIDTHS = tuple(b - a for a, b in (RET_COLS, SWA_COLS, POOL_COLS, SSD_COLS))

ROW_TILE = 512
MERGE_TILE = 1024
N_CHUNKS = ROW_TILE // CHUNK
V7X_VMEM_LIMIT = 56 * 1024 * 1024

_F32 = jnp.float32
_BF16 = jnp.bfloat16


def _sigmoid(x):
    return 0.5 * jnp.tanh(0.5 * x) + 0.5


def _silu(x):
    half = 0.5 * x
    return half + half * jnp.tanh(half)


def _softplus(x):
    return jnp.maximum(x, 0.0) + jnp.log(1.0 + jnp.exp(-jnp.abs(x)))


def _dot(a, b):
    return jnp.dot(a, b, preferred_element_type=_F32)


def _dot_nt(a, b):
    return lax.dot_general(a, b, (((1,), (1,)), ((), ())), preferred_element_type=_F32)


def _dot_tn(a, b):
    return lax.dot_general(a, b, (((0,), (0,)), ((), ())), preferred_element_type=_F32)


def _const_spec(shape):
    nd = len(shape)
    return pl.BlockSpec(shape, lambda *_: (0,) * nd, pipeline_mode=pl.Buffered(1))


def _weight_window_spec(layer, cols):
    return pl.BlockSpec((pl.Element(1), pl.Element(D_MODEL), pl.Element(cols[1] - cols[0])),
                        lambda *_: (layer, 0, cols[0]), pipeline_mode=pl.Buffered(1))


def _row_spec(width, tile=None):
    return pl.BlockSpec((1, tile or ROW_TILE, width), lambda b, t: (b, t, 0))


def _mod_spec():
    return pl.BlockSpec((1, 1, 3 * D_MODEL), lambda b, t: (b, 0, 0))


def _params():
    return pltpu.CompilerParams(dimension_semantics=("parallel", "arbitrary"),
                                vmem_limit_bytes=V7X_VMEM_LIMIT)


def _modulated_norm(x, mod_ref, nw_ref):
    shift = mod_ref[0, :, 0:D_MODEL]
    scale = mod_ref[0, :, D_MODEL:2 * D_MODEL]
    rs = lax.rsqrt(jnp.mean(x * x, axis=-1, keepdims=True) + EPS)
    return (x * rs) * nw_ref[...] * (1.0 + scale) + shift


def _mod_kernel(c_ref, w_ref, b_ref, o_ref):
    sc = _silu(c_ref[...]).astype(_BF16)
    o_ref[0] = _dot(sc, w_ref[0].astype(_BF16)) + b_ref[0]


def _modulation(c, ada_w, ada_b):
    depth, _, n = ada_w.shape
    bn = D_MODEL
    batch = c.shape[0]
    return pl.pallas_call(
        _mod_kernel,
        grid=(depth, n // bn),
        in_specs=[pl.BlockSpec((batch, D_MODEL), lambda l, j: (0, 0)),
                  pl.BlockSpec((1, D_MODEL, bn), lambda l, j: (l, 0, j)),
                  pl.BlockSpec((1, 1, bn), lambda l, j: (l, 0, j))],
        out_specs=pl.BlockSpec((1, batch, bn), lambda l, j: (l, 0, j)),
        out_shape=jax.ShapeDtypeStruct((depth, batch, n), _F32),
        name="adaln_modulation",
    )(c, ada_w, ada_b.reshape(depth, 1, n))


def _retention_tables():
    log_g = np.log(1.0 - 2.0 ** (-5.0 - np.arange(RET_HEADS, dtype=np.float64)))
    pos = np.arange(CHUNK, dtype=np.float64)
    diff = pos[:, None] - pos[None, :]
    inner = np.where(diff >= 0, np.exp(log_g[:, None, None] * np.where(diff >= 0, diff, 0.0)), 0.0)
    q_decay = np.repeat(np.exp(log_g[:, None] * (pos + 1.0)).T, RET_DV, axis=1)
    k_decay = np.repeat(np.exp(log_g[:, None] * (CHUNK - 1.0 - pos)).T, RET_DK, axis=1)
    chunk_decay = np.repeat(np.exp(log_g * CHUNK), RET_DV)[None, :]
    block_diag = (np.arange(RET_QK_W)[:, None] // RET_DK == np.arange(RET_V_W)[None, :] // RET_DV)
    return tuple(jnp.asarray(t, _F32) for t in (inner, q_decay, k_decay, chunk_decay, block_diag))


def _retention_tile(p, o, inner_ref, qd_ref, kd_ref, cd_ref, bd_ref, state_ref):
    head_of_lane = lax.broadcasted_iota(jnp.int32, (CHUNK, RET_QK_W), 1) // RET_DK
    for ci in range(N_CHUNKS):
        rows = slice(ci * CHUNK, (ci + 1) * CHUNK)
        q = p[rows, 0:RET_QK_W]
        k = p[rows, RET_QK_W:2 * RET_QK_W].astype(_F32) * (RET_DK ** -0.5)
        k_bf = k.astype(_BF16)
        v = p[rows, 2 * RET_QK_W:2 * RET_QK_W + RET_V_W]
        state = state_ref[...]
        cross = _dot(q, state.astype(_BF16)) * qd_ref[...]
        update = _dot_tn((k * kd_ref[...]).astype(_BF16), v)
        state_ref[...] = state * cd_ref[...] + update * bd_ref[...]
        for hd in range(RET_HEADS):
            lanes = slice(hd * RET_DV, (hd + 1) * RET_DV)
            q_h = jnp.where(head_of_lane == hd, q, jnp.zeros_like(q))
            scores = _dot_nt(q_h, k_bf) * inner_ref[hd]
            out = _dot(scores.astype(_BF16), v[:, lanes]) + cross[:, lanes]
            out = out * lax.rsqrt(jnp.mean(out * out, axis=-1, keepdims=True) + EPS)
            g0 = 2 * RET_QK_W + RET_V_W + hd * RET_DV
            gate = p[rows, g0:g0 + RET_DV].astype(_F32)
            o[rows, lanes] = (out * _silu(gate)).astype(_BF16)


def _swa_bias():
    slopes = 2.0 ** (-8.0 * np.arange(1, SWA_HQ + 1, dtype=np.float64) / SWA_HQ)
    qi = np.arange(CHUNK)
    kj = np.arange(2 * CHUNK)
    delta = CHUNK + qi[:, None] - kj[None, :]
    valid = (delta >= 0) & (delta < CHUNK)
    variants = []
    for ok in (valid, valid & (kj[None, :] >= CHUNK)):
        bias = np.where(ok[None], -slopes[:, None, None] * delta[None].astype(np.float64), -np.inf)
        variants.append(bias.reshape(SWA_HKV, SWA_GROUP * CHUNK, 2 * CHUNK))
    return jnp.asarray(np.stack(variants) * LOG2_E, _F32)


def _swa_sink_rows(sinks):
    rows = jnp.repeat(sinks.astype(_F32) * LOG2_E, CHUNK)[:, None]
    return jnp.broadcast_to(rows, (SWA_HQ * CHUNK, LANES)).reshape(SWA_HKV, SWA_GROUP * CHUNK, LANES)


def _swa_tile(first_tile, p, o, bias_ref, sink_ref, kt_buf, v_buf):
    low_half = lax.broadcasted_iota(jnp.int32, (ROW_TILE, SWA_KV_W), 1) < SWA_D
    k = p[:, SWA_Q_W:SWA_Q_W + SWA_KV_W].astype(_F32)
    k_swapped = pltpu.roll(k, SWA_D, axis=1)
    for j, dup in enumerate((jnp.where(low_half, k, k_swapped), jnp.where(low_half, k_swapped, k))):
        for bi in range(N_CHUNKS):
            kt_buf[j, :, (bi + 1) * CHUNK:(bi + 2) * CHUNK] = dup[bi * CHUNK:(bi + 1) * CHUNK, :].T.astype(_BF16)
    v = p[:, SWA_Q_W + SWA_KV_W:SWA_Q_W + 2 * SWA_KV_W].astype(_F32)
    v_swapped = pltpu.roll(v, SWA_D, axis=1)
    for j, dup in enumerate((jnp.where(low_half, v, v_swapped), jnp.where(low_half, v_swapped, v))):
        v_buf[CHUNK:CHUNK + ROW_TILE, j * SWA_EXT_W:j * SWA_EXT_W + LANES] = dup.astype(_BF16)
        v_buf[CHUNK:CHUNK + ROW_TILE, j * SWA_EXT_W + LANES:(j + 1) * SWA_EXT_W] = jnp.ones((ROW_TILE, LANES), _BF16)

    low_lanes = lax.broadcasted_iota(jnp.int32, (CHUNK, LANES), 1) < SWA_D
    high_lanes = jnp.logical_not(low_lanes)
    first_variant = first_tile.astype(jnp.int32)
    gate0 = SWA_Q_W + 2 * SWA_KV_W

    for bi in range(N_CHUNKS):
        rows = slice(bi * CHUNK, (bi + 1) * CHUNK)
        keys = slice(bi * CHUNK, (bi + 2) * CHUNK)
        variant = first_variant if bi == 0 else 0
        for j in range(SWA_HKV):
            k_t = kt_buf[j, :, keys]
            v_ext = v_buf[keys, j * SWA_EXT_W:(j + 1) * SWA_EXT_W]
            for pair in range(SWA_GROUP // 2):
                c0 = (j * SWA_GROUP // 2 + pair) * LANES
                qp = p[rows, c0:c0 + LANES].astype(_F32) * (SWA_D ** -0.5 * LOG2_E)
                halves = []
                for odd in range(2):
                    g_rows = slice((2 * pair + odd) * CHUNK, (2 * pair + odd + 1) * CHUNK)
                    q_h = jnp.where(high_lanes if odd else low_lanes, qp, 0.0).astype(_BF16)
                    s = _dot(q_h, k_t) + bias_ref[variant, j, g_rows, :]
                    s_prev, s_cur = s[:, 0:CHUNK], s[:, CHUNK:2 * CHUNK]
                    sink = sink_ref[j, g_rows, :]
                    m = jnp.maximum(jnp.max(jnp.maximum(s_prev, s_cur), axis=-1, keepdims=True), sink)
                    e = jnp.concatenate([jnp.exp2(s_prev - m), jnp.exp2(s_cur - m)], axis=1).astype(_BF16)
                    ov = _dot(e, v_ext)
                    halves.append(ov[:, 0:LANES] / (ov[:, LANES:2 * LANES] + jnp.exp2(sink - m)))
                gate = p[rows, gate0 + c0:gate0 + c0 + LANES].astype(_F32)
                o[rows, c0:c0 + LANES] = (jnp.where(low_lanes, halves[0], halves[1]) * _silu(gate)).astype(_BF16)

    kt_buf[:, :, 0:CHUNK] = kt_buf[:, :, ROW_TILE:ROW_TILE + CHUNK]
    v_buf[0:CHUNK, :] = v_buf[ROW_TILE:ROW_TILE + CHUNK, :]


def _pool_inverse_counts():
    pos = np.arange(ROW_TILE, dtype=np.float64)[:, None]
    windows = np.repeat(np.asarray(POOL_WINDOWS, np.float64), POOL_GD)[None, :]
    steady = np.broadcast_to(1.0 / windows, (ROW_TILE, BRANCH))
    start = 1.0 / np.minimum(pos + 1.0, windows)
    return jnp.asarray(np.stack([steady, start]), _F32)


def _pool_tile(first_tile, p, o, inv_ref, w_ref, scale_ref, hist, levels):
    hist[CHUNK:CHUNK + ROW_TILE, :] = p[:, 0:BRANCH].astype(_F32)
    inv = inv_ref[first_tile.astype(jnp.int32)]
    ext_rows = POOL_EXT + ROW_TILE
    base = CHUNK - POOL_EXT
    sums = []
    for g, w in enumerate(POOL_WINDOWS):
        c0 = g * POOL_GD
        shift = w // 2
        if g == 0:
            total = hist[base:base + ext_rows, :] + hist[base - shift:base - shift + ext_rows, :]
        else:
            prev = levels[g - 1]
            total = (prev[POOL_PAD:POOL_PAD + ext_rows, c0:]
                     + prev[POOL_PAD - shift:POOL_PAD - shift + ext_rows, c0:])
        if g + 1 < len(POOL_WINDOWS):
            levels[g][POOL_PAD:POOL_PAD + ext_rows, c0:] = total
        sums.append(total)
    for g in range(len(POOL_WINDOWS)):
        lanes = slice(g * POOL_GD, (g + 1) * POOL_GD)
        u = hist[CHUNK:CHUNK + ROW_TILE, lanes]
        acc = sums[g][POOL_EXT:, 0:POOL_GD]
        diff = acc * inv[:, lanes] - u
        y = _dot(diff.astype(_BF16), w_ref[g])
        gate = p[:, BRANCH + g * POOL_GD:BRANCH + (g + 1) * POOL_GD].astype(_F32)
        o[:, lanes] = (y * scale_ref[:, lanes] * _silu(gate)).astype(_BF16)
    hist[0:CHUNK, :] = hist[ROW_TILE:ROW_TILE + CHUNK, :]


def _expansion_matrix():
    e = np.zeros((2 * LANES, M_X_W), np.float32)
    for hd in range(M_HEADS):
        e[hd, hd * M_P:(hd + 1) * M_P] = 1.0
        e[LANES + hd, hd * M_P:(hd + 1) * M_P] = 1.0
    return jnp.asarray(e, _BF16)


def _pad_lanes(v, width=DT_PAD):
    return jnp.pad(v.astype(_F32), (0, width - v.shape[0])).reshape(1, width)


def _ssd_tile(p, dt_raw, o, cw_ref, cb_ref, dtb_ref, alog_ref, dskip_ref, nw_ref, tri_ref, exp_ref,
              hist, xbc_buf, state_ref):
    hist[CONV_HIST:CONV_HIST + ROW_TILE, :] = p[:, 0:M_CONV_W].astype(_F32)
    acc = cb_ref[...] + cw_ref[M_CONV - 1:M_CONV, :] * hist[CONV_HIST:CONV_HIST + ROW_TILE, :]
    for j in range(1, M_CONV):
        acc = acc + cw_ref[M_CONV - 1 - j:M_CONV - j, :] * hist[CONV_HIST - j:CONV_HIST - j + ROW_TILE, :]
    xbc_buf[...] = _silu(acc)
    hist[0:CONV_HIST, :] = hist[ROW_TILE:ROW_TILE + CONV_HIST, :]

    a_row = -jnp.exp(alog_ref[...])
    causal = (lax.broadcasted_iota(jnp.int32, (CHUNK, CHUNK), 0)
              >= lax.broadcasted_iota(jnp.int32, (CHUNK, CHUNK), 1))
    head_of_lane = lax.broadcasted_iota(jnp.int32, (CHUNK, M_GW), 1) // M_P

    def expand(vals):
        stacked = jnp.concatenate(vals, axis=0)
        hi = stacked.astype(_BF16)
        lo = (stacked - hi.astype(_F32)).astype(_BF16)
        wide = _dot(jnp.concatenate([hi, lo], axis=1), exp_ref[...])
        return [wide[i * CHUNK:(i + 1) * CHUNK, :] for i in range(len(vals))]

    for ci in range(N_CHUNKS):
        rows = slice(ci * CHUNK, (ci + 1) * CHUNK)
        dt = _softplus(dt_raw[rows, :] + dtb_ref[...])
        acs = jnp.dot(tri_ref[...], dt * a_row, preferred_element_type=_F32, precision=lax.Precision.HIGHEST)
        acs_t = acs.T
        last = acs[CHUNK - 1:CHUNK, :]
        dt_x, w_x, ex_x = expand([dt, dt * jnp.exp(last - acs), jnp.exp(acs)])
        x = xbc_buf[rows, 0:M_X_W]
        xdt = (x * dt_x).astype(_BF16)
        xd = (x * w_x).astype(_BF16)
        for g in range(M_GROUPS):
            lanes = slice(g * M_GW, (g + 1) * M_GW)
            bmat = xbc_buf[rows, M_X_W + g * M_N:M_X_W + (g + 1) * M_N].astype(_BF16)
            cmat = xbc_buf[rows, M_X_W + M_BC + g * M_N:M_X_W + M_BC + (g + 1) * M_N].astype(_BF16)
            cb = _dot_nt(cmat, bmat)
            state = state_ref[g]
            ex_g = ex_x[:, lanes]
            y = _dot(cmat, state.astype(_BF16)) * ex_g + x[:, lanes] * dskip_ref[:, lanes]
            xdt_g = xdt[:, lanes]
            for r in range(M_R):
                hd = g * M_R + r
                seg = acs[:, hd:hd + 1] - acs_t[hd:hd + 1, :]
                lmat = jnp.exp(jnp.where(causal, seg, -jnp.inf))
                x_h = jnp.where(head_of_lane == r, xdt_g, jnp.zeros_like(xdt_g))
                y = y + _dot((cb * lmat).astype(_BF16), x_h)
            state_ref[g] = state * ex_g[CHUNK - 1:CHUNK, :] + _dot_tn(bmat, xd[:, lanes])
            z = p[rows, M_CONV_W + g * M_GW:M_CONV_W + (g + 1) * M_GW].astype(_F32)
            yz = y * _silu(z)
            yz = yz * lax.rsqrt(jnp.mean(yz * yz, axis=-1, keepdims=True) + EPS)
            o[rows, lanes] = (yz * nw_ref[:, lanes]).astype(_BF16)


N_RET_T, N_SWA_T, N_POOL_T, N_SSD_T = 5, 2, 3, 8


def _mixers_kernel(x_ref, mod_ref, nw_ref, wr_ref, ws_ref, wp_ref, wm_ref, wdt_ref, *rest):
    consts, rest = rest[:N_RET_T + N_SWA_T + N_POOL_T + N_SSD_T], rest[N_RET_T + N_SWA_T + N_POOL_T + N_SSD_T:]
    ret_c, consts = consts[:N_RET_T], consts[N_RET_T:]
    swa_c, consts = consts[:N_SWA_T], consts[N_SWA_T:]
    pool_c, ssd_c = consts[:N_POOL_T], consts[N_POOL_T:]
    ret_o, att_o, pool_o, ssm_o, h_o = (r.at[0] for r in rest[:N_BRANCHES + 1])
    proj = rest[N_BRANCHES + 1:N_BRANCHES + 6]
    ret_state, kt_buf, v_buf, pool_hist, conv_hist, xbc_buf, ssd_state = rest[N_BRANCHES + 6:N_BRANCHES + 13]
    pool_levels = rest[N_BRANCHES + 13:]
    first_tile = pl.program_id(1) == 0

    @pl.when(first_tile)
    def _():
        ret_state[...] = jnp.zeros_like(ret_state)
        ssd_state[...] = jnp.zeros_like(ssd_state)
        kt_buf[:, :, 0:CHUNK] = jnp.zeros((SWA_HKV, LANES, CHUNK), _BF16)
        v_buf[0:CHUNK, :] = jnp.zeros((CHUNK, SWA_HKV * SWA_EXT_W), _BF16)
        pool_hist[0:CHUNK, :] = jnp.zeros((CHUNK, BRANCH), _F32)
        for level in pool_levels:
            level[0:POOL_PAD, :] = jnp.zeros((POOL_PAD, BRANCH), _F32)
        conv_hist[0:CONV_HIST, :] = jnp.zeros((CONV_HIST, M_CONV_W), _F32)

    h = _modulated_norm(x_ref[0], mod_ref, nw_ref).astype(_BF16)
    h_o[...] = h
    ret_p, swa_p, pool_p, ssd_p, dt_p = proj
    for ref, w_ref in ((ssd_p, wm_ref.at[0]), (dt_p, wdt_ref), (swa_p, ws_ref.at[0]), (ret_p, wr_ref.at[0]),
                       (pool_p, wp_ref.at[0])):
        ref[...] = _dot(h, w_ref[...]).astype(ref.dtype)

    _ssd_tile(ssd_p, dt_p, ssm_o, *ssd_c, conv_hist, xbc_buf, ssd_state)
    _swa_tile(first_tile, swa_p, att_o, *swa_c, kt_buf, v_buf)
    _retention_tile(ret_p, ret_o, *ret_c, ret_state)
    _pool_tile(first_tile, pool_p, pool_o, *pool_c, pool_hist, pool_levels)


def _mixers(layer, x, mod, norm_w, w_in, w_dt, swa_sinks, pool_w, pool_scale, conv_w, conv_b, dt_bias, a_log,
            d_skip, ssm_norm_w):
    batch, seq, _ = x.shape
    windows = (RET_COLS, SWA_COLS, POOL_COLS, SSD_COLS)
    consts = list(_retention_tables())
    consts += [_swa_bias(), _swa_sink_rows(swa_sinks)]
    consts += [_pool_inverse_counts(), pool_w.astype(_BF16), pool_scale.reshape(1, BRANCH)]
    consts += [conv_w, conv_b.reshape(1, M_CONV_W), _pad_lanes(dt_bias), _pad_lanes(a_log),
               jnp.repeat(d_skip.astype(_F32), M_P).reshape(1, BRANCH), ssm_norm_w.reshape(1, BRANCH),
               jnp.asarray(np.tril(np.ones((CHUNK, CHUNK), np.float32))), _expansion_matrix()]
    assert len(consts) == N_RET_T + N_SWA_T + N_POOL_T + N_SSD_T
    operands = [x, mod, norm_w] + [w_in] * len(windows) + [w_dt, *consts]
    in_specs = [_row_spec(D_MODEL), _mod_spec(), _const_spec(norm_w.shape)]
    in_specs += [_weight_window_spec(layer, cols) for cols in windows]
    in_specs += [_const_spec(a.shape) for a in [w_dt, *consts]]
    scratch = [pltpu.VMEM((ROW_TILE, w), _BF16) for w in MIXER_WIDTHS] + [pltpu.VMEM((ROW_TILE, DT_PAD), _F32)]
    scratch += [pltpu.VMEM((RET_QK_W, RET_V_W), _F32),
                pltpu.VMEM((SWA_HKV, LANES, CHUNK + ROW_TILE), _BF16),
                pltpu.VMEM((CHUNK + ROW_TILE, SWA_HKV * SWA_EXT_W), _BF16),
                pltpu.VMEM((CHUNK + ROW_TILE, BRANCH), _F32),
                pltpu.VMEM((CONV_HIST + ROW_TILE, M_CONV_W), _F32),
                pltpu.VMEM((ROW_TILE, M_CONV_W), _F32),
                pltpu.VMEM((M_GROUPS, M_N, M_GW), _F32)]
    scratch += [pltpu.VMEM((POOL_PAD + POOL_EXT + ROW_TILE, BRANCH), _F32)] * (len(POOL_WINDOWS) - 1)
    return pl.pallas_call(
        _mixers_kernel,
        grid=(batch, seq // ROW_TILE),
        in_specs=in_specs,
        out_specs=[_row_spec(BRANCH)] * N_BRANCHES + [_row_spec(D_MODEL)],
        out_shape=[jax.ShapeDtypeStruct((batch, seq, BRANCH), _BF16)] * N_BRANCHES
        + [jax.ShapeDtypeStruct((batch, seq, D_MODEL), _BF16)],
        scratch_shapes=scratch,
        compiler_params=_params(),
        name="norm_proj_mixers",
    )(*operands)


def _merge_kernel(final, x_ref, mod_ref, h_ref, wg_ref, ret_ref, att_ref, pool_ref, ssm_ref, wup_ref, wout_ref,
                  *rest):
    if final:
        fnw_ref, o_ref = rest
    else:
        (o_ref,) = rest
    h = h_ref[0]
    merged = None
    for i, br_ref in enumerate((ret_ref, att_ref, pool_ref, ssm_ref)):
        gate = _sigmoid(_dot(h, wg_ref[:, i * D_MODEL:(i + 1) * D_MODEL]))
        term = gate * _dot(br_ref[0], wup_ref[i])
        merged = term if merged is None else merged + term
    out = _dot(merged.astype(_BF16), wout_ref[...])
    y = x_ref[0] + mod_ref[0, :, 2 * D_MODEL:3 * D_MODEL] * out
    if final:
        y = y * lax.rsqrt(jnp.mean(y * y, axis=-1, keepdims=True) + EPS) * fnw_ref[...]
    o_ref[0] = y


def _merge(x, mod, h, wg, branches, w_up, w_out, final_norm_w):
    batch, seq, _ = x.shape
    final = final_norm_w is not None
    in_specs = [_row_spec(D_MODEL, MERGE_TILE), _mod_spec(), _row_spec(D_MODEL, MERGE_TILE), _const_spec(wg.shape)]
    in_specs += [_row_spec(BRANCH, MERGE_TILE)] * N_BRANCHES
    in_specs += [_const_spec(w_up.shape), _const_spec(w_out.shape)]
    args = [x, mod, h, wg, *branches, w_up, w_out]
    if final:
        in_specs.append(_const_spec((1, D_MODEL)))
        args.append(final_norm_w.reshape(1, D_MODEL))
    return pl.pallas_call(
        functools.partial(_merge_kernel, final),
        grid=(batch, seq // MERGE_TILE),
        in_specs=in_specs,
        out_specs=_row_spec(D_MODEL, MERGE_TILE),
        out_shape=jax.ShapeDtypeStruct(x.shape, _F32),
        compiler_params=_params(),
        name="merge_out_proj",
    )(*args)


def kernel(x, c, ada_w, ada_b, norm_w, w_in, swa_sinks, pool_w, pool_scale, conv_w, conv_b, dt_bias, a_log,
           d_skip, ssm_norm_w, w_up, w_out, final_norm_w):
    depth = ada_w.shape[0]
    batch = x.shape[0]
    mod_all = _modulation(c, ada_w, ada_b)
    w_in_bf = w_in.astype(_BF16)
    for l in range(depth):
        mod = mod_all[l].reshape(batch, 1, 3 * D_MODEL)
        nw = norm_w[l].reshape(1, D_MODEL)
        w_dt = jnp.pad(w_in_bf[l, :, DT_COLS[0]:DT_COLS[1]], ((0, 0), (0, DT_PAD - M_HEADS)))
        *branches, h = _mixers(l, x, mod, nw, w_in_bf, w_dt, swa_sinks[l], pool_w[l], pool_scale[l], conv_w[l],
                               conv_b[l], dt_bias[l], a_log[l], d_skip[l], ssm_norm_w[l])
        wg = w_in_bf[l, :, MG_COLS[0]:MG_COLS[1]]
        x = _merge(x, mod, h, wg, branches, w_up[l].astype(_BF16), w_out[l].astype(_BF16),
                   final_norm_w if l == depth - 1 else None)
    return x
```

```python
import functools

import jax
import jax.numpy as jnp
import numpy as np
from jax import lax
from jax.experimental import pallas as pl
from jax.experimental.pallas import tpu as pltpu

D_MODEL = 1024
BRANCH = 512
N_BRANCHES = 4
EPS = 1e-6
CHUNK = 128
LANES = 128

RET_HEADS, RET_DK, RET_DV = 4, 64, 128
RET_QK_W = RET_HEADS * RET_DK
RET_V_W = RET_HEADS * RET_DV
SWA_HQ, SWA_HKV, SWA_D = 8, 2, 64
SWA_GROUP = SWA_HQ // SWA_HKV
SWA_Q_W = SWA_HQ * SWA_D
SWA_KV_W = SWA_HKV * SWA_D
SWA_EXT_W = 2 * LANES
LOG2_E = 1.4426950408889634
POOL_WINDOWS = (2, 4, 8, 16)
POOL_GD = BRANCH // len(POOL_WINDOWS)
POOL_EXT = 16
POOL_PAD = 8
M_HEADS, M_P, M_GROUPS, M_N, M_CONV = 8, 64, 2, 128, 4
M_R = M_HEADS // M_GROUPS
M_GW = M_R * M_P
M_BC = M_GROUPS * M_N
M_X_W = M_HEADS * M_P
M_CONV_W = M_X_W + 2 * M_BC
CONV_HIST = 8
DT_PAD = LANES

_SIZES = (256, 256, 512, 512, 512, 128, 128, 512, 512, 512, 1024, 512, M_HEADS, N_BRANCHES * D_MODEL)
_OFFS = np.concatenate([[0], np.cumsum(_SIZES)]).tolist()
RET_COLS = (_OFFS[0], _OFFS[4])
SWA_COLS = (_OFFS[4], _OFFS[8])
POOL_COLS = (_OFFS[8], _OFFS[10])
SSD_COLS = (_OFFS[10], _OFFS[12])
DT_COLS = (_OFFS[12], _OFFS[13])
MG_COLS = (_OFFS[13], _OFFS[14])
MIXER_WIDTHS = tuple(b - a for a, b in (RET_COLS, SWA_COLS, POOL_COLS, SSD_COLS))
PACKED_DT_COLS = (DT_COLS[0], DT_COLS[0] + DT_PAD)
PACKED_MG_COLS = (PACKED_DT_COLS[1], PACKED_DT_COLS[1] + MG_COLS[1] - MG_COLS[0])

ROW_TILE = 512
MERGE_TILE = 1024
N_CHUNKS = ROW_TILE // CHUNK
V7X_VMEM_LIMIT = 56 * 1024 * 1024

_F32 = jnp.float32
_BF16 = jnp.bfloat16


def _sigmoid(x):
    return 0.5 * jnp.tanh(0.5 * x) + 0.5


def _silu(x):
    half = 0.5 * x
    return half + half * jnp.tanh(half)


def _softplus(x):
    return jnp.maximum(x, 0.0) + jnp.log(1.0 + jnp.exp(-jnp.abs(x)))


def _dot(a, b):
    return jnp.dot(a, b, preferred_element_type=_F32)


def _dot_nt(a, b):
    return lax.dot_general(a, b, (((1,), (1,)), ((), ())), preferred_element_type=_F32)


def _dot_tn(a, b):
    return lax.dot_general(a, b, (((0,), (0,)), ((), ())), preferred_element_type=_F32)


def _const_spec(shape):
    nd = len(shape)
    return pl.BlockSpec(shape, lambda *_: (0,) * nd, pipeline_mode=pl.Buffered(1))


def _weight_window_spec(layer, cols):
    return pl.BlockSpec((pl.Element(1), pl.Element(D_MODEL), pl.Element(cols[1] - cols[0])),
                        lambda *_: (layer, 0, cols[0]), pipeline_mode=pl.Buffered(1))


def _row_spec(width, tile=None):
    return pl.BlockSpec((1, tile or ROW_TILE, width), lambda b, t: (b, t, 0))


def _mod_spec():
    return pl.BlockSpec((1, 1, 3 * D_MODEL), lambda b, t: (b, 0, 0))


def _params():
    return pltpu.CompilerParams(dimension_semantics=("parallel", "arbitrary"),
                                vmem_limit_bytes=V7X_VMEM_LIMIT)


def _modulated_norm(x, mod_ref, nw_ref):
    shift = mod_ref[0, :, 0:D_MODEL]
    scale = mod_ref[0, :, D_MODEL:2 * D_MODEL]
    rs = lax.rsqrt(jnp.mean(x * x, axis=-1, keepdims=True) + EPS)
    return (x * rs) * nw_ref[...] * (1.0 + scale) + shift


def _mod_kernel(c_ref, w_ref, b_ref, o_ref):
    sc = _silu(c_ref[...]).astype(_BF16)
    o_ref[0] = _dot(sc, w_ref[0].astype(_BF16)) + b_ref[0]


def _modulation(c, ada_w, ada_b):
    depth, _, n = ada_w.shape
    bn = D_MODEL
    batch = c.shape[0]
    return pl.pallas_call(
        _mod_kernel,
        grid=(depth, n // bn),
        in_specs=[pl.BlockSpec((batch, D_MODEL), lambda l, j: (0, 0)),
                  pl.BlockSpec((1, D_MODEL, bn), lambda l, j: (l, 0, j)),
                  pl.BlockSpec((1, 1, bn), lambda l, j: (l, 0, j))],
        out_specs=pl.BlockSpec((1, batch, bn), lambda l, j: (l, 0, j)),
        out_shape=jax.ShapeDtypeStruct((depth, batch, n), _F32),
        name="adaln_modulation",
    )(c, ada_w, ada_b.reshape(depth, 1, n))


def _retention_tables():
    log_g = np.log(1.0 - 2.0 ** (-5.0 - np.arange(RET_HEADS, dtype=np.float64)))
    pos = np.arange(CHUNK, dtype=np.float64)
    diff = pos[:, None] - pos[None, :]
    inner = np.where(diff >= 0, np.exp(log_g[:, None, None] * np.where(diff >= 0, diff, 0.0)), 0.0)
    q_decay = np.repeat(np.exp(log_g[:, None] * (pos + 1.0)).T, RET_DV, axis=1)
    k_decay = np.repeat(np.exp(log_g[:, None] * (CHUNK - 1.0 - pos)).T, RET_DK, axis=1)
    chunk_decay = np.repeat(np.exp(log_g * CHUNK), RET_DV)[None, :]
    block_diag = (np.arange(RET_QK_W)[:, None] // RET_DK == np.arange(RET_V_W)[None, :] // RET_DV)
    return tuple(jnp.asarray(t, _F32) for t in (inner, q_decay, k_decay, chunk_decay, block_diag))


def _retention_tile(p, o, inner_ref, qd_ref, kd_ref, cd_ref, bd_ref, state_ref):
    head_of_lane = lax.broadcasted_iota(jnp.int32, (CHUNK, RET_QK_W), 1) // RET_DK
    for ci in range(N_CHUNKS):
        rows = slice(ci * CHUNK, (ci + 1) * CHUNK)
        q = p[rows, 0:RET_QK_W]
        k = p[rows, RET_QK_W:2 * RET_QK_W].astype(_F32) * (RET_DK ** -0.5)
        k_bf = k.astype(_BF16)
        v = p[rows, 2 * RET_QK_W:2 * RET_QK_W + RET_V_W]
        state = state_ref[...]
        cross = _dot(q, state.astype(_BF16)) * qd_ref[...]
        update = _dot_tn((k * kd_ref[...]).astype(_BF16), v)
        state_ref[...] = state * cd_ref[...] + update * bd_ref[...]
        for hd in range(RET_HEADS):
            lanes = slice(hd * RET_DV, (hd + 1) * RET_DV)
            q_h = jnp.where(head_of_lane == hd, q, jnp.zeros_like(q))
            scores = _dot_nt(q_h, k_bf) * inner_ref[hd]
            out = _dot(scores.astype(_BF16), v[:, lanes]) + cross[:, lanes]
            out = out * lax.rsqrt(jnp.mean(out * out, axis=-1, keepdims=True) + EPS)
            g0 = 2 * RET_QK_W + RET_V_W + hd * RET_DV
            gate = p[rows, g0:g0 + RET_DV].astype(_F32)
            o[rows, lanes] = (out * _silu(gate)).astype(_BF16)


def _swa_bias():
    slopes = 2.0 ** (-8.0 * np.arange(1, SWA_HQ + 1, dtype=np.float64) / SWA_HQ)
    qi = np.arange(CHUNK)
    kj = np.arange(2 * CHUNK)
    delta = CHUNK + qi[:, None] - kj[None, :]
    valid = (delta >= 0) & (delta < CHUNK)
    variants = []
    for ok in (valid, valid & (kj[None, :] >= CHUNK)):
        bias = np.where(ok[None], -slopes[:, None, None] * delta[None].astype(np.float64), -np.inf)
        variants.append(bias.reshape(SWA_HKV, SWA_GROUP * CHUNK, 2 * CHUNK))
    return jnp.asarray(np.stack(variants) * LOG2_E, _F32)


def _swa_sink_rows(sinks):
    rows = jnp.repeat(sinks.astype(_F32) * LOG2_E, CHUNK)[:, None]
    return jnp.broadcast_to(rows, (SWA_HQ * CHUNK, LANES)).reshape(SWA_HKV, SWA_GROUP * CHUNK, LANES)


def _swa_tile(first_tile, p, o, bias_ref, sink_ref, kt_buf, v_buf):
    low_half = lax.broadcasted_iota(jnp.int32, (ROW_TILE, SWA_KV_W), 1) < SWA_D
    k = p[:, SWA_Q_W:SWA_Q_W + SWA_KV_W].astype(_F32)
    k_swapped = pltpu.roll(k, SWA_D, axis=1)
    for j, dup in enumerate((jnp.where(low_half, k, k_swapped), jnp.where(low_half, k_swapped, k))):
        for bi in range(N_CHUNKS):
            kt_buf[j, :, (bi + 1) * CHUNK:(bi + 2) * CHUNK] = dup[bi * CHUNK:(bi + 1) * CHUNK, :].T.astype(_BF16)
    v = p[:, SWA_Q_W + SWA_KV_W:SWA_Q_W + 2 * SWA_KV_W].astype(_F32)
    v_swapped = pltpu.roll(v, SWA_D, axis=1)
    for j, dup in enumerate((jnp.where(low_half, v, v_swapped), jnp.where(low_half, v_swapped, v))):
        v_buf[CHUNK:CHUNK + ROW_TILE, j * SWA_EXT_W:j * SWA_EXT_W + LANES] = dup.astype(_BF16)
        v_buf[CHUNK:CHUNK + ROW_TILE, j * SWA_EXT_W + LANES:(j + 1) * SWA_EXT_W] = jnp.ones((ROW_TILE, LANES), _BF16)

    low_lanes = lax.broadcasted_iota(jnp.int32, (CHUNK, LANES), 1) < SWA_D
    high_lanes = jnp.logical_not(low_lanes)
    first_variant = first_tile.astype(jnp.int32)
    gate0 = SWA_Q_W + 2 * SWA_KV_W

    for bi in range(N_CHUNKS):
        rows = slice(bi * CHUNK, (bi + 1) * CHUNK)
        keys = slice(bi * CHUNK, (bi + 2) * CHUNK)
        variant = first_variant if bi == 0 else 0
        for j in range(SWA_HKV):
            k_t = kt_buf[j, :, keys]
            v_ext = v_buf[keys, j * SWA_EXT_W:(j + 1) * SWA_EXT_W]
            for pair in range(SWA_GROUP // 2):
                c0 = (j * SWA_GROUP // 2 + pair) * LANES
                qp = p[rows, c0:c0 + LANES].astype(_F32) * (SWA_D ** -0.5 * LOG2_E)
                halves = []
                for odd in range(2):
                    g_rows = slice((2 * pair + odd) * CHUNK, (2 * pair + odd + 1) * CHUNK)
                    q_h = jnp.where(high_lanes if odd else low_lanes, qp, 0.0).astype(_BF16)
                    s = _dot(q_h, k_t) + bias_ref[variant, j, g_rows, :]
                    s_prev, s_cur = s[:, 0:CHUNK], s[:, CHUNK:2 * CHUNK]
                    sink = sink_ref[j, g_rows, :]
                    m = jnp.maximum(jnp.max(jnp.maximum(s_prev, s_cur), axis=-1, keepdims=True), sink)
                    e = jnp.concatenate([jnp.exp2(s_prev - m), jnp.exp2(s_cur - m)], axis=1).astype(_BF16)
                    ov = _dot(e, v_ext)
                    halves.append(ov[:, 0:LANES] / (ov[:, LANES:2 * LANES] + jnp.exp2(sink - m)))
                gate = p[rows, gate0 + c0:gate0 + c0 + LANES].astype(_F32)
                o[rows, c0:c0 + LANES] = (jnp.where(low_lanes, halves[0], halves[1]) * _silu(gate)).astype(_BF16)

    kt_buf[:, :, 0:CHUNK] = kt_buf[:, :, ROW_TILE:ROW_TILE + CHUNK]
    v_buf[0:CHUNK, :] = v_buf[ROW_TILE:ROW_TILE + CHUNK, :]


def _pool_inverse_counts():
    pos = np.arange(ROW_TILE, dtype=np.float64)[:, None]
    windows = np.repeat(np.asarray(POOL_WINDOWS, np.float64), POOL_GD)[None, :]
    steady = np.broadcast_to(1.0 / windows, (ROW_TILE, BRANCH))
    start = 1.0 / np.minimum(pos + 1.0, windows)
    return jnp.asarray(np.stack([steady, start]), _F32)


def _pool_tile(first_tile, p, o, inv_ref, w_ref, scale_ref, hist, levels):
    hist[CHUNK:CHUNK + ROW_TILE, :] = p[:, 0:BRANCH].astype(_F32)
    inv = inv_ref[first_tile.astype(jnp.int32)]
    ext_rows = POOL_EXT + ROW_TILE
    base = CHUNK - POOL_EXT
    sums = []
    for g, w in enumerate(POOL_WINDOWS):
        c0 = g * POOL_GD
        shift = w // 2
        if g == 0:
            total = hist[base:base + ext_rows, :] + hist[base - shift:base - shift + ext_rows, :]
        else:
            prev = levels[g - 1]
            total = (prev[POOL_PAD:POOL_PAD + ext_rows, c0:]
                     + prev[POOL_PAD - shift:POOL_PAD - shift + ext_rows, c0:])
        if g + 1 < len(POOL_WINDOWS):
            levels[g][POOL_PAD:POOL_PAD + ext_rows, c0:] = total
        sums.append(total)
    for g in range(len(POOL_WINDOWS)):
        lanes = slice(g * POOL_GD, (g + 1) * POOL_GD)
        u = hist[CHUNK:CHUNK + ROW_TILE, lanes]
        acc = sums[g][POOL_EXT:, 0:POOL_GD]
        diff = acc * inv[:, lanes] - u
        y = _dot(diff.astype(_BF16), w_ref[g])
        gate = p[:, BRANCH + g * POOL_GD:BRANCH + (g + 1) * POOL_GD].astype(_F32)
        o[:, lanes] = (y * scale_ref[:, lanes] * _silu(gate)).astype(_BF16)
    hist[0:CHUNK, :] = hist[ROW_TILE:ROW_TILE + CHUNK, :]


def _expansion_matrix():
    e = np.zeros((2 * LANES, M_X_W), np.float32)
    for hd in range(M_HEADS):
        e[hd, hd * M_P:(hd + 1) * M_P] = 1.0
        e[LANES + hd, hd * M_P:(hd + 1) * M_P] = 1.0
    return jnp.asarray(e, _BF16)


def _pad_lanes(v, width=DT_PAD):
    return jnp.pad(v.astype(_F32), (0, width - v.shape[0])).reshape(1, width)


def _ssd_tile(p, dt_raw, o, cw_ref, cb_ref, dtb_ref, alog_ref, dskip_ref, nw_ref, tri_ref, exp_ref,
              hist, xbc_buf, state_ref):
    hist[CONV_HIST:CONV_HIST + ROW_TILE, :] = p[:, 0:M_CONV_W].astype(_F32)
    acc = cb_ref[...] + cw_ref[M_CONV - 1:M_CONV, :] * hist[CONV_HIST:CONV_HIST + ROW_TILE, :]
    for j in range(1, M_CONV):
        acc = acc + cw_ref[M_CONV - 1 - j:M_CONV - j, :] * hist[CONV_HIST - j:CONV_HIST - j + ROW_TILE, :]
    xbc_buf[...] = _silu(acc)
    hist[0:CONV_HIST, :] = hist[ROW_TILE:ROW_TILE + CONV_HIST, :]

    a_row = -jnp.exp(alog_ref[...])
    causal = (lax.broadcasted_iota(jnp.int32, (CHUNK, CHUNK), 0)
              >= lax.broadcasted_iota(jnp.int32, (CHUNK, CHUNK), 1))
    head_of_lane = lax.broadcasted_iota(jnp.int32, (CHUNK, M_GW), 1) // M_P

    def expand(vals):
        stacked = jnp.concatenate(vals, axis=0)
        hi = stacked.astype(_BF16)
        lo = (stacked - hi.astype(_F32)).astype(_BF16)
        wide = _dot(jnp.concatenate([hi, lo], axis=1), exp_ref[...])
        return [wide[i * CHUNK:(i + 1) * CHUNK, :] for i in range(len(vals))]

    for ci in range(N_CHUNKS):
        rows = slice(ci * CHUNK, (ci + 1) * CHUNK)
        dt = _softplus(dt_raw[rows, :] + dtb_ref[...])
        acs = jnp.dot(tri_ref[...], dt * a_row, preferred_element_type=_F32, precision=lax.Precision.HIGHEST)
        acs_t = acs.T
        last = acs[CHUNK - 1:CHUNK, :]
        dt_x, w_x, ex_x = expand([dt, dt * jnp.exp(last - acs), jnp.exp(acs)])
        x = xbc_buf[rows, 0:M_X_W]
        xdt = (x * dt_x).astype(_BF16)
        xd = (x * w_x).astype(_BF16)
        for g in range(M_GROUPS):
            lanes = slice(g * M_GW, (g + 1) * M_GW)
            bmat = xbc_buf[rows, M_X_W + g * M_N:M_X_W + (g + 1) * M_N].astype(_BF16)
            cmat = xbc_buf[rows, M_X_W + M_BC + g * M_N:M_X_W + M_BC + (g + 1) * M_N].astype(_BF16)
            cb = _dot_nt(cmat, bmat)
            state = state_ref[g]
            ex_g = ex_x[:, lanes]
            y = _dot(cmat, state.astype(_BF16)) * ex_g + x[:, lanes] * dskip_ref[:, lanes]
            xdt_g = xdt[:, lanes]
            for r in range(M_R):
                hd = g * M_R + r
                seg = acs[:, hd:hd + 1] - acs_t[hd:hd + 1, :]
                lmat = jnp.exp(jnp.where(causal, seg, -jnp.inf))
                x_h = jnp.where(head_of_lane == r, xdt_g, jnp.zeros_like(xdt_g))
                y = y + _dot((cb * lmat).astype(_BF16), x_h)
            state_ref[g] = state * ex_g[CHUNK - 1:CHUNK, :] + _dot_tn(bmat, xd[:, lanes])
            z = p[rows, M_CONV_W + g * M_GW:M_CONV_W + (g + 1) * M_GW].astype(_F32)
            yz = y * _silu(z)
            yz = yz * lax.rsqrt(jnp.mean(yz * yz, axis=-1, keepdims=True) + EPS)
            o[rows, lanes] = (yz * nw_ref[:, lanes]).astype(_BF16)


N_RET_T, N_SWA_T, N_POOL_T, N_SSD_T = 5, 2, 3, 8


def _mixers_kernel(x_ref, mod_ref, nw_ref, wr_ref, ws_ref, wp_ref, wm_ref, wdt_ref, *rest):
    consts, rest = rest[:N_RET_T + N_SWA_T + N_POOL_T + N_SSD_T], rest[N_RET_T + N_SWA_T + N_POOL_T + N_SSD_T:]
    ret_c, consts = consts[:N_RET_T], consts[N_RET_T:]
    swa_c, consts = consts[:N_SWA_T], consts[N_SWA_T:]
    pool_c, ssd_c = consts[:N_POOL_T], consts[N_POOL_T:]
    ret_o, att_o, pool_o, ssm_o, h_o = (r.at[0] for r in rest[:N_BRANCHES + 1])
    proj = rest[N_BRANCHES + 1:N_BRANCHES + 6]
    ret_state, kt_buf, v_buf, pool_hist, conv_hist, xbc_buf, ssd_state = rest[N_BRANCHES + 6:N_BRANCHES + 13]
    pool_levels = rest[N_BRANCHES + 13:]
    first_tile = pl.program_id(1) == 0

    @pl.when(first_tile)
    def _():
        ret_state[...] = jnp.zeros_like(ret_state)
        ssd_state[...] = jnp.zeros_like(ssd_state)
        kt_buf[:, :, 0:CHUNK] = jnp.zeros((SWA_HKV, LANES, CHUNK), _BF16)
        v_buf[0:CHUNK, :] = jnp.zeros((CHUNK, SWA_HKV * SWA_EXT_W), _BF16)
        pool_hist[0:CHUNK, :] = jnp.zeros((CHUNK, BRANCH), _F32)
        for level in pool_levels:
            level[0:POOL_PAD, :] = jnp.zeros((POOL_PAD, BRANCH), _F32)
        conv_hist[0:CONV_HIST, :] = jnp.zeros((CONV_HIST, M_CONV_W), _F32)

    h = _modulated_norm(x_ref[0], mod_ref, nw_ref).astype(_BF16)
    h_o[...] = h
    for ref, w_ref in zip(proj, (wr_ref.at[0], ws_ref.at[0], wp_ref.at[0], wm_ref.at[0], wdt_ref.at[0])):
        ref[...] = _dot(h, w_ref[...]).astype(ref.dtype)

    ret_p, swa_p, pool_p, ssd_p, dt_p = proj
    _retention_tile(ret_p, ret_o, *ret_c, ret_state)
    _swa_tile(first_tile, swa_p, att_o, *swa_c, kt_buf, v_buf)
    _pool_tile(first_tile, pool_p, pool_o, *pool_c, pool_hist, pool_levels)
    _ssd_tile(ssd_p, dt_p, ssm_o, *ssd_c, conv_hist, xbc_buf, ssd_state)


def _mixers(layer, x, mod, norm_w, w_in, swa_sinks, pool_w, pool_scale, conv_w, conv_b, dt_bias, a_log, d_skip,
            ssm_norm_w):
    batch, seq, _ = x.shape
    windows = (RET_COLS, SWA_COLS, POOL_COLS, SSD_COLS, PACKED_DT_COLS)
    consts = list(_retention_tables())
    consts += [_swa_bias(), _swa_sink_rows(swa_sinks)]
    consts += [_pool_inverse_counts(), pool_w.astype(_BF16), pool_scale.reshape(1, BRANCH)]
    consts += [conv_w, conv_b.reshape(1, M_CONV_W), _pad_lanes(dt_bias), _pad_lanes(a_log),
               jnp.repeat(d_skip.astype(_F32), M_P).reshape(1, BRANCH), ssm_norm_w.reshape(1, BRANCH),
               jnp.asarray(np.tril(np.ones((CHUNK, CHUNK), np.float32))), _expansion_matrix()]
    assert len(consts) == N_RET_T + N_SWA_T + N_POOL_T + N_SSD_T
    operands = [x, mod, norm_w] + [w_in] * len(windows) + consts
    in_specs = [_row_spec(D_MODEL), _mod_spec(), _const_spec(norm_w.shape)]
    in_specs += [_weight_window_spec(layer, cols) for cols in windows]
    in_specs += [_const_spec(a.shape) for a in consts]
    scratch = [pltpu.VMEM((ROW_TILE, w), _BF16) for w in MIXER_WIDTHS] + [pltpu.VMEM((ROW_TILE, DT_PAD), _F32)]
    scratch += [pltpu.VMEM((RET_QK_W, RET_V_W), _F32),
                pltpu.VMEM((SWA_HKV, LANES, CHUNK + ROW_TILE), _BF16),
                pltpu.VMEM((CHUNK + ROW_TILE, SWA_HKV * SWA_EXT_W), _BF16),
                pltpu.VMEM((CHUNK + ROW_TILE, BRANCH), _F32),
                pltpu.VMEM((CONV_HIST + ROW_TILE, M_CONV_W), _F32),
                pltpu.VMEM((ROW_TILE, M_CONV_W), _F32),
                pltpu.VMEM((M_GROUPS, M_N, M_GW), _F32)]
    scratch += [pltpu.VMEM((POOL_PAD + POOL_EXT + ROW_TILE, BRANCH), _F32)] * (len(POOL_WINDOWS) - 1)
    return pl.pallas_call(
        _mixers_kernel,
        grid=(batch, seq // ROW_TILE),
        in_specs=in_specs,
        out_specs=[_row_spec(BRANCH)] * N_BRANCHES + [_row_spec(D_MODEL)],
        out_shape=[jax.ShapeDtypeStruct((batch, seq, BRANCH), _BF16)] * N_BRANCHES
        + [jax.ShapeDtypeStruct((batch, seq, D_MODEL), _BF16)],
        scratch_shapes=scratch,
        compiler_params=_params(),
        name="norm_proj_mixers",
    )(*operands)


def _merge_kernel(final, x_ref, mod_ref, h_ref, wg_ref, ret_ref, att_ref, pool_ref, ssm_ref, wup_ref, wout_ref,
                  *rest):
    if final:
        fnw_ref, o_ref = rest
    else:
        (o_ref,) = rest
    h = h_ref[0]
    merged = None
    for i, br_ref in enumerate((ret_ref, att_ref, pool_ref, ssm_ref)):
        gate = _sigmoid(_dot(h, wg_ref[0, :, i * D_MODEL:(i + 1) * D_MODEL]))
        term = gate * _dot(br_ref[0], wup_ref[i])
        merged = term if merged is None else merged + term
    out = _dot(merged.astype(_BF16), wout_ref[...])
    y = x_ref[0] + mod_ref[0, :, 2 * D_MODEL:3 * D_MODEL] * out
    if final:
        y = y * lax.rsqrt(jnp.mean(y * y, axis=-1, keepdims=True) + EPS) * fnw_ref[...]
    o_ref[0] = y


def _merge(layer, x, mod, h, w_in, branches, w_up, w_out, final_norm_w):
    batch, seq, _ = x.shape
    final = final_norm_w is not None
    in_specs = [_row_spec(D_MODEL, MERGE_TILE), _mod_spec(), _row_spec(D_MODEL, MERGE_TILE),
                _weight_window_spec(layer, PACKED_MG_COLS)]
    in_specs += [_row_spec(BRANCH, MERGE_TILE)] * N_BRANCHES
    in_specs += [_const_spec(w_up.shape), _const_spec(w_out.shape)]
    args = [x, mod, h, w_in, *branches, w_up, w_out]
    if final:
        in_specs.append(_const_spec((1, D_MODEL)))
        args.append(final_norm_w.reshape(1, D_MODEL))
    return pl.pallas_call(
        functools.partial(_merge_kernel, final),
        grid=(batch, seq // MERGE_TILE),
        in_specs=in_specs,
        out_specs=_row_spec(D_MODEL, MERGE_TILE),
        out_shape=jax.ShapeDtypeStruct(x.shape, _F32),
        compiler_params=_params(),
        name="merge_out_proj",
    )(*args)


def kernel(x, c, ada_w, ada_b, norm_w, w_in, swa_sinks, pool_w, pool_scale, conv_w, conv_b, dt_bias, a_log,
           d_skip, ssm_norm_w, w_up, w_out, final_norm_w):
    depth = ada_w.shape[0]
    batch = x.shape[0]
    mod_all = _modulation(c, ada_w, ada_b)
    dt_padding = jnp.zeros(w_in.shape[:2] + (DT_PAD - M_HEADS,), _BF16)
    w_packed = jnp.concatenate([w_in[..., :DT_COLS[1]].astype(_BF16), dt_padding,
                                w_in[..., MG_COLS[0]:].astype(_BF16)], axis=-1)
    for l in range(depth):
        mod = mod_all[l].reshape(batch, 1, 3 * D_MODEL)
        nw = norm_w[l].reshape(1, D_MODEL)
        *branches, h = _mixers(l, x, mod, nw, w_packed, swa_sinks[l], pool_w[l], pool_scale[l], conv_w[l], conv_b[l],
                               dt_bias[l], a_log[l], d_skip[l], ssm_norm_w[l])
        x = _merge(l, x, mod, h, w_packed, branches, w_up[l].astype(_BF16), w_out[l].astype(_BF16),
                   final_norm_w if l == depth - 1 else None)
    return x
```

```python
import functools

import jax
import jax.numpy as jnp
import numpy as np
from jax import lax
from jax.experimental import pallas as pl
from jax.experimental.pallas import tpu as pltpu

D_MODEL = 1024
BRANCH = 512
N_BRANCHES = 4
EPS = 1e-6
CHUNK = 128
LANES = 128

RET_HEADS, RET_DK, RET_DV = 4, 64, 128
RET_QK_W = RET_HEADS * RET_DK
RET_V_W = RET_HEADS * RET_DV
SWA_HQ, SWA_HKV, SWA_D = 8, 2, 64
SWA_GROUP = SWA_HQ // SWA_HKV
SWA_Q_W = SWA_HQ * SWA_D
SWA_KV_W = SWA_HKV * SWA_D
SWA_EXT_W = 2 * LANES
LOG2_E = 1.4426950408889634
POOL_WINDOWS = (2, 4, 8, 16)
POOL_GD = BRANCH // len(POOL_WINDOWS)
POOL_EXT = 16
POOL_PAD = 8
M_HEADS, M_P, M_GROUPS, M_N, M_CONV = 8, 64, 2, 128, 4
M_R = M_HEADS // M_GROUPS
M_GW = M_R * M_P
M_BC = M_GROUPS * M_N
M_X_W = M_HEADS * M_P
M_CONV_W = M_X_W + 2 * M_BC
CONV_HIST = 8
DT_PAD = LANES

_SIZES = (256, 256, 512, 512, 512, 128, 128, 512, 512, 512, 1024, 512, M_HEADS, N_BRANCHES * D_MODEL)
_OFFS = np.concatenate([[0], np.cumsum(_SIZES)]).tolist()
RET_COLS = (_OFFS[0], _OFFS[4])
SWA_COLS = (_OFFS[4], _OFFS[8])
POOL_COLS = (_OFFS[8], _OFFS[10])
SSD_COLS = (_OFFS[10], _OFFS[12])
DT_COLS = (_OFFS[12], _OFFS[13])
MG_COLS = (_OFFS[13], _OFFS[14])
MIXER_WIDTHS = tuple(b - a for a, b in (RET_COLS, SWA_COLS, POOL_COLS, SSD_COLS))

ROW_TILE = 512
MERGE_TILE = 1024
N_CHUNKS = ROW_TILE // CHUNK
V7X_VMEM_LIMIT = 56 * 1024 * 1024

_F32 = jnp.float32
_BF16 = jnp.bfloat16


def _sigmoid(x):
    return 0.5 * jnp.tanh(0.5 * x) + 0.5


def _silu(x):
    half = 0.5 * x
    return half + half * jnp.tanh(half)


def _softplus(x):
    return jnp.maximum(x, 0.0) + jnp.log(1.0 + jnp.exp(-jnp.abs(x)))


def _dot(a, b):
    return jnp.dot(a, b, preferred_element_type=_F32)


def _dot_nt(a, b):
    return lax.dot_general(a, b, (((1,), (1,)), ((), ())), preferred_element_type=_F32)


def _dot_tn(a, b):
    return lax.dot_general(a, b, (((0,), (0,)), ((), ())), preferred_element_type=_F32)


def _const_spec(shape):
    nd = len(shape)
    return pl.BlockSpec(shape, lambda *_: (0,) * nd, pipeline_mode=pl.Buffered(1))


def _weight_window_spec(layer, cols):
    return pl.BlockSpec((pl.Element(1), pl.Element(D_MODEL), pl.Element(cols[1] - cols[0])),
                        lambda *_: (layer, 0, cols[0]), pipeline_mode=pl.Buffered(1))


def _row_spec(width, tile=None):
    return pl.BlockSpec((1, tile or ROW_TILE, width), lambda b, t: (b, t, 0))


def _mod_spec():
    return pl.BlockSpec((1, 1, 3 * D_MODEL), lambda b, t: (b, 0, 0))


def _params():
    return pltpu.CompilerParams(dimension_semantics=("parallel", "arbitrary"),
                                vmem_limit_bytes=V7X_VMEM_LIMIT)


def _modulated_norm(x, mod_ref, nw_ref):
    shift = mod_ref[0, :, 0:D_MODEL]
    scale = mod_ref[0, :, D_MODEL:2 * D_MODEL]
    rs = lax.rsqrt(jnp.mean(x * x, axis=-1, keepdims=True) + EPS)
    return (x * rs) * nw_ref[...] * (1.0 + scale) + shift


def _mod_kernel(c_ref, w_ref, b_ref, o_ref):
    sc = _silu(c_ref[...]).astype(_BF16)
    o_ref[0] = _dot(sc, w_ref[0].astype(_BF16)) + b_ref[0]


def _modulation(c, ada_w, ada_b):
    depth, _, n = ada_w.shape
    bn = D_MODEL
    batch = c.shape[0]
    return pl.pallas_call(
        _mod_kernel,
        grid=(depth, n // bn),
        in_specs=[pl.BlockSpec((batch, D_MODEL), lambda l, j: (0, 0)),
                  pl.BlockSpec((1, D_MODEL, bn), lambda l, j: (l, 0, j)),
                  pl.BlockSpec((1, 1, bn), lambda l, j: (l, 0, j))],
        out_specs=pl.BlockSpec((1, batch, bn), lambda l, j: (l, 0, j)),
        out_shape=jax.ShapeDtypeStruct((depth, batch, n), _F32),
        name="adaln_modulation",
    )(c, ada_w, ada_b.reshape(depth, 1, n))


CAST_ROWS = 128


def _cast_kernel(w_ref, o_ref):
    o_ref[...] = w_ref[...].astype(_BF16)


def _cast_weights(w):
    depth, rows, n = w.shape
    spec = pl.BlockSpec((1, CAST_ROWS, n), lambda l, i: (l, i, 0))
    return pl.pallas_call(
        _cast_kernel,
        grid=(depth, rows // CAST_ROWS),
        in_specs=[spec],
        out_specs=spec,
        out_shape=jax.ShapeDtypeStruct(w.shape, _BF16),
        name="cast_projection_weights",
    )(w)


def _retention_tables():
    log_g = np.log(1.0 - 2.0 ** (-5.0 - np.arange(RET_HEADS, dtype=np.float64)))
    pos = np.arange(CHUNK, dtype=np.float64)
    diff = pos[:, None] - pos[None, :]
    inner = np.where(diff >= 0, np.exp(log_g[:, None, None] * np.where(diff >= 0, diff, 0.0)), 0.0)
    q_decay = np.repeat(np.exp(log_g[:, None] * (pos + 1.0)).T, RET_DV, axis=1)
    k_decay = np.repeat(np.exp(log_g[:, None] * (CHUNK - 1.0 - pos)).T, RET_DK, axis=1)
    chunk_decay = np.repeat(np.exp(log_g * CHUNK), RET_DV)[None, :]
    block_diag = (np.arange(RET_QK_W)[:, None] // RET_DK == np.arange(RET_V_W)[None, :] // RET_DV)
    return tuple(jnp.asarray(t, _F32) for t in (inner, q_decay, k_decay, chunk_decay, block_diag))


def _retention_tile(p, o, inner_ref, qd_ref, kd_ref, cd_ref, bd_ref, state_ref):
    head_of_lane = lax.broadcasted_iota(jnp.int32, (CHUNK, RET_QK_W), 1) // RET_DK
    for ci in range(N_CHUNKS):
        rows = slice(ci * CHUNK, (ci + 1) * CHUNK)
        q = p[rows, 0:RET_QK_W]
        k = p[rows, RET_QK_W:2 * RET_QK_W].astype(_F32) * (RET_DK ** -0.5)
        k_bf = k.astype(_BF16)
        v = p[rows, 2 * RET_QK_W:2 * RET_QK_W + RET_V_W]
        state = state_ref[...]
        cross = _dot(q, state.astype(_BF16)) * qd_ref[...]
        update = _dot_tn((k * kd_ref[...]).astype(_BF16), v)
        state_ref[...] = state * cd_ref[...] + update * bd_ref[...]
        for hd in range(RET_HEADS):
            lanes = slice(hd * RET_DV, (hd + 1) * RET_DV)
            q_h = jnp.where(head_of_lane == hd, q, jnp.zeros_like(q))
            scores = _dot_nt(q_h, k_bf) * inner_ref[hd]
            out = _dot(scores.astype(_BF16), v[:, lanes]) + cross[:, lanes]
            out = out * lax.rsqrt(jnp.mean(out * out, axis=-1, keepdims=True) + EPS)
            g0 = 2 * RET_QK_W + RET_V_W + hd * RET_DV
            gate = p[rows, g0:g0 + RET_DV].astype(_F32)
            o[rows, lanes] = (out * _silu(gate)).astype(_BF16)


def _swa_bias():
    slopes = 2.0 ** (-8.0 * np.arange(1, SWA_HQ + 1, dtype=np.float64) / SWA_HQ)
    qi = np.arange(CHUNK)
    kj = np.arange(2 * CHUNK)
    delta = CHUNK + qi[:, None] - kj[None, :]
    valid = (delta >= 0) & (delta < CHUNK)
    variants = []
    for ok in (valid, valid & (kj[None, :] >= CHUNK)):
        bias = np.where(ok[None], -slopes[:, None, None] * delta[None].astype(np.float64), -np.inf)
        variants.append(bias.reshape(SWA_HKV, SWA_GROUP * CHUNK, 2 * CHUNK))
    return jnp.asarray(np.stack(variants) * LOG2_E, _F32)


def _swa_sink_rows(sinks):
    rows = jnp.repeat(sinks.astype(_F32) * LOG2_E, CHUNK)[:, None]
    return jnp.broadcast_to(rows, (SWA_HQ * CHUNK, LANES)).reshape(SWA_HKV, SWA_GROUP * CHUNK, LANES)


def _swa_tile(first_tile, p, o, bias_ref, sink_ref, kt_buf, v_buf):
    low_half = lax.broadcasted_iota(jnp.int32, (ROW_TILE, SWA_KV_W), 1) < SWA_D
    k = p[:, SWA_Q_W:SWA_Q_W + SWA_KV_W].astype(_F32)
    k_swapped = pltpu.roll(k, SWA_D, axis=1)
    for j, dup in enumerate((jnp.where(low_half, k, k_swapped), jnp.where(low_half, k_swapped, k))):
        for bi in range(N_CHUNKS):
            kt_buf[j, :, (bi + 1) * CHUNK:(bi + 2) * CHUNK] = dup[bi * CHUNK:(bi + 1) * CHUNK, :].T.astype(_BF16)
    v = p[:, SWA_Q_W + SWA_KV_W:SWA_Q_W + 2 * SWA_KV_W].astype(_F32)
    v_swapped = pltpu.roll(v, SWA_D, axis=1)
    for j, dup in enumerate((jnp.where(low_half, v, v_swapped), jnp.where(low_half, v_swapped, v))):
        v_buf[CHUNK:CHUNK + ROW_TILE, j * SWA_EXT_W:j * SWA_EXT_W + LANES] = dup.astype(_BF16)
        v_buf[CHUNK:CHUNK + ROW_TILE, j * SWA_EXT_W + LANES:(j + 1) * SWA_EXT_W] = jnp.ones((ROW_TILE, LANES), _BF16)

    low_lanes = lax.broadcasted_iota(jnp.int32, (CHUNK, LANES), 1) < SWA_D
    high_lanes = jnp.logical_not(low_lanes)
    first_variant = first_tile.astype(jnp.int32)
    gate0 = SWA_Q_W + 2 * SWA_KV_W

    for bi in range(N_CHUNKS):
        rows = slice(bi * CHUNK, (bi + 1) * CHUNK)
        keys = slice(bi * CHUNK, (bi + 2) * CHUNK)
        variant = first_variant if bi == 0 else 0
        for j in range(SWA_HKV):
            k_t = kt_buf[j, :, keys]
            v_ext = v_buf[keys, j * SWA_EXT_W:(j + 1) * SWA_EXT_W]
            for pair in range(SWA_GROUP // 2):
                c0 = (j * SWA_GROUP // 2 + pair) * LANES
                qp = p[rows, c0:c0 + LANES].astype(_F32) * (SWA_D ** -0.5 * LOG2_E)
                halves = []
                for odd in range(2):
                    g_rows = slice((2 * pair + odd) * CHUNK, (2 * pair + odd + 1) * CHUNK)
                    q_h = jnp.where(high_lanes if odd else low_lanes, qp, 0.0).astype(_BF16)
                    s = _dot(q_h, k_t) + bias_ref[variant, j, g_rows, :]
                    s_prev, s_cur = s[:, 0:CHUNK], s[:, CHUNK:2 * CHUNK]
                    sink = sink_ref[j, g_rows, :]
                    m = jnp.maximum(jnp.max(jnp.maximum(s_prev, s_cur), axis=-1, keepdims=True), sink)
                    e = jnp.concatenate([jnp.exp2(s_prev - m), jnp.exp2(s_cur - m)], axis=1).astype(_BF16)
                    ov = _dot(e, v_ext)
                    halves.append(ov[:, 0:LANES] / (ov[:, LANES:2 * LANES] + jnp.exp2(sink - m)))
                gate = p[rows, gate0 + c0:gate0 + c0 + LANES].astype(_F32)
                o[rows, c0:c0 + LANES] = (jnp.where(low_lanes, halves[0], halves[1]) * _silu(gate)).astype(_BF16)

    kt_buf[:, :, 0:CHUNK] = kt_buf[:, :, ROW_TILE:ROW_TILE + CHUNK]
    v_buf[0:CHUNK, :] = v_buf[ROW_TILE:ROW_TILE + CHUNK, :]


def _pool_inverse_counts():
    pos = np.arange(ROW_TILE, dtype=np.float64)[:, None]
    windows = np.repeat(np.asarray(POOL_WINDOWS, np.float64), POOL_GD)[None, :]
    steady = np.broadcast_to(1.0 / windows, (ROW_TILE, BRANCH))
    start = 1.0 / np.minimum(pos + 1.0, windows)
    return jnp.asarray(np.stack([steady, start]), _F32)


def _pool_tile(first_tile, p, o, inv_ref, w_ref, scale_ref, hist, levels):
    hist[CHUNK:CHUNK + ROW_TILE, :] = p[:, 0:BRANCH].astype(_F32)
    inv = inv_ref[first_tile.astype(jnp.int32)]
    ext_rows = POOL_EXT + ROW_TILE
    base = CHUNK - POOL_EXT
    sums = []
    for g, w in enumerate(POOL_WINDOWS):
        c0 = g * POOL_GD
        shift = w // 2
        if g == 0:
            total = hist[base:base + ext_rows, :] + hist[base - shift:base - shift + ext_rows, :]
        else:
            prev = levels[g - 1]
            total = (prev[POOL_PAD:POOL_PAD + ext_rows, c0:]
                     + prev[POOL_PAD - shift:POOL_PAD - shift + ext_rows, c0:])
        if g + 1 < len(POOL_WINDOWS):
            levels[g][POOL_PAD:POOL_PAD + ext_rows, c0:] = total
        sums.append(total)
    for g in range(len(POOL_WINDOWS)):
        lanes = slice(g * POOL_GD, (g + 1) * POOL_GD)
        u = hist[CHUNK:CHUNK + ROW_TILE, lanes]
        acc = sums[g][POOL_EXT:, 0:POOL_GD]
        diff = acc * inv[:, lanes] - u
        y = _dot(diff.astype(_BF16), w_ref[g])
        gate = p[:, BRANCH + g * POOL_GD:BRANCH + (g + 1) * POOL_GD].astype(_F32)
        o[:, lanes] = (y * scale_ref[:, lanes] * _silu(gate)).astype(_BF16)
    hist[0:CHUNK, :] = hist[ROW_TILE:ROW_TILE + CHUNK, :]


def _expansion_matrix():
    e = np.zeros((2 * LANES, M_X_W), np.float32)
    for hd in range(M_HEADS):
        e[hd, hd * M_P:(hd + 1) * M_P] = 1.0
        e[LANES + hd, hd * M_P:(hd + 1) * M_P] = 1.0
    return jnp.asarray(e, _BF16)


def _pad_lanes(v, width=DT_PAD):
    return jnp.pad(v.astype(_F32), (0, width - v.shape[0])).reshape(1, width)


def _ssd_tile(p, dt_raw, o, cw_ref, cb_ref, dtb_ref, alog_ref, dskip_ref, nw_ref, tri_ref, exp_ref,
              hist, xbc_buf, state_ref):
    hist[CONV_HIST:CONV_HIST + ROW_TILE, :] = p[:, 0:M_CONV_W].astype(_F32)
    acc = cb_ref[...] + cw_ref[M_CONV - 1:M_CONV, :] * hist[CONV_HIST:CONV_HIST + ROW_TILE, :]
    for j in range(1, M_CONV):
        acc = acc + cw_ref[M_CONV - 1 - j:M_CONV - j, :] * hist[CONV_HIST - j:CONV_HIST - j + ROW_TILE, :]
    xbc_buf[...] = _silu(acc)
    hist[0:CONV_HIST, :] = hist[ROW_TILE:ROW_TILE + CONV_HIST, :]

    a_row = -jnp.exp(alog_ref[...])
    causal = (lax.broadcasted_iota(jnp.int32, (CHUNK, CHUNK), 0)
              >= lax.broadcasted_iota(jnp.int32, (CHUNK, CHUNK), 1))
    head_of_lane = lax.broadcasted_iota(jnp.int32, (CHUNK, M_GW), 1) // M_P

    def expand(vals):
        stacked = jnp.concatenate(vals, axis=0)
        hi = stacked.astype(_BF16)
        lo = (stacked - hi.astype(_F32)).astype(_BF16)
        wide = _dot(jnp.concatenate([hi, lo], axis=1), exp_ref[...])
        return [wide[i * CHUNK:(i + 1) * CHUNK, :] for i in range(len(vals))]

    for ci in range(N_CHUNKS):
        rows = slice(ci * CHUNK, (ci + 1) * CHUNK)
        dt = _softplus(dt_raw[rows, :] + dtb_ref[...])
        acs = jnp.dot(tri_ref[...], dt * a_row, preferred_element_type=_F32, precision=lax.Precision.HIGHEST)
        acs_t = acs.T
        last = acs[CHUNK - 1:CHUNK, :]
        dt_x, w_x, ex_x = expand([dt, dt * jnp.exp(last - acs), jnp.exp(acs)])
        x = xbc_buf[rows, 0:M_X_W]
        xdt = (x * dt_x).astype(_BF16)
        xd = (x * w_x).astype(_BF16)
        for g in range(M_GROUPS):
            lanes = slice(g * M_GW, (g + 1) * M_GW)
            bmat = xbc_buf[rows, M_X_W + g * M_N:M_X_W + (g + 1) * M_N].astype(_BF16)
            cmat = xbc_buf[rows, M_X_W + M_BC + g * M_N:M_X_W + M_BC + (g + 1) * M_N].astype(_BF16)
            cb = _dot_nt(cmat, bmat)
            state = state_ref[g]
            ex_g = ex_x[:, lanes]
            y = _dot(cmat, state.astype(_BF16)) * ex_g + x[:, lanes] * dskip_ref[:, lanes]
            xdt_g = xdt[:, lanes]
            for r in range(M_R):
                hd = g * M_R + r
                seg = acs[:, hd:hd + 1] - acs_t[hd:hd + 1, :]
                lmat = jnp.exp(jnp.where(causal, seg, -jnp.inf))
                x_h = jnp.where(head_of_lane == r, xdt_g, jnp.zeros_like(xdt_g))
                y = y + _dot((cb * lmat).astype(_BF16), x_h)
            state_ref[g] = state * ex_g[CHUNK - 1:CHUNK, :] + _dot_tn(bmat, xd[:, lanes])
            z = p[rows, M_CONV_W + g * M_GW:M_CONV_W + (g + 1) * M_GW].astype(_F32)
            yz = y * _silu(z)
            yz = yz * lax.rsqrt(jnp.mean(yz * yz, axis=-1, keepdims=True) + EPS)
            o[rows, lanes] = (yz * nw_ref[:, lanes]).astype(_BF16)


N_RET_T, N_SWA_T, N_POOL_T, N_SSD_T = 5, 2, 3, 8


def _mixers_kernel(x_ref, mod_ref, nw_ref, wr_ref, ws_ref, wp_ref, wm_ref, wdt_ref, *rest):
    consts, rest = rest[:N_RET_T + N_SWA_T + N_POOL_T + N_SSD_T], rest[N_RET_T + N_SWA_T + N_POOL_T + N_SSD_T:]
    ret_c, consts = consts[:N_RET_T], consts[N_RET_T:]
    swa_c, consts = consts[:N_SWA_T], consts[N_SWA_T:]
    pool_c, ssd_c = consts[:N_POOL_T], consts[N_POOL_T:]
    ret_o, att_o, pool_o, ssm_o, h_o = (r.at[0] for r in rest[:N_BRANCHES + 1])
    proj = rest[N_BRANCHES + 1:N_BRANCHES + 6]
    ret_state, kt_buf, v_buf, pool_hist, conv_hist, xbc_buf, ssd_state = rest[N_BRANCHES + 6:N_BRANCHES + 13]
    pool_levels = rest[N_BRANCHES + 13:]
    first_tile = pl.program_id(1) == 0

    @pl.when(first_tile)
    def _():
        ret_state[...] = jnp.zeros_like(ret_state)
        ssd_state[...] = jnp.zeros_like(ssd_state)
        kt_buf[:, :, 0:CHUNK] = jnp.zeros((SWA_HKV, LANES, CHUNK), _BF16)
        v_buf[0:CHUNK, :] = jnp.zeros((CHUNK, SWA_HKV * SWA_EXT_W), _BF16)
        pool_hist[0:CHUNK, :] = jnp.zeros((CHUNK, BRANCH), _F32)
        for level in pool_levels:
            level[0:POOL_PAD, :] = jnp.zeros((POOL_PAD, BRANCH), _F32)
        conv_hist[0:CONV_HIST, :] = jnp.zeros((CONV_HIST, M_CONV_W), _F32)

    h = _modulated_norm(x_ref[0], mod_ref, nw_ref).astype(_BF16)
    h_o[...] = h
    for ref, w_ref in zip(proj, (wr_ref.at[0], ws_ref.at[0], wp_ref.at[0], wm_ref.at[0], wdt_ref)):
        ref[...] = _dot(h, w_ref[...]).astype(ref.dtype)

    ret_p, swa_p, pool_p, ssd_p, dt_p = proj
    _retention_tile(ret_p, ret_o, *ret_c, ret_state)
    _swa_tile(first_tile, swa_p, att_o, *swa_c, kt_buf, v_buf)
    _pool_tile(first_tile, pool_p, pool_o, *pool_c, pool_hist, pool_levels)
    _ssd_tile(ssd_p, dt_p, ssm_o, *ssd_c, conv_hist, xbc_buf, ssd_state)


def _mixers(layer, x, mod, norm_w, w_in, w_dt, swa_sinks, pool_w, pool_scale, conv_w, conv_b, dt_bias, a_log,
            d_skip, ssm_norm_w):
    batch, seq, _ = x.shape
    windows = (RET_COLS, SWA_COLS, POOL_COLS, SSD_COLS)
    consts = list(_retention_tables())
    consts += [_swa_bias(), _swa_sink_rows(swa_sinks)]
    consts += [_pool_inverse_counts(), pool_w.astype(_BF16), pool_scale.reshape(1, BRANCH)]
    consts += [conv_w, conv_b.reshape(1, M_CONV_W), _pad_lanes(dt_bias), _pad_lanes(a_log),
               jnp.repeat(d_skip.astype(_F32), M_P).reshape(1, BRANCH), ssm_norm_w.reshape(1, BRANCH),
               jnp.asarray(np.tril(np.ones((CHUNK, CHUNK), np.float32))), _expansion_matrix()]
    assert len(consts) == N_RET_T + N_SWA_T + N_POOL_T + N_SSD_T
    operands = [x, mod, norm_w] + [w_in] * len(windows) + [w_dt, *consts]
    in_specs = [_row_spec(D_MODEL), _mod_spec(), _const_spec(norm_w.shape)]
    in_specs += [_weight_window_spec(layer, cols) for cols in windows]
    in_specs += [_const_spec(a.shape) for a in [w_dt, *consts]]
    scratch = [pltpu.VMEM((ROW_TILE, w), _BF16) for w in MIXER_WIDTHS] + [pltpu.VMEM((ROW_TILE, DT_PAD), _F32)]
    scratch += [pltpu.VMEM((RET_QK_W, RET_V_W), _F32),
                pltpu.VMEM((SWA_HKV, LANES, CHUNK + ROW_TILE), _BF16),
                pltpu.VMEM((CHUNK + ROW_TILE, SWA_HKV * SWA_EXT_W), _BF16),
                pltpu.VMEM((CHUNK + ROW_TILE, BRANCH), _F32),
                pltpu.VMEM((CONV_HIST + ROW_TILE, M_CONV_W), _F32),
                pltpu.VMEM((ROW_TILE, M_CONV_W), _F32),
                pltpu.VMEM((M_GROUPS, M_N, M_GW), _F32)]
    scratch += [pltpu.VMEM((POOL_PAD + POOL_EXT + ROW_TILE, BRANCH), _F32)] * (len(POOL_WINDOWS) - 1)
    return pl.pallas_call(
        _mixers_kernel,
        grid=(batch, seq // ROW_TILE),
        in_specs=in_specs,
        out_specs=[_row_spec(BRANCH)] * N_BRANCHES + [_row_spec(D_MODEL)],
        out_shape=[jax.ShapeDtypeStruct((batch, seq, BRANCH), _BF16)] * N_BRANCHES
        + [jax.ShapeDtypeStruct((batch, seq, D_MODEL), _BF16)],
        scratch_shapes=scratch,
        compiler_params=_params(),
        name="norm_proj_mixers",
    )(*operands)


def _merge_kernel(final, x_ref, mod_ref, h_ref, wg_ref, ret_ref, att_ref, pool_ref, ssm_ref, wup_ref, wout_ref,
                  *rest):
    if final:
        fnw_ref, o_ref = rest
    else:
        (o_ref,) = rest
    h = h_ref[0]
    merged = None
    for i, br_ref in enumerate((ret_ref, att_ref, pool_ref, ssm_ref)):
        gate = _sigmoid(_dot(h, wg_ref[:, i * D_MODEL:(i + 1) * D_MODEL]))
        term = gate * _dot(br_ref[0], wup_ref[i])
        merged = term if merged is None else merged + term
    out = _dot(merged.astype(_BF16), wout_ref[...])
    y = x_ref[0] + mod_ref[0, :, 2 * D_MODEL:3 * D_MODEL] * out
    if final:
        y = y * lax.rsqrt(jnp.mean(y * y, axis=-1, keepdims=True) + EPS) * fnw_ref[...]
    o_ref[0] = y


def _merge(x, mod, h, wg, branches, w_up, w_out, final_norm_w):
    batch, seq, _ = x.shape
    final = final_norm_w is not None
    in_specs = [_row_spec(D_MODEL, MERGE_TILE), _mod_spec(), _row_spec(D_MODEL, MERGE_TILE), _const_spec(wg.shape)]
    in_specs += [_row_spec(BRANCH, MERGE_TILE)] * N_BRANCHES
    in_specs += [_const_spec(w_up.shape), _const_spec(w_out.shape)]
    args = [x, mod, h, wg, *branches, w_up, w_out]
    if final:
        in_specs.append(_const_spec((1, D_MODEL)))
        args.append(final_norm_w.reshape(1, D_MODEL))
    return pl.pallas_call(
        functools.partial(_merge_kernel, final),
        grid=(batch, seq // MERGE_TILE),
        in_specs=in_specs,
        out_specs=_row_spec(D_MODEL, MERGE_TILE),
        out_shape=jax.ShapeDtypeStruct(x.shape, _F32),
        compiler_params=_params(),
        name="merge_out_proj",
    )(*args)


def kernel(x, c, ada_w, ada_b, norm_w, w_in, swa_sinks, pool_w, pool_scale, conv_w, conv_b, dt_bias, a_log,
           d_skip, ssm_norm_w, w_up, w_out, final_norm_w):
    depth = ada_w.shape[0]
    batch = x.shape[0]
    mod_all = _modulation(c, ada_w, ada_b)
    w_in_bf = _cast_weights(w_in)
    for l in range(depth):
        mod = mod_all[l].reshape(batch, 1, 3 * D_MODEL)
        nw = norm_w[l].reshape(1, D_MODEL)
        w_dt = jnp.pad(w_in_bf[l, :, DT_COLS[0]:DT_COLS[1]], ((0, 0), (0, DT_PAD - M_HEADS)))
        *branches, h = _mixers(l, x, mod, nw, w_in_bf, w_dt, swa_sinks[l], pool_w[l], pool_scale[l], conv_w[l],
                               conv_b[l], dt_bias[l], a_log[l], d_skip[l], ssm_norm_w[l])
        wg = w_in_bf[l, :, MG_COLS[0]:MG_COLS[1]]
        x = _merge(x, mod, h, wg, branches, w_up[l].astype(_BF16), w_out[l].astype(_BF16),
                   final_norm_w if l == depth - 1 else None)
    return x
```

```python
import functools

import jax
import jax.numpy as jnp
import numpy as np
from jax import lax
from jax.experimental import pallas as pl
from jax.experimental.pallas import tpu as pltpu

D_MODEL = 1024
BRANCH = 512
N_BRANCHES = 4
EPS = 1e-6
CHUNK = 128
LANES = 128

RET_HEADS, RET_DK, RET_DV = 4, 64, 128
RET_QK_W = RET_HEADS * RET_DK
RET_V_W = RET_HEADS * RET_DV
SWA_HQ, SWA_HKV, SWA_D = 8, 2, 64
SWA_GROUP = SWA_HQ // SWA_HKV
SWA_Q_W = SWA_HQ * SWA_D
SWA_KV_W = SWA_HKV * SWA_D
SWA_EXT_W = 2 * LANES
LOG2_E = 1.4426950408889634
POOL_WINDOWS = (2, 4, 8, 16)
POOL_GD = BRANCH // len(POOL_WINDOWS)
POOL_EXT = 16
POOL_PAD = 8
M_HEADS, M_P, M_GROUPS, M_N, M_CONV = 8, 64, 2, 128, 4
M_R = M_HEADS // M_GROUPS
M_GW = M_R * M_P
M_BC = M_GROUPS * M_N
M_X_W = M_HEADS * M_P
M_CONV_W = M_X_W + 2 * M_BC
CONV_HIST = 8
DT_PAD = LANES

_SIZES = (256, 256, 512, 512, 512, 128, 128, 512, 512, 512, 1024, 512, M_HEADS, N_BRANCHES * D_MODEL)
_OFFS = np.concatenate([[0], np.cumsum(_SIZES)]).tolist()
RET_COLS = (_OFFS[0], _OFFS[4])
SWA_COLS = (_OFFS[4], _OFFS[8])
POOL_COLS = (_OFFS[8], _OFFS[10])
SSD_COLS = (_OFFS[10], _OFFS[12])
DT_COLS = (_OFFS[12], _OFFS[13])
MG_COLS = (_OFFS[13], _OFFS[14])
MIXER_WIDTHS = (RET_COLS[1] - RET_COLS[0], SWA_COLS[1] - SWA_COLS[0], BRANCH, BRANCH)

ROW_TILE = 512
MERGE_TILE = 1024
N_CHUNKS = ROW_TILE // CHUNK
V7X_VMEM_LIMIT = 56 * 1024 * 1024

_F32 = jnp.float32
_BF16 = jnp.bfloat16


def _sigmoid(x):
    return 0.5 * jnp.tanh(0.5 * x) + 0.5


def _silu(x):
    half = 0.5 * x
    return half + half * jnp.tanh(half)


def _softplus(x):
    return jnp.maximum(x, 0.0) + jnp.log(1.0 + jnp.exp(-jnp.abs(x)))


def _dot(a, b):
    return jnp.dot(a, b, preferred_element_type=_F32)


def _dot_nt(a, b):
    return lax.dot_general(a, b, (((1,), (1,)), ((), ())), preferred_element_type=_F32)


def _dot_tn(a, b):
    return lax.dot_general(a, b, (((0,), (0,)), ((), ())), preferred_element_type=_F32)


def _const_spec(shape):
    nd = len(shape)
    return pl.BlockSpec(shape, lambda *_: (0,) * nd, pipeline_mode=pl.Buffered(1))


def _weight_window_spec(layer, cols):
    return pl.BlockSpec((pl.Element(1), pl.Element(D_MODEL), pl.Element(cols[1] - cols[0])),
                        lambda *_: (layer, 0, cols[0]), pipeline_mode=pl.Buffered(1))


def _row_spec(width, tile=None):
    return pl.BlockSpec((1, tile or ROW_TILE, width), lambda b, t: (b, t, 0))


def _mod_spec():
    return pl.BlockSpec((1, 1, 3 * D_MODEL), lambda b, t: (b, 0, 0))


def _params():
    return pltpu.CompilerParams(dimension_semantics=("parallel", "arbitrary"),
                                vmem_limit_bytes=V7X_VMEM_LIMIT)


def _modulated_norm(x, mod_ref, nw_ref):
    shift = mod_ref[0, :, 0:D_MODEL]
    scale = mod_ref[0, :, D_MODEL:2 * D_MODEL]
    rs = lax.rsqrt(jnp.mean(x * x, axis=-1, keepdims=True) + EPS)
    return (x * rs) * nw_ref[...] * (1.0 + scale) + shift


def _mod_kernel(c_ref, w_ref, b_ref, o_ref):
    sc = _silu(c_ref[...]).astype(_BF16)
    o_ref[0] = _dot(sc, w_ref[0].astype(_BF16)) + b_ref[0]


def _modulation(c, ada_w, ada_b):
    depth, _, n = ada_w.shape
    bn = D_MODEL
    batch = c.shape[0]
    return pl.pallas_call(
        _mod_kernel,
        grid=(depth, n // bn),
        in_specs=[pl.BlockSpec((batch, D_MODEL), lambda l, j: (0, 0)),
                  pl.BlockSpec((1, D_MODEL, bn), lambda l, j: (l, 0, j)),
                  pl.BlockSpec((1, 1, bn), lambda l, j: (l, 0, j))],
        out_specs=pl.BlockSpec((1, batch, bn), lambda l, j: (l, 0, j)),
        out_shape=jax.ShapeDtypeStruct((depth, batch, n), _F32),
        name="adaln_modulation",
    )(c, ada_w, ada_b.reshape(depth, 1, n))


def _retention_tables():
    log_g = np.log(1.0 - 2.0 ** (-5.0 - np.arange(RET_HEADS, dtype=np.float64)))
    pos = np.arange(CHUNK, dtype=np.float64)
    diff = pos[:, None] - pos[None, :]
    inner = np.where(diff >= 0, np.exp(log_g[:, None, None] * np.where(diff >= 0, diff, 0.0)), 0.0)
    q_decay = np.repeat(np.exp(log_g[:, None] * (pos + 1.0)).T, RET_DV, axis=1)
    k_decay = np.repeat(np.exp(log_g[:, None] * (CHUNK - 1.0 - pos)).T, RET_DK, axis=1)
    chunk_decay = np.repeat(np.exp(log_g * CHUNK), RET_DV)[None, :]
    block_diag = (np.arange(RET_QK_W)[:, None] // RET_DK == np.arange(RET_V_W)[None, :] // RET_DV)
    return tuple(jnp.asarray(t, _F32) for t in (inner, q_decay, k_decay, chunk_decay, block_diag))


def _retention_tile(p, o, inner_ref, qd_ref, kd_ref, cd_ref, bd_ref, state_ref):
    head_of_lane = lax.broadcasted_iota(jnp.int32, (CHUNK, RET_QK_W), 1) // RET_DK
    for ci in range(N_CHUNKS):
        rows = slice(ci * CHUNK, (ci + 1) * CHUNK)
        q = p[rows, 0:RET_QK_W]
        k = p[rows, RET_QK_W:2 * RET_QK_W].astype(_F32) * (RET_DK ** -0.5)
        k_bf = k.astype(_BF16)
        v = p[rows, 2 * RET_QK_W:2 * RET_QK_W + RET_V_W]
        state = state_ref[...]
        cross = _dot(q, state.astype(_BF16)) * qd_ref[...]
        update = _dot_tn((k * kd_ref[...]).astype(_BF16), v)
        state_ref[...] = state * cd_ref[...] + update * bd_ref[...]
        for hd in range(RET_HEADS):
            lanes = slice(hd * RET_DV, (hd + 1) * RET_DV)
            q_h = jnp.where(head_of_lane == hd, q, jnp.zeros_like(q))
            scores = _dot_nt(q_h, k_bf) * inner_ref[hd]
            out = _dot(scores.astype(_BF16), v[:, lanes]) + cross[:, lanes]
            out = out * lax.rsqrt(jnp.mean(out * out, axis=-1, keepdims=True) + EPS)
            g0 = 2 * RET_QK_W + RET_V_W + hd * RET_DV
            gate = p[rows, g0:g0 + RET_DV].astype(_F32)
            o[rows, lanes] = (out * _silu(gate)).astype(_BF16)


def _swa_bias():
    slopes = 2.0 ** (-8.0 * np.arange(1, SWA_HQ + 1, dtype=np.float64) / SWA_HQ)
    qi = np.arange(CHUNK)
    kj = np.arange(2 * CHUNK)
    delta = CHUNK + qi[:, None] - kj[None, :]
    valid = (delta >= 0) & (delta < CHUNK)
    variants = []
    for ok in (valid, valid & (kj[None, :] >= CHUNK)):
        bias = np.where(ok[None], -slopes[:, None, None] * delta[None].astype(np.float64), -np.inf)
        variants.append(bias.reshape(SWA_HKV, SWA_GROUP * CHUNK, 2 * CHUNK))
    return jnp.asarray(np.stack(variants) * LOG2_E, _F32)


def _swa_sink_rows(sinks):
    rows = jnp.repeat(sinks.astype(_F32) * LOG2_E, CHUNK)[:, None]
    return jnp.broadcast_to(rows, (SWA_HQ * CHUNK, LANES)).reshape(SWA_HKV, SWA_GROUP * CHUNK, LANES)


def _swa_tile(first_tile, p, o, bias_ref, sink_ref, kt_buf, v_buf):
    low_half = lax.broadcasted_iota(jnp.int32, (ROW_TILE, SWA_KV_W), 1) < SWA_D
    k = p[:, SWA_Q_W:SWA_Q_W + SWA_KV_W].astype(_F32)
    k_swapped = pltpu.roll(k, SWA_D, axis=1)
    for j, dup in enumerate((jnp.where(low_half, k, k_swapped), jnp.where(low_half, k_swapped, k))):
        for bi in range(N_CHUNKS):
            kt_buf[j, :, (bi + 1) * CHUNK:(bi + 2) * CHUNK] = dup[bi * CHUNK:(bi + 1) * CHUNK, :].T.astype(_BF16)
    v = p[:, SWA_Q_W + SWA_KV_W:SWA_Q_W + 2 * SWA_KV_W].astype(_F32)
    v_swapped = pltpu.roll(v, SWA_D, axis=1)
    for j, dup in enumerate((jnp.where(low_half, v, v_swapped), jnp.where(low_half, v_swapped, v))):
        v_buf[CHUNK:CHUNK + ROW_TILE, j * SWA_EXT_W:j * SWA_EXT_W + LANES] = dup.astype(_BF16)
        v_buf[CHUNK:CHUNK + ROW_TILE, j * SWA_EXT_W + LANES:(j + 1) * SWA_EXT_W] = jnp.ones((ROW_TILE, LANES), _BF16)

    low_lanes = lax.broadcasted_iota(jnp.int32, (CHUNK, LANES), 1) < SWA_D
    high_lanes = jnp.logical_not(low_lanes)
    first_variant = first_tile.astype(jnp.int32)
    gate0 = SWA_Q_W + 2 * SWA_KV_W

    for bi in range(N_CHUNKS):
        rows = slice(bi * CHUNK, (bi + 1) * CHUNK)
        keys = slice(bi * CHUNK, (bi + 2) * CHUNK)
        variant = first_variant if bi == 0 else 0
        for j in range(SWA_HKV):
            k_t = kt_buf[j, :, keys]
            v_ext = v_buf[keys, j * SWA_EXT_W:(j + 1) * SWA_EXT_W]
            for pair in range(SWA_GROUP // 2):
                c0 = (j * SWA_GROUP // 2 + pair) * LANES
                qp = p[rows, c0:c0 + LANES].astype(_F32) * (SWA_D ** -0.5 * LOG2_E)
                halves = []
                for odd in range(2):
                    g_rows = slice((2 * pair + odd) * CHUNK, (2 * pair + odd + 1) * CHUNK)
                    q_h = jnp.where(high_lanes if odd else low_lanes, qp, 0.0).astype(_BF16)
                    s = _dot(q_h, k_t) + bias_ref[variant, j, g_rows, :]
                    s_prev, s_cur = s[:, 0:CHUNK], s[:, CHUNK:2 * CHUNK]
                    sink = sink_ref[j, g_rows, :]
                    m = jnp.maximum(jnp.max(jnp.maximum(s_prev, s_cur), axis=-1, keepdims=True), sink)
                    e = jnp.concatenate([jnp.exp2(s_prev - m), jnp.exp2(s_cur - m)], axis=1).astype(_BF16)
                    ov = _dot(e, v_ext)
                    halves.append(ov[:, 0:LANES] / (ov[:, LANES:2 * LANES] + jnp.exp2(sink - m)))
                gate = p[rows, gate0 + c0:gate0 + c0 + LANES].astype(_F32)
                o[rows, c0:c0 + LANES] = (jnp.where(low_lanes, halves[0], halves[1]) * _silu(gate)).astype(_BF16)

    kt_buf[:, :, 0:CHUNK] = kt_buf[:, :, ROW_TILE:ROW_TILE + CHUNK]
    v_buf[0:CHUNK, :] = v_buf[ROW_TILE:ROW_TILE + CHUNK, :]


def _pool_inverse_counts():
    pos = np.arange(ROW_TILE, dtype=np.float64)[:, None]
    windows = np.repeat(np.asarray(POOL_WINDOWS, np.float64), POOL_GD)[None, :]
    steady = np.broadcast_to(1.0 / windows, (ROW_TILE, BRANCH))
    start = 1.0 / np.minimum(pos + 1.0, windows)
    return jnp.asarray(np.stack([steady, start]), _F32)


def _pool_tile(first_tile, p, o, inv_ref, w_ref, scale_ref, hist, levels):
    inv = inv_ref[first_tile.astype(jnp.int32)]
    ext_rows = POOL_EXT + ROW_TILE
    base = CHUNK - POOL_EXT
    sums = []
    for g, w in enumerate(POOL_WINDOWS):
        c0 = g * POOL_GD
        shift = w // 2
        if g == 0:
            total = hist[base:base + ext_rows, :] + hist[base - shift:base - shift + ext_rows, :]
        else:
            prev = levels[g - 1]
            total = (prev[POOL_PAD:POOL_PAD + ext_rows, c0:]
                     + prev[POOL_PAD - shift:POOL_PAD - shift + ext_rows, c0:])
        if g + 1 < len(POOL_WINDOWS):
            levels[g][POOL_PAD:POOL_PAD + ext_rows, c0:] = total
        sums.append(total)
    for g in range(len(POOL_WINDOWS)):
        lanes = slice(g * POOL_GD, (g + 1) * POOL_GD)
        u = hist[CHUNK:CHUNK + ROW_TILE, lanes]
        acc = sums[g][POOL_EXT:, 0:POOL_GD]
        diff = acc * inv[:, lanes] - u
        y = _dot(diff.astype(_BF16), w_ref[g])
        gate = p[:, lanes].astype(_F32)
        o[:, lanes] = (y * scale_ref[:, lanes] * _silu(gate)).astype(_BF16)
    hist[0:CHUNK, :] = hist[ROW_TILE:ROW_TILE + CHUNK, :]


def _expansion_matrix():
    e = np.zeros((2 * LANES, M_X_W), np.float32)
    for hd in range(M_HEADS):
        e[hd, hd * M_P:(hd + 1) * M_P] = 1.0
        e[LANES + hd, hd * M_P:(hd + 1) * M_P] = 1.0
    return jnp.asarray(e, _BF16)


def _pad_lanes(v, width=DT_PAD):
    return jnp.pad(v.astype(_F32), (0, width - v.shape[0])).reshape(1, width)


def _ssd_tile(p, dt_raw, o, cw_ref, cb_ref, dtb_ref, alog_ref, dskip_ref, nw_ref, tri_ref, exp_ref,
              hist, xbc_buf, state_ref):
    acc = cb_ref[...] + cw_ref[M_CONV - 1:M_CONV, :] * hist[CONV_HIST:CONV_HIST + ROW_TILE, :]
    for j in range(1, M_CONV):
        acc = acc + cw_ref[M_CONV - 1 - j:M_CONV - j, :] * hist[CONV_HIST - j:CONV_HIST - j + ROW_TILE, :]
    xbc_buf[...] = _silu(acc)
    hist[0:CONV_HIST, :] = hist[ROW_TILE:ROW_TILE + CONV_HIST, :]

    a_row = -jnp.exp(alog_ref[...])
    causal = (lax.broadcasted_iota(jnp.int32, (CHUNK, CHUNK), 0)
              >= lax.broadcasted_iota(jnp.int32, (CHUNK, CHUNK), 1))
    head_of_lane = lax.broadcasted_iota(jnp.int32, (CHUNK, M_GW), 1) // M_P

    def expand(vals):
        stacked = jnp.concatenate(vals, axis=0)
        hi = stacked.astype(_BF16)
        lo = (stacked - hi.astype(_F32)).astype(_BF16)
        wide = _dot(jnp.concatenate([hi, lo], axis=1), exp_ref[...])
        return [wide[i * CHUNK:(i + 1) * CHUNK, :] for i in range(len(vals))]

    for ci in range(N_CHUNKS):
        rows = slice(ci * CHUNK, (ci + 1) * CHUNK)
        dt = _softplus(dt_raw[rows, :] + dtb_ref[...])
        acs = jnp.dot(tri_ref[...], dt * a_row, preferred_element_type=_F32, precision=lax.Precision.HIGHEST)
        acs_t = acs.T
        last = acs[CHUNK - 1:CHUNK, :]
        dt_x, w_x, ex_x = expand([dt, dt * jnp.exp(last - acs), jnp.exp(acs)])
        x = xbc_buf[rows, 0:M_X_W]
        xdt = (x * dt_x).astype(_BF16)
        xd = (x * w_x).astype(_BF16)
        for g in range(M_GROUPS):
            lanes = slice(g * M_GW, (g + 1) * M_GW)
            bmat = xbc_buf[rows, M_X_W + g * M_N:M_X_W + (g + 1) * M_N].astype(_BF16)
            cmat = xbc_buf[rows, M_X_W + M_BC + g * M_N:M_X_W + M_BC + (g + 1) * M_N].astype(_BF16)
            cb = _dot_nt(cmat, bmat)
            state = state_ref[g]
            ex_g = ex_x[:, lanes]
            y = _dot(cmat, state.astype(_BF16)) * ex_g + x[:, lanes] * dskip_ref[:, lanes]
            xdt_g = xdt[:, lanes]
            for r in range(M_R):
                hd = g * M_R + r
                seg = acs[:, hd:hd + 1] - acs_t[hd:hd + 1, :]
                lmat = jnp.exp(jnp.where(causal, seg, -jnp.inf))
                x_h = jnp.where(head_of_lane == r, xdt_g, jnp.zeros_like(xdt_g))
                y = y + _dot((cb * lmat).astype(_BF16), x_h)
            state_ref[g] = state * ex_g[CHUNK - 1:CHUNK, :] + _dot_tn(bmat, xd[:, lanes])
            z = p[rows, lanes].astype(_F32)
            yz = y * _silu(z)
            yz = yz * lax.rsqrt(jnp.mean(yz * yz, axis=-1, keepdims=True) + EPS)
            o[rows, lanes] = (yz * nw_ref[:, lanes]).astype(_BF16)


N_RET_T, N_SWA_T, N_POOL_T, N_SSD_T = 5, 2, 3, 8


def _mixers_kernel(x_ref, mod_ref, nw_ref, wr_ref, ws_ref, wp_ref, wm_ref, wdt_ref, *rest):
    consts, rest = rest[:N_RET_T + N_SWA_T + N_POOL_T + N_SSD_T], rest[N_RET_T + N_SWA_T + N_POOL_T + N_SSD_T:]
    ret_c, consts = consts[:N_RET_T], consts[N_RET_T:]
    swa_c, consts = consts[:N_SWA_T], consts[N_SWA_T:]
    pool_c, ssd_c = consts[:N_POOL_T], consts[N_POOL_T:]
    ret_o, att_o, pool_o, ssm_o, h_o = (r.at[0] for r in rest[:N_BRANCHES + 1])
    proj = rest[N_BRANCHES + 1:N_BRANCHES + 6]
    ret_state, kt_buf, v_buf, pool_hist, conv_hist, xbc_buf, ssd_state = rest[N_BRANCHES + 6:N_BRANCHES + 13]
    pool_levels = rest[N_BRANCHES + 13:]
    first_tile = pl.program_id(1) == 0

    @pl.when(first_tile)
    def _():
        ret_state[...] = jnp.zeros_like(ret_state)
        ssd_state[...] = jnp.zeros_like(ssd_state)
        kt_buf[:, :, 0:CHUNK] = jnp.zeros((SWA_HKV, LANES, CHUNK), _BF16)
        v_buf[0:CHUNK, :] = jnp.zeros((CHUNK, SWA_HKV * SWA_EXT_W), _BF16)
        pool_hist[0:CHUNK, :] = jnp.zeros((CHUNK, BRANCH), _F32)
        for level in pool_levels:
            level[0:POOL_PAD, :] = jnp.zeros((POOL_PAD, BRANCH), _F32)
        conv_hist[0:CONV_HIST, :] = jnp.zeros((CONV_HIST, M_CONV_W), _F32)

    h = _modulated_norm(x_ref[0], mod_ref, nw_ref).astype(_BF16)
    h_o[...] = h
    ret_p, swa_p, pool_p, ssd_p, dt_p = proj
    ret_p[...] = _dot(h, wr_ref[0]).astype(_BF16)
    swa_p[...] = _dot(h, ws_ref[0]).astype(_BF16)
    full = _dot(h, wp_ref[0])
    pool_hist[CHUNK:CHUNK + ROW_TILE, :] = full[:, 0:BRANCH]
    pool_p[...] = full[:, BRANCH:].astype(_BF16)
    full = _dot(h, wm_ref[0])
    conv_hist[CONV_HIST:CONV_HIST + ROW_TILE, :] = full[:, 0:M_CONV_W]
    ssd_p[...] = full[:, M_CONV_W:].astype(_BF16)
    dt_p[...] = _dot(h, wdt_ref[...])
    _retention_tile(ret_p, ret_o, *ret_c, ret_state)
    _swa_tile(first_tile, swa_p, att_o, *swa_c, kt_buf, v_buf)
    _pool_tile(first_tile, pool_p, pool_o, *pool_c, pool_hist, pool_levels)
    _ssd_tile(ssd_p, dt_p, ssm_o, *ssd_c, conv_hist, xbc_buf, ssd_state)


def _mixers(layer, x, mod, norm_w, w_in, w_dt, swa_sinks, pool_w, pool_scale, conv_w, conv_b, dt_bias, a_log,
            d_skip, ssm_norm_w):
    batch, seq, _ = x.shape
    windows = (RET_COLS, SWA_COLS, POOL_COLS, SSD_COLS)
    consts = list(_retention_tables())
    consts += [_swa_bias(), _swa_sink_rows(swa_sinks)]
    consts += [_pool_inverse_counts(), pool_w.astype(_BF16), pool_scale.reshape(1, BRANCH)]
    consts += [conv_w, conv_b.reshape(1, M_CONV_W), _pad_lanes(dt_bias), _pad_lanes(a_log),
               jnp.repeat(d_skip.astype(_F32), M_P).reshape(1, BRANCH), ssm_norm_w.reshape(1, BRANCH),
               jnp.asarray(np.tril(np.ones((CHUNK, CHUNK), np.float32))), _expansion_matrix()]
    assert len(consts) == N_RET_T + N_SWA_T + N_POOL_T + N_SSD_T
    operands = [x, mod, norm_w] + [w_in] * len(windows) + [w_dt, *consts]
    in_specs = [_row_spec(D_MODEL), _mod_spec(), _const_spec(norm_w.shape)]
    in_specs += [_weight_window_spec(layer, cols) for cols in windows]
    in_specs += [_const_spec(a.shape) for a in [w_dt, *consts]]
    scratch = [pltpu.VMEM((ROW_TILE, w), _BF16) for w in MIXER_WIDTHS] + [pltpu.VMEM((ROW_TILE, DT_PAD), _F32)]
    scratch += [pltpu.VMEM((RET_QK_W, RET_V_W), _F32),
                pltpu.VMEM((SWA_HKV, LANES, CHUNK + ROW_TILE), _BF16),
                pltpu.VMEM((CHUNK + ROW_TILE, SWA_HKV * SWA_EXT_W), _BF16),
                pltpu.VMEM((CHUNK + ROW_TILE, BRANCH), _F32),
                pltpu.VMEM((CONV_HIST + ROW_TILE, M_CONV_W), _F32),
                pltpu.VMEM((ROW_TILE, M_CONV_W), _F32),
                pltpu.VMEM((M_GROUPS, M_N, M_GW), _F32)]
    scratch += [pltpu.VMEM((POOL_PAD + POOL_EXT + ROW_TILE, BRANCH), _F32)] * (len(POOL_WINDOWS) - 1)
    return pl.pallas_call(
        _mixers_kernel,
        grid=(batch, seq // ROW_TILE),
        in_specs=in_specs,
        out_specs=[_row_spec(BRANCH)] * N_BRANCHES + [_row_spec(D_MODEL)],
        out_shape=[jax.ShapeDtypeStruct((batch, seq, BRANCH), _BF16)] * N_BRANCHES
        + [jax.ShapeDtypeStruct((batch, seq, D_MODEL), _BF16)],
        scratch_shapes=scratch,
        compiler_params=_params(),
        name="norm_proj_mixers",
    )(*operands)


def _merge_kernel(final, x_ref, mod_ref, h_ref, wg_ref, ret_ref, att_ref, pool_ref, ssm_ref, wup_ref, wout_ref,
                  *rest):
    if final:
        fnw_ref, o_ref = rest
    else:
        (o_ref,) = rest
    h = h_ref[0]
    merged = None
    for i, br_ref in enumerate((ret_ref, att_ref, pool_ref, ssm_ref)):
        gate = _sigmoid(_dot(h, wg_ref[:, i * D_MODEL:(i + 1) * D_MODEL]))
        term = gate * _dot(br_ref[0], wup_ref[i])
        merged = term if merged is None else merged + term
    out = _dot(merged.astype(_BF16), wout_ref[...])
    y = x_ref[0] + mod_ref[0, :, 2 * D_MODEL:3 * D_MODEL] * out
    if final:
        y = y * lax.rsqrt(jnp.mean(y * y, axis=-1, keepdims=True) + EPS) * fnw_ref[...]
    o_ref[0] = y


def _merge(x, mod, h, wg, branches, w_up, w_out, final_norm_w):
    batch, seq, _ = x.shape
    final = final_norm_w is not None
    in_specs = [_row_spec(D_MODEL, MERGE_TILE), _mod_spec(), _row_spec(D_MODEL, MERGE_TILE), _const_spec(wg.shape)]
    in_specs += [_row_spec(BRANCH, MERGE_TILE)] * N_BRANCHES
    in_specs += [_const_spec(w_up.shape), _const_spec(w_out.shape)]
    args = [x, mod, h, wg, *branches, w_up, w_out]
    if final:
        in_specs.append(_const_spec((1, D_MODEL)))
        args.append(final_norm_w.reshape(1, D_MODEL))
    return pl.pallas_call(
        functools.partial(_merge_kernel, final),
        grid=(batch, seq // MERGE_TILE),
        in_specs=in_specs,
        out_specs=_row_spec(D_MODEL, MERGE_TILE),
        out_shape=jax.ShapeDtypeStruct(x.shape, _F32),
        compiler_params=_params(),
        name="merge_out_proj",
    )(*args)


def kernel(x, c, ada_w, ada_b, norm_w, w_in, swa_sinks, pool_w, pool_scale, conv_w, conv_b, dt_bias, a_log,
           d_skip, ssm_norm_w, w_up, w_out, final_norm_w):
    depth = ada_w.shape[0]
    batch = x.shape[0]
    mod_all = _modulation(c, ada_w, ada_b)
    w_in_bf = w_in.astype(_BF16)
    for l in range(depth):
        mod = mod_all[l].reshape(batch, 1, 3 * D_MODEL)
        nw = norm_w[l].reshape(1, D_MODEL)
        w_dt = jnp.pad(w_in_bf[l, :, DT_COLS[0]:DT_COLS[1]], ((0, 0), (0, DT_PAD - M_HEADS)))
        *branches, h = _mixers(l, x, mod, nw, w_in_bf, w_dt, swa_sinks[l], pool_w[l], pool_scale[l], conv_w[l],
                               conv_b[l], dt_bias[l], a_log[l], d_skip[l], ssm_norm_w[l])
        wg = w_in_bf[l, :, MG_COLS[0]:MG_COLS[1]]
        x = _merge(x, mod, h, wg, branches, w_up[l].astype(_BF16), w_out[l].astype(_BF16),
                   final_norm_w if l == depth - 1 else None)
    return x
```

```python
import functools

import jax
import jax.numpy as jnp
import numpy as np
from jax import lax
from jax.experimental import pallas as pl
from jax.experimental.pallas import tpu as pltpu

D_MODEL = 1024
BRANCH = 512
N_BRANCHES = 4
EPS = 1e-6
CHUNK = 128
LANES = 128

RET_HEADS, RET_DK, RET_DV = 4, 64, 128
RET_QK_W = RET_HEADS * RET_DK
RET_V_W = RET_HEADS * RET_DV
SWA_HQ, SWA_HKV, SWA_D = 8, 2, 64
SWA_GROUP = SWA_HQ // SWA_HKV
SWA_Q_W = SWA_HQ * SWA_D
SWA_KV_W = SWA_HKV * SWA_D
SWA_EXT_W = 2 * LANES
LOG2_E = 1.4426950408889634
POOL_WINDOWS = (2, 4, 8, 16)
POOL_GD = BRANCH // len(POOL_WINDOWS)
POOL_EXT = 16
POOL_PAD = 8
M_HEADS, M_P, M_GROUPS, M_N, M_CONV = 8, 64, 2, 128, 4
M_R = M_HEADS // M_GROUPS
M_GW = M_R * M_P
M_BC = M_GROUPS * M_N
M_X_W = M_HEADS * M_P
M_CONV_W = M_X_W + 2 * M_BC
CONV_HIST = 8
DT_PAD = LANES

_SIZES = (256, 256, 512, 512, 512, 128, 128, 512, 512, 512, 1024, 512, M_HEADS, N_BRANCHES * D_MODEL)
_OFFS = np.concatenate([[0], np.cumsum(_SIZES)]).tolist()
RET_COLS = (_OFFS[0], _OFFS[4])
SWA_COLS = (_OFFS[4], _OFFS[8])
POOL_COLS = (_OFFS[8], _OFFS[10])
SSD_COLS = (_OFFS[10], _OFFS[12])
DT_COLS = (_OFFS[12], _OFFS[13])
MG_COLS = (_OFFS[13], _OFFS[14])
MIXER_WIDTHS = tuple(b - a for a, b in (RET_COLS, SWA_COLS, POOL_COLS, SSD_COLS))

ROW_TILE = 512
MERGE_TILE = 1024
N_CHUNKS = ROW_TILE // CHUNK
V7X_VMEM_LIMIT = 56 * 1024 * 1024

_F32 = jnp.float32
_BF16 = jnp.bfloat16


def _sigmoid(x):
    return 0.5 * jnp.tanh(0.5 * x) + 0.5


def _silu(x):
    half = 0.5 * x
    return half + half * jnp.tanh(half)


def _softplus(x):
    return jnp.maximum(x, 0.0) + jnp.log(1.0 + jnp.exp(-jnp.abs(x)))


def _dot(a, b):
    return jnp.dot(a, b, preferred_element_type=_F32)


def _dot_nt(a, b):
    return lax.dot_general(a, b, (((1,), (1,)), ((), ())), preferred_element_type=_F32)


def _dot_tn(a, b):
    return lax.dot_general(a, b, (((0,), (0,)), ((), ())), preferred_element_type=_F32)


def _const_spec(shape):
    nd = len(shape)
    return pl.BlockSpec(shape, lambda *_: (0,) * nd, pipeline_mode=pl.Buffered(1))


def _weight_window_spec(layer, cols):
    return pl.BlockSpec((pl.Element(1), pl.Element(D_MODEL), pl.Element(cols[1] - cols[0])),
                        lambda *_: (layer, 0, cols[0]), pipeline_mode=pl.Buffered(1))


def _row_spec(width, tile=None):
    return pl.BlockSpec((1, tile or ROW_TILE, width), lambda b, t: (b, t, 0))


def _mod_spec():
    return pl.BlockSpec((1, 1, 3 * D_MODEL), lambda b, t: (b, 0, 0))


def _params():
    return pltpu.CompilerParams(dimension_semantics=("parallel", "arbitrary"),
                                vmem_limit_bytes=V7X_VMEM_LIMIT)


def _modulated_norm(x, mod_ref, nw_ref):
    shift = mod_ref[0, :, 0:D_MODEL]
    scale = mod_ref[0, :, D_MODEL:2 * D_MODEL]
    rs = lax.rsqrt(jnp.mean(x * x, axis=-1, keepdims=True) + EPS)
    return (x * rs) * nw_ref[...] * (1.0 + scale) + shift


def _mod_kernel(c_ref, w_ref, b_ref, o_ref):
    sc = _silu(c_ref[...]).astype(_BF16)
    o_ref[0] = _dot(sc, w_ref[0].astype(_BF16)) + b_ref[0]


def _modulation(c, ada_w, ada_b):
    depth, _, n = ada_w.shape
    bn = D_MODEL
    batch = c.shape[0]
    return pl.pallas_call(
        _mod_kernel,
        grid=(depth, n // bn),
        in_specs=[pl.BlockSpec((batch, D_MODEL), lambda l, j: (0, 0)),
                  pl.BlockSpec((1, D_MODEL, bn), lambda l, j: (l, 0, j)),
                  pl.BlockSpec((1, 1, bn), lambda l, j: (l, 0, j))],
        out_specs=pl.BlockSpec((1, batch, bn), lambda l, j: (l, 0, j)),
        out_shape=jax.ShapeDtypeStruct((depth, batch, n), _F32),
        name="adaln_modulation",
    )(c, ada_w, ada_b.reshape(depth, 1, n))


def _retention_tables():
    log_g = np.log(1.0 - 2.0 ** (-5.0 - np.arange(RET_HEADS, dtype=np.float64)))
    pos = np.arange(CHUNK, dtype=np.float64)
    diff = pos[:, None] - pos[None, :]
    inner = np.where(diff >= 0, np.exp(log_g[:, None, None] * np.where(diff >= 0, diff, 0.0)), 0.0)
    q_decay = np.repeat(np.exp(log_g[:, None] * (pos + 1.0)).T, RET_DV, axis=1)
    k_decay = np.repeat(np.exp(log_g[:, None] * (CHUNK - 1.0 - pos)).T, RET_DK, axis=1)
    chunk_decay = np.repeat(np.exp(log_g * CHUNK), RET_DV)[None, :]
    block_diag = (np.arange(RET_QK_W)[:, None] // RET_DK == np.arange(RET_V_W)[None, :] // RET_DV)
    return tuple(jnp.asarray(t, _F32) for t in (inner, q_decay, k_decay, chunk_decay, block_diag))


def _retention_tile(p, o, inner_ref, qd_ref, kd_ref, cd_ref, bd_ref, state_ref):
    head_of_lane = lax.broadcasted_iota(jnp.int32, (CHUNK, RET_QK_W), 1) // RET_DK
    for ci in range(N_CHUNKS):
        rows = slice(ci * CHUNK, (ci + 1) * CHUNK)
        q = p[rows, 0:RET_QK_W]
        k = p[rows, RET_QK_W:2 * RET_QK_W].astype(_F32) * (RET_DK ** -0.5)
        k_bf = k.astype(_BF16)
        v = p[rows, 2 * RET_QK_W:2 * RET_QK_W + RET_V_W]
        state = state_ref[...]
        cross = _dot(q, state.astype(_BF16)) * qd_ref[...]
        update = _dot_tn((k * kd_ref[...]).astype(_BF16), v)
        state_ref[...] = state * cd_ref[...] + update * bd_ref[...]
        for hd in range(RET_HEADS):
            lanes = slice(hd * RET_DV, (hd + 1) * RET_DV)
            q_h = jnp.where(head_of_lane == hd, q, jnp.zeros_like(q))
            scores = _dot_nt(q_h, k_bf) * inner_ref[hd]
            out = _dot(scores.astype(_BF16), v[:, lanes]) + cross[:, lanes]
            out = out * lax.rsqrt(jnp.mean(out * out, axis=-1, keepdims=True) + EPS)
            g0 = 2 * RET_QK_W + RET_V_W + hd * RET_DV
            gate = p[rows, g0:g0 + RET_DV].astype(_F32)
            o[rows, lanes] = (out * _silu(gate)).astype(_BF16)


def _swa_bias():
    slopes = 2.0 ** (-8.0 * np.arange(1, SWA_HQ + 1, dtype=np.float64) / SWA_HQ)
    qi = np.arange(CHUNK)
    kj = np.arange(2 * CHUNK)
    delta = CHUNK + qi[:, None] - kj[None, :]
    valid = (delta >= 0) & (delta < CHUNK)
    variants = []
    for ok in (valid, valid & (kj[None, :] >= CHUNK)):
        bias = np.where(ok[None], -slopes[:, None, None] * delta[None].astype(np.float64), -np.inf)
        variants.append(bias.reshape(SWA_HKV, SWA_GROUP * CHUNK, 2 * CHUNK))
    return jnp.asarray(np.stack(variants) * LOG2_E, _F32)


def _swa_sink_rows(sinks):
    rows = jnp.repeat(sinks.astype(_F32) * LOG2_E, CHUNK)[:, None]
    return jnp.broadcast_to(rows, (SWA_HQ * CHUNK, LANES)).reshape(SWA_HKV, SWA_GROUP * CHUNK, LANES)


def _swa_tile(first_tile, p, o, bias_ref, sink_ref, kt_buf, v_buf):
    low_half = lax.broadcasted_iota(jnp.int32, (ROW_TILE, SWA_KV_W), 1) < SWA_D
    k = p[:, SWA_Q_W:SWA_Q_W + SWA_KV_W].astype(_F32)
    k_swapped = pltpu.roll(k, SWA_D, axis=1)
    for j, dup in enumerate((jnp.where(low_half, k, k_swapped), jnp.where(low_half, k_swapped, k))):
        for bi in range(N_CHUNKS):
            kt_buf[j, :, (bi + 1) * CHUNK:(bi + 2) * CHUNK] = dup[bi * CHUNK:(bi + 1) * CHUNK, :].T.astype(_BF16)
    v = p[:, SWA_Q_W + SWA_KV_W:SWA_Q_W + 2 * SWA_KV_W].astype(_F32)
    v_swapped = pltpu.roll(v, SWA_D, axis=1)
    for j, dup in enumerate((jnp.where(low_half, v, v_swapped), jnp.where(low_half, v_swapped, v))):
        v_buf[CHUNK:CHUNK + ROW_TILE, j * SWA_EXT_W:j * SWA_EXT_W + LANES] = dup.astype(_BF16)
        v_buf[CHUNK:CHUNK + ROW_TILE, j * SWA_EXT_W + LANES:(j + 1) * SWA_EXT_W] = jnp.ones((ROW_TILE, LANES), _BF16)

    low_lanes = lax.broadcasted_iota(jnp.int32, (CHUNK, LANES), 1) < SWA_D
    high_lanes = jnp.logical_not(low_lanes)
    first_variant = first_tile.astype(jnp.int32)
    gate0 = SWA_Q_W + 2 * SWA_KV_W

    for bi in range(N_CHUNKS):
        rows = slice(bi * CHUNK, (bi + 1) * CHUNK)
        keys = slice(bi * CHUNK, (bi + 2) * CHUNK)
        variant = first_variant if bi == 0 else 0
        for j in range(SWA_HKV):
            k_t = kt_buf[j, :, keys]
            v_ext = v_buf[keys, j * SWA_EXT_W:(j + 1) * SWA_EXT_W]
            for pair in range(SWA_GROUP // 2):
                c0 = (j * SWA_GROUP // 2 + pair) * LANES
                qp = p[rows, c0:c0 + LANES].astype(_F32) * (SWA_D ** -0.5 * LOG2_E)
                halves = []
                for odd in range(2):
                    g_rows = slice((2 * pair + odd) * CHUNK, (2 * pair + odd + 1) * CHUNK)
                    q_h = jnp.where(high_lanes if odd else low_lanes, qp, 0.0).astype(_BF16)
                    s = _dot(q_h, k_t) + bias_ref[variant, j, g_rows, :]
                    s_prev, s_cur = s[:, 0:CHUNK], s[:, CHUNK:2 * CHUNK]
                    sink = sink_ref[j, g_rows, :]
                    m = jnp.maximum(jnp.max(jnp.maximum(s_prev, s_cur), axis=-1, keepdims=True), sink)
                    e = jnp.concatenate([jnp.exp2(s_prev - m), jnp.exp2(s_cur - m)], axis=1).astype(_BF16)
                    ov = _dot(e, v_ext)
                    halves.append(ov[:, 0:LANES] / (ov[:, LANES:2 * LANES] + jnp.exp2(sink - m)))
                gate = p[rows, gate0 + c0:gate0 + c0 + LANES].astype(_F32)
                o[rows, c0:c0 + LANES] = (jnp.where(low_lanes, halves[0], halves[1]) * _silu(gate)).astype(_BF16)

    kt_buf[:, :, 0:CHUNK] = kt_buf[:, :, ROW_TILE:ROW_TILE + CHUNK]
    v_buf[0:CHUNK, :] = v_buf[ROW_TILE:ROW_TILE + CHUNK, :]


def _pool_inverse_counts():
    pos = np.arange(ROW_TILE, dtype=np.float64)[:, None]
    windows = np.repeat(np.asarray(POOL_WINDOWS, np.float64), POOL_GD)[None, :]
    steady = np.broadcast_to(1.0 / windows, (ROW_TILE, BRANCH))
    start = 1.0 / np.minimum(pos + 1.0, windows)
    return jnp.asarray(np.stack([steady, start]), _F32)


def _pool_tile(first_tile, p, o, inv_ref, w_ref, scale_ref, hist, levels):
    hist[CHUNK:CHUNK + ROW_TILE, :] = p[:, 0:BRANCH].astype(_F32)
    inv = inv_ref[first_tile.astype(jnp.int32)]
    ext_rows = POOL_EXT + ROW_TILE
    base = CHUNK - POOL_EXT
    sums = []
    for g, w in enumerate(POOL_WINDOWS):
        c0 = g * POOL_GD
        shift = w // 2
        if g == 0:
            total = hist[base:base + ext_rows, :] + hist[base - shift:base - shift + ext_rows, :]
        else:
            prev = levels[g - 1]
            total = (prev[POOL_PAD:POOL_PAD + ext_rows, c0:]
                     + prev[POOL_PAD - shift:POOL_PAD - shift + ext_rows, c0:])
        if g + 1 < len(POOL_WINDOWS):
            levels[g][POOL_PAD:POOL_PAD + ext_rows, c0:] = total
        sums.append(total)
    for g in range(len(POOL_WINDOWS)):
        lanes = slice(g * POOL_GD, (g + 1) * POOL_GD)
        u = hist[CHUNK:CHUNK + ROW_TILE, lanes]
        acc = sums[g][POOL_EXT:, 0:POOL_GD]
        diff = acc * inv[:, lanes] - u
        y = _dot(diff.astype(_BF16), w_ref[g])
        gate = p[:, BRANCH + g * POOL_GD:BRANCH + (g + 1) * POOL_GD].astype(_F32)
        o[:, lanes] = (y * scale_ref[:, lanes] * _silu(gate)).astype(_BF16)
    hist[0:CHUNK, :] = hist[ROW_TILE:ROW_TILE + CHUNK, :]


def _expansion_matrix():
    e = np.zeros((2 * LANES, M_X_W), np.float32)
    for hd in range(M_HEADS):
        e[hd, hd * M_P:(hd + 1) * M_P] = 1.0
        e[LANES + hd, hd * M_P:(hd + 1) * M_P] = 1.0
    return jnp.asarray(e, _BF16)


def _pad_lanes(v, width=DT_PAD):
    return jnp.pad(v.astype(_F32), (0, width - v.shape[0])).reshape(1, width)


def _ssd_tile(p, dt_raw, o, cw_ref, cb_ref, dtb_ref, alog_ref, dskip_ref, nw_ref, tri_ref, exp_ref,
              hist, xbc_buf, state_ref):
    hist[CONV_HIST:CONV_HIST + ROW_TILE, :] = p[:, 0:M_CONV_W].astype(_F32)
    acc = cb_ref[...] + cw_ref[M_CONV - 1:M_CONV, :] * hist[CONV_HIST:CONV_HIST + ROW_TILE, :]
    for j in range(1, M_CONV):
        acc = acc + cw_ref[M_CONV - 1 - j:M_CONV - j, :] * hist[CONV_HIST - j:CONV_HIST - j + ROW_TILE, :]
    xbc_buf[...] = _silu(acc)
    hist[0:CONV_HIST, :] = hist[ROW_TILE:ROW_TILE + CONV_HIST, :]

    a_row = -jnp.exp(alog_ref[...])
    causal = (lax.broadcasted_iota(jnp.int32, (CHUNK, CHUNK), 0)
              >= lax.broadcasted_iota(jnp.int32, (CHUNK, CHUNK), 1))
    head_of_lane = lax.broadcasted_iota(jnp.int32, (CHUNK, M_GW), 1) // M_P

    def expand(vals):
        stacked = jnp.concatenate(vals, axis=0)
        hi = stacked.astype(_BF16)
        lo = (stacked - hi.astype(_F32)).astype(_BF16)
        wide = _dot(jnp.concatenate([hi, lo], axis=1), exp_ref[...])
        return [wide[i * CHUNK:(i + 1) * CHUNK, :] for i in range(len(vals))]

    for ci in range(N_CHUNKS):
        rows = slice(ci * CHUNK, (ci + 1) * CHUNK)
        dt = _softplus(dt_raw[rows, :] + dtb_ref[...])
        dta = dt * a_row
        hi = dta.astype(_BF16)
        rest = dta - hi.astype(_F32)
        mid = rest.astype(_BF16)
        lo = (rest - mid.astype(_F32)).astype(_BF16)
        acs = _dot(tri_ref[...], jnp.concatenate([hi, mid, lo], axis=0))
        acs_t = acs.T
        dt_t = dt.T
        last = acs[CHUNK - 1:CHUNK, :]
        w_x, ex_x = expand([dt * jnp.exp(last - acs), jnp.exp(acs)])
        x = xbc_buf[rows, 0:M_X_W]
        x_bf = x.astype(_BF16)
        xd = (x * w_x).astype(_BF16)
        for g in range(M_GROUPS):
            lanes = slice(g * M_GW, (g + 1) * M_GW)
            bmat = xbc_buf[rows, M_X_W + g * M_N:M_X_W + (g + 1) * M_N].astype(_BF16)
            cmat = xbc_buf[rows, M_X_W + M_BC + g * M_N:M_X_W + M_BC + (g + 1) * M_N].astype(_BF16)
            cb = _dot_nt(cmat, bmat)
            state = state_ref[g]
            ex_g = ex_x[:, lanes]
            y = _dot(cmat, state.astype(_BF16)) * ex_g + x[:, lanes] * dskip_ref[:, lanes]
            x_g = x_bf[:, lanes]
            for r in range(M_R):
                hd = g * M_R + r
                seg = acs[:, hd:hd + 1] - acs_t[hd:hd + 1, :]
                lmat = jnp.exp(jnp.where(causal, seg, -jnp.inf)) * dt_t[hd:hd + 1, :]
                x_h = jnp.where(head_of_lane == r, x_g, jnp.zeros_like(x_g))
                y = y + _dot((cb * lmat).astype(_BF16), x_h)
            state_ref[g] = state * ex_g[CHUNK - 1:CHUNK, :] + _dot_tn(bmat, xd[:, lanes])
            z = p[rows, M_CONV_W + g * M_GW:M_CONV_W + (g + 1) * M_GW].astype(_F32)
            yz = y * _silu(z)
            yz = yz * lax.rsqrt(jnp.mean(yz * yz, axis=-1, keepdims=True) + EPS)
            o[rows, lanes] = (yz * nw_ref[:, lanes]).astype(_BF16)


N_RET_T, N_SWA_T, N_POOL_T, N_SSD_T = 5, 2, 3, 8


def _mixers_kernel(x_ref, mod_ref, nw_ref, wr_ref, ws_ref, wp_ref, wm_ref, wdt_ref, *rest):
    consts, rest = rest[:N_RET_T + N_SWA_T + N_POOL_T + N_SSD_T], rest[N_RET_T + N_SWA_T + N_POOL_T + N_SSD_T:]
    ret_c, consts = consts[:N_RET_T], consts[N_RET_T:]
    swa_c, consts = consts[:N_SWA_T], consts[N_SWA_T:]
    pool_c, ssd_c = consts[:N_POOL_T], consts[N_POOL_T:]
    ret_o, att_o, pool_o, ssm_o, h_o = (r.at[0] for r in rest[:N_BRANCHES + 1])
    proj = rest[N_BRANCHES + 1:N_BRANCHES + 6]
    ret_state, kt_buf, v_buf, pool_hist, conv_hist, xbc_buf, ssd_state = rest[N_BRANCHES + 6:N_BRANCHES + 13]
    pool_levels = rest[N_BRANCHES + 13:]
    first_tile = pl.program_id(1) == 0

    @pl.when(first_tile)
    def _():
        ret_state[...] = jnp.zeros_like(ret_state)
        ssd_state[...] = jnp.zeros_like(ssd_state)
        kt_buf[:, :, 0:CHUNK] = jnp.zeros((SWA_HKV, LANES, CHUNK), _BF16)
        v_buf[0:CHUNK, :] = jnp.zeros((CHUNK, SWA_HKV * SWA_EXT_W), _BF16)
        pool_hist[0:CHUNK, :] = jnp.zeros((CHUNK, BRANCH), _F32)
        for level in pool_levels:
            level[0:POOL_PAD, :] = jnp.zeros((POOL_PAD, BRANCH), _F32)
        conv_hist[0:CONV_HIST, :] = jnp.zeros((CONV_HIST, M_CONV_W), _F32)

    h = _modulated_norm(x_ref[0], mod_ref, nw_ref).astype(_BF16)
    h_o[...] = h
    for ref, w_ref in zip(proj, (wr_ref.at[0], ws_ref.at[0], wp_ref.at[0], wm_ref.at[0], wdt_ref)):
        ref[...] = _dot(h, w_ref[...]).astype(ref.dtype)

    ret_p, swa_p, pool_p, ssd_p, dt_p = proj
    _retention_tile(ret_p, ret_o, *ret_c, ret_state)
    _swa_tile(first_tile, swa_p, att_o, *swa_c, kt_buf, v_buf)
    _pool_tile(first_tile, pool_p, pool_o, *pool_c, pool_hist, pool_levels)
    _ssd_tile(ssd_p, dt_p, ssm_o, *ssd_c, conv_hist, xbc_buf, ssd_state)


def _mixers(layer, x, mod, norm_w, w_in, w_dt, swa_sinks, pool_w, pool_scale, conv_w, conv_b, dt_bias, a_log,
            d_skip, ssm_norm_w):
    batch, seq, _ = x.shape
    windows = (RET_COLS, SWA_COLS, POOL_COLS, SSD_COLS)
    consts = list(_retention_tables())
    consts += [_swa_bias(), _swa_sink_rows(swa_sinks)]
    consts += [_pool_inverse_counts(), pool_w.astype(_BF16), pool_scale.reshape(1, BRANCH)]
    consts += [conv_w, conv_b.reshape(1, M_CONV_W), _pad_lanes(dt_bias), _pad_lanes(a_log),
               jnp.repeat(d_skip.astype(_F32), M_P).reshape(1, BRANCH), ssm_norm_w.reshape(1, BRANCH),
               jnp.asarray(np.tile(np.tril(np.ones((CHUNK, CHUNK), np.float32)), (1, 3)), _BF16),
               _expansion_matrix()]
    assert len(consts) == N_RET_T + N_SWA_T + N_POOL_T + N_SSD_T
    operands = [x, mod, norm_w] + [w_in] * len(windows) + [w_dt, *consts]
    in_specs = [_row_spec(D_MODEL), _mod_spec(), _const_spec(norm_w.shape)]
    in_specs += [_weight_window_spec(layer, cols) for cols in windows]
    in_specs += [_const_spec(a.shape) for a in [w_dt, *consts]]
    scratch = [pltpu.VMEM((ROW_TILE, w), _BF16) for w in MIXER_WIDTHS] + [pltpu.VMEM((ROW_TILE, DT_PAD), _F32)]
    scratch += [pltpu.VMEM((RET_QK_W, RET_V_W), _F32),
                pltpu.VMEM((SWA_HKV, LANES, CHUNK + ROW_TILE), _BF16),
                pltpu.VMEM((CHUNK + ROW_TILE, SWA_HKV * SWA_EXT_W), _BF16),
                pltpu.VMEM((CHUNK + ROW_TILE, BRANCH), _F32),
                pltpu.VMEM((CONV_HIST + ROW_TILE, M_CONV_W), _F32),
                pltpu.VMEM((ROW_TILE, M_CONV_W), _F32),
                pltpu.VMEM((M_GROUPS, M_N, M_GW), _F32)]
    scratch += [pltpu.VMEM((POOL_PAD + POOL_EXT + ROW_TILE, BRANCH), _F32)] * (len(POOL_WINDOWS) - 1)
    return pl.pallas_call(
        _mixers_kernel,
        grid=(batch, seq // ROW_TILE),
        in_specs=in_specs,
        out_specs=[_row_spec(BRANCH)] * N_BRANCHES + [_row_spec(D_MODEL)],
        out_shape=[jax.ShapeDtypeStruct((batch, seq, BRANCH), _BF16)] * N_BRANCHES
        + [jax.ShapeDtypeStruct((batch, seq, D_MODEL), _BF16)],
        scratch_shapes=scratch,
        compiler_params=_params(),
        name="norm_proj_mixers",
    )(*operands)


def _merge_kernel(final, x_ref, mod_ref, h_ref, wg_ref, ret_ref, att_ref, pool_ref, ssm_ref, wup_ref, wout_ref,
                  *rest):
    if final:
        fnw_ref, o_ref = rest
    else:
        (o_ref,) = rest
    h = h_ref[0]
    merged = None
    for i, br_ref in enumerate((ret_ref, att_ref, pool_ref, ssm_ref)):
        gate = _sigmoid(_dot(h, wg_ref[:, i * D_MODEL:(i + 1) * D_MODEL]))
        term = gate * _dot(br_ref[0], wup_ref[i])
        merged = term if merged is None else merged + term
    out = _dot(merged.astype(_BF16), wout_ref[...])
    y = x_ref[0] + mod_ref[0, :, 2 * D_MODEL:3 * D_MODEL] * out
    if final:
        y = y * lax.rsqrt(jnp.mean(y * y, axis=-1, keepdims=True) + EPS) * fnw_ref[...]
    o_ref[0] = y


def _merge(x, mod, h, wg, branches, w_up, w_out, final_norm_w):
    batch, seq, _ = x.shape
    final = final_norm_w is not None
    in_specs = [_row_spec(D_MODEL, MERGE_TILE), _mod_spec(), _row_spec(D_MODEL, MERGE_TILE), _const_spec(wg.shape)]
    in_specs += [_row_spec(BRANCH, MERGE_TILE)] * N_BRANCHES
    in_specs += [_const_spec(w_up.shape), _const_spec(w_out.shape)]
    args = [x, mod, h, wg, *branches, w_up, w_out]
    if final:
        in_specs.append(_const_spec((1, D_MODEL)))
        args.append(final_norm_w.reshape(1, D_MODEL))
    return pl.pallas_call(
        functools.partial(_merge_kernel, final),
        grid=(batch, seq // MERGE_TILE),
        in_specs=in_specs,
        out_specs=_row_spec(D_MODEL, MERGE_TILE),
        out_shape=jax.ShapeDtypeStruct(x.shape, _F32),
        compiler_params=_params(),
        name="merge_out_proj",
    )(*args)


def kernel(x, c, ada_w, ada_b, norm_w, w_in, swa_sinks, pool_w, pool_scale, conv_w, conv_b, dt_bias, a_log,
           d_skip, ssm_norm_w, w_up, w_out, final_norm_w):
    depth = ada_w.shape[0]
    batch = x.shape[0]
    mod_all = _modulation(c, ada_w, ada_b)
    w_in_bf = w_in.astype(_BF16)
    for l in range(depth):
        mod = mod_all[l].reshape(batch, 1, 3 * D_MODEL)
        nw = norm_w[l].reshape(1, D_MODEL)
        w_dt = jnp.pad(w_in_bf[l, :, DT_COLS[0]:DT_COLS[1]], ((0, 0), (0, DT_PAD - M_HEADS)))
        *branches, h = _mixers(l, x, mod, nw, w_in_bf, w_dt, swa_sinks[l], pool_w[l], pool_scale[l], conv_w[l],
                               conv_b[l], dt_bias[l], a_log[l], d_skip[l], ssm_norm_w[l])
        wg = w_in_bf[l, :, MG_COLS[0]:MG_COLS[1]]
        x = _merge(x, mod, h, wg, branches, w_up[l].astype(_BF16), w_out[l].astype(_BF16),
                   final_norm_w if l == depth - 1 else None)
    return x
```

```python
import functools

import jax
import jax.numpy as jnp
import numpy as np
from jax import lax
from jax.experimental import pallas as pl
from jax.experimental.pallas import tpu as pltpu

D_MODEL = 1024
BRANCH = 512
N_BRANCHES = 4
EPS = 1e-6
CHUNK = 128
LANES = 128

RET_HEADS, RET_DK, RET_DV = 4, 64, 128
RET_QK_W = RET_HEADS * RET_DK
RET_V_W = RET_HEADS * RET_DV
SWA_HQ, SWA_HKV, SWA_D = 8, 2, 64
SWA_GROUP = SWA_HQ // SWA_HKV
SWA_Q_W = SWA_HQ * SWA_D
SWA_KV_W = SWA_HKV * SWA_D
SWA_EXT_W = 2 * LANES
LOG2_E = 1.4426950408889634
POOL_WINDOWS = (2, 4, 8, 16)
POOL_GD = BRANCH // len(POOL_WINDOWS)
POOL_EXT = 16
POOL_PAD = 8
M_HEADS, M_P, M_GROUPS, M_N, M_CONV = 8, 64, 2, 128, 4
M_R = M_HEADS // M_GROUPS
M_GW = M_R * M_P
M_BC = M_GROUPS * M_N
M_X_W = M_HEADS * M_P
M_CONV_W = M_X_W + 2 * M_BC
CONV_HIST = 8
DT_PAD = LANES

_SIZES = (256, 256, 512, 512, 512, 128, 128, 512, 512, 512, 1024, 512, M_HEADS, N_BRANCHES * D_MODEL)
_OFFS = np.concatenate([[0], np.cumsum(_SIZES)]).tolist()
RET_COLS = (_OFFS[0], _OFFS[4])
SWA_COLS = (_OFFS[4], _OFFS[8])
POOL_COLS = (_OFFS[8], _OFFS[10])
SSD_COLS = (_OFFS[10], _OFFS[12])
DT_COLS = (_OFFS[12], _OFFS[13])
MG_COLS = (_OFFS[13], _OFFS[14])
MIXER_WIDTHS = tuple(b - a for a, b in (RET_COLS, SWA_COLS, POOL_COLS, SSD_COLS))

ROW_TILE = 512
MERGE_TILE = 1024
N_CHUNKS = ROW_TILE // CHUNK
V7X_VMEM_LIMIT = 56 * 1024 * 1024

_F32 = jnp.float32
_BF16 = jnp.bfloat16


def _sigmoid(x):
    return 0.5 * jnp.tanh(0.5 * x) + 0.5


def _silu(x):
    half = 0.5 * x
    return half + half * jnp.tanh(half)


def _softplus(x):
    return jnp.maximum(x, 0.0) + jnp.log(1.0 + jnp.exp(-jnp.abs(x)))


def _dot(a, b):
    return jnp.dot(a, b, preferred_element_type=_F32)


def _dot_nt(a, b):
    return lax.dot_general(a, b, (((1,), (1,)), ((), ())), preferred_element_type=_F32)


def _dot_tn(a, b):
    return lax.dot_general(a, b, (((0,), (0,)), ((), ())), preferred_element_type=_F32)


def _const_spec(shape):
    nd = len(shape)
    return pl.BlockSpec(shape, lambda *_: (0,) * nd, pipeline_mode=pl.Buffered(1))


def _weight_window_spec(layer, cols):
    return pl.BlockSpec((pl.Element(1), pl.Element(D_MODEL), pl.Element(cols[1] - cols[0])),
                        lambda *_: (layer, 0, cols[0]), pipeline_mode=pl.Buffered(1))


def _row_spec(width, tile=None):
    return pl.BlockSpec((1, tile or ROW_TILE, width), lambda b, t: (b, t, 0))


def _mod_spec():
    return pl.BlockSpec((1, 1, 3 * D_MODEL), lambda b, t: (b, 0, 0))


def _params():
    return pltpu.CompilerParams(dimension_semantics=("parallel", "arbitrary"),
                                vmem_limit_bytes=V7X_VMEM_LIMIT)


def _modulated_norm(x, mod_ref, nw_ref):
    shift = mod_ref[0, :, 0:D_MODEL]
    scale = mod_ref[0, :, D_MODEL:2 * D_MODEL]
    rs = lax.rsqrt(jnp.mean(x * x, axis=-1, keepdims=True) + EPS)
    return (x * rs) * nw_ref[...] * (1.0 + scale) + shift


def _mod_kernel(c_ref, w_ref, b_ref, o_ref):
    sc = _silu(c_ref[...]).astype(_BF16)
    o_ref[0] = _dot(sc, w_ref[0].astype(_BF16)) + b_ref[0]


def _modulation(c, ada_w, ada_b):
    depth, _, n = ada_w.shape
    bn = D_MODEL
    batch = c.shape[0]
    return pl.pallas_call(
        _mod_kernel,
        grid=(depth, n // bn),
        in_specs=[pl.BlockSpec((batch, D_MODEL), lambda l, j: (0, 0)),
                  pl.BlockSpec((1, D_MODEL, bn), lambda l, j: (l, 0, j)),
                  pl.BlockSpec((1, 1, bn), lambda l, j: (l, 0, j))],
        out_specs=pl.BlockSpec((1, batch, bn), lambda l, j: (l, 0, j)),
        out_shape=jax.ShapeDtypeStruct((depth, batch, n), _F32),
        name="adaln_modulation",
    )(c, ada_w, ada_b.reshape(depth, 1, n))


def _retention_tables():
    log_g = np.log(1.0 - 2.0 ** (-5.0 - np.arange(RET_HEADS, dtype=np.float64)))
    pos = np.arange(CHUNK, dtype=np.float64)
    diff = pos[:, None] - pos[None, :]
    inner = np.where(diff >= 0, np.exp(log_g[:, None, None] * np.where(diff >= 0, diff, 0.0)), 0.0)
    q_decay = np.repeat(np.exp(log_g[:, None] * (pos + 1.0)).T, RET_DV, axis=1)
    k_decay = np.repeat(np.exp(log_g[:, None] * (CHUNK - 1.0 - pos)).T, RET_DK, axis=1)
    chunk_decay = np.repeat(np.exp(log_g * CHUNK), RET_DV)[None, :]
    block_diag = (np.arange(RET_QK_W)[:, None] // RET_DK == np.arange(RET_V_W)[None, :] // RET_DV)
    return tuple(jnp.asarray(t, _F32) for t in (inner, q_decay, k_decay, chunk_decay, block_diag))


def _retention_tile(p, o, inner_ref, qd_ref, kd_ref, cd_ref, bd_ref, state_ref):
    head_of_lane = lax.broadcasted_iota(jnp.int32, (CHUNK, RET_QK_W), 1) // RET_DK
    for ci in range(N_CHUNKS):
        rows = slice(ci * CHUNK, (ci + 1) * CHUNK)
        q = p[rows, 0:RET_QK_W]
        k = p[rows, RET_QK_W:2 * RET_QK_W].astype(_F32) * (RET_DK ** -0.5)
        k_bf = k.astype(_BF16)
        v = p[rows, 2 * RET_QK_W:2 * RET_QK_W + RET_V_W]
        state = state_ref[...]
        cross = _dot(q, state.astype(_BF16)) * qd_ref[...]
        update = _dot_tn((k * kd_ref[...]).astype(_BF16), v)
        state_ref[...] = state * cd_ref[...] + update * bd_ref[...]
        for hd in range(RET_HEADS):
            lanes = slice(hd * RET_DV, (hd + 1) * RET_DV)
            q_h = jnp.where(head_of_lane == hd, q, jnp.zeros_like(q))
            scores = _dot_nt(q_h, k_bf) * inner_ref[hd]
            out = _dot(scores.astype(_BF16), v[:, lanes]) + cross[:, lanes]
            out = out * lax.rsqrt(jnp.mean(out * out, axis=-1, keepdims=True) + EPS)
            g0 = 2 * RET_QK_W + RET_V_W + hd * RET_DV
            gate = p[rows, g0:g0 + RET_DV].astype(_F32)
            o[rows, lanes] = (out * _silu(gate)).astype(_BF16)


def _swa_bias():
    slopes = 2.0 ** (-8.0 * np.arange(1, SWA_HQ + 1, dtype=np.float64) / SWA_HQ)
    qi = np.arange(CHUNK)
    kj = np.arange(2 * CHUNK)
    delta = CHUNK + qi[:, None] - kj[None, :]
    valid = (delta >= 0) & (delta < CHUNK)
    variants = []
    for ok in (valid, valid & (kj[None, :] >= CHUNK)):
        bias = np.where(ok[None], -slopes[:, None, None] * delta[None].astype(np.float64), -np.inf)
        variants.append(bias.reshape(SWA_HKV, SWA_GROUP * CHUNK, 2 * CHUNK))
    return jnp.asarray(np.stack(variants) * LOG2_E, _F32)


def _swa_sink_rows(sinks):
    rows = jnp.repeat(sinks.astype(_F32) * LOG2_E, CHUNK)[:, None]
    return jnp.broadcast_to(rows, (SWA_HQ * CHUNK, LANES)).reshape(SWA_HKV, SWA_GROUP * CHUNK, LANES)


def _swa_tile(first_tile, p, o, bias_ref, sink_ref, kt_buf, v_buf):
    low_half = lax.broadcasted_iota(jnp.int32, (ROW_TILE, SWA_KV_W), 1) < SWA_D
    k = p[:, SWA_Q_W:SWA_Q_W + SWA_KV_W].astype(_F32)
    k_swapped = pltpu.roll(k, SWA_D, axis=1)
    for j, dup in enumerate((jnp.where(low_half, k, k_swapped), jnp.where(low_half, k_swapped, k))):
        for bi in range(N_CHUNKS):
            kt_buf[j, :, (bi + 1) * CHUNK:(bi + 2) * CHUNK] = dup[bi * CHUNK:(bi + 1) * CHUNK, :].T.astype(_BF16)
    v = p[:, SWA_Q_W + SWA_KV_W:SWA_Q_W + 2 * SWA_KV_W].astype(_F32)
    v_swapped = pltpu.roll(v, SWA_D, axis=1)
    for j, dup in enumerate((jnp.where(low_half, v, v_swapped), jnp.where(low_half, v_swapped, v))):
        v_buf[CHUNK:CHUNK + ROW_TILE, j * SWA_EXT_W:j * SWA_EXT_W + LANES] = dup.astype(_BF16)
        v_buf[CHUNK:CHUNK + ROW_TILE, j * SWA_EXT_W + LANES:(j + 1) * SWA_EXT_W] = jnp.ones((ROW_TILE, LANES), _BF16)

    low_lanes = lax.broadcasted_iota(jnp.int32, (CHUNK, LANES), 1) < SWA_D
    high_lanes = jnp.logical_not(low_lanes)
    first_variant = first_tile.astype(jnp.int32)
    gate0 = SWA_Q_W + 2 * SWA_KV_W

    for bi in range(N_CHUNKS):
        rows = slice(bi * CHUNK, (bi + 1) * CHUNK)
        keys = slice(bi * CHUNK, (bi + 2) * CHUNK)
        variant = first_variant if bi == 0 else 0
        for j in range(SWA_HKV):
            k_t = kt_buf[j, :, keys]
            v_ext = v_buf[keys, j * SWA_EXT_W:(j + 1) * SWA_EXT_W]
            for pair in range(SWA_GROUP // 2):
                c0 = (j * SWA_GROUP // 2 + pair) * LANES
                qp = p[rows, c0:c0 + LANES].astype(_F32) * (SWA_D ** -0.5 * LOG2_E)
                halves = []
                for odd in range(2):
                    g_rows = slice((2 * pair + odd) * CHUNK, (2 * pair + odd + 1) * CHUNK)
                    q_h = jnp.where(high_lanes if odd else low_lanes, qp, 0.0).astype(_BF16)
                    s = _dot(q_h, k_t) + bias_ref[variant, j, g_rows, :]
                    s_prev, s_cur = s[:, 0:CHUNK], s[:, CHUNK:2 * CHUNK]
                    sink = sink_ref[j, g_rows, :]
                    m = jnp.maximum(jnp.max(jnp.maximum(s_prev, s_cur), axis=-1, keepdims=True), sink)
                    e = jnp.concatenate([jnp.exp2(s_prev - m), jnp.exp2(s_cur - m)], axis=1).astype(_BF16)
                    ov = _dot(e, v_ext)
                    halves.append(ov[:, 0:LANES] / (ov[:, LANES:2 * LANES] + jnp.exp2(sink - m)))
                gate = p[rows, gate0 + c0:gate0 + c0 + LANES].astype(_F32)
                o[rows, c0:c0 + LANES] = (jnp.where(low_lanes, halves[0], halves[1]) * _silu(gate)).astype(_BF16)

    kt_buf[:, :, 0:CHUNK] = kt_buf[:, :, ROW_TILE:ROW_TILE + CHUNK]
    v_buf[0:CHUNK, :] = v_buf[ROW_TILE:ROW_TILE + CHUNK, :]


def _pool_inverse_counts():
    pos = np.arange(ROW_TILE, dtype=np.float64)[:, None]
    windows = np.repeat(np.asarray(POOL_WINDOWS, np.float64), POOL_GD)[None, :]
    steady = np.broadcast_to(1.0 / windows, (ROW_TILE, BRANCH))
    start = 1.0 / np.minimum(pos + 1.0, windows)
    return jnp.asarray(np.stack([steady, start]), _F32)


def _pool_tile(first_tile, p, o, inv_ref, w_ref, scale_ref, hist, levels):
    hist[CHUNK:CHUNK + ROW_TILE, :] = p[:, 0:BRANCH].astype(_F32)
    inv = inv_ref[first_tile.astype(jnp.int32)]
    ext_rows = POOL_EXT + ROW_TILE
    base = CHUNK - POOL_EXT
    sums = []
    for g, w in enumerate(POOL_WINDOWS):
        c0 = g * POOL_GD
        shift = w // 2
        if g == 0:
            total = hist[base:base + ext_rows, :] + hist[base - shift:base - shift + ext_rows, :]
        else:
            prev = levels[g - 1]
            total = (prev[POOL_PAD:POOL_PAD + ext_rows, c0:]
                     + prev[POOL_PAD - shift:POOL_PAD - shift + ext_rows, c0:])
        if g + 1 < len(POOL_WINDOWS):
            levels[g][POOL_PAD:POOL_PAD + ext_rows, c0:] = total
        sums.append(total)
    for g in range(len(POOL_WINDOWS)):
        lanes = slice(g * POOL_GD, (g + 1) * POOL_GD)
        u = hist[CHUNK:CHUNK + ROW_TILE, lanes]
        acc = sums[g][POOL_EXT:, 0:POOL_GD]
        diff = acc * inv[:, lanes] - u
        y = _dot(diff.astype(_BF16), w_ref[g])
        gate = p[:, BRANCH + g * POOL_GD:BRANCH + (g + 1) * POOL_GD].astype(_F32)
        o[:, lanes] = (y * scale_ref[:, lanes] * _silu(gate)).astype(_BF16)
    hist[0:CHUNK, :] = hist[ROW_TILE:ROW_TILE + CHUNK, :]


def _expansion_matrix():
    e = np.zeros((2 * LANES, M_X_W), np.float32)
    for hd in range(M_HEADS):
        e[hd, hd * M_P:(hd + 1) * M_P] = 1.0
        e[LANES + hd, hd * M_P:(hd + 1) * M_P] = 1.0
    return jnp.asarray(e, _BF16)


def _pad_lanes(v, width=DT_PAD):
    return jnp.pad(v.astype(_F32), (0, width - v.shape[0])).reshape(1, width)


def _ssd_tile(p, dt_raw, o, cw_ref, cb_ref, dtb_ref, alog_ref, dskip_ref, nw_ref, tri_ref, exp_ref,
              hist, xbc_buf, state_ref):
    hist[CONV_HIST:CONV_HIST + ROW_TILE, :] = p[:, 0:M_CONV_W].astype(_F32)
    acc = cb_ref[...] + cw_ref[M_CONV - 1:M_CONV, :] * hist[CONV_HIST:CONV_HIST + ROW_TILE, :]
    for j in range(1, M_CONV):
        acc = acc + cw_ref[M_CONV - 1 - j:M_CONV - j, :] * hist[CONV_HIST - j:CONV_HIST - j + ROW_TILE, :]
    xbc_buf[...] = _silu(acc)
    hist[0:CONV_HIST, :] = hist[ROW_TILE:ROW_TILE + CONV_HIST, :]

    a_row = -jnp.exp(alog_ref[...])
    causal = (lax.broadcasted_iota(jnp.int32, (CHUNK, CHUNK), 0)
              >= lax.broadcasted_iota(jnp.int32, (CHUNK, CHUNK), 1))
    head_of_lane = lax.broadcasted_iota(jnp.int32, (CHUNK, M_GW), 1) // M_P

    def expand(vals):
        stacked = jnp.concatenate(vals, axis=0)
        hi = stacked.astype(_BF16)
        lo = (stacked - hi.astype(_F32)).astype(_BF16)
        wide = _dot(jnp.concatenate([hi, lo], axis=1), exp_ref[...])
        return [wide[i * CHUNK:(i + 1) * CHUNK, :] for i in range(len(vals))]

    for ci in range(N_CHUNKS):
        rows = slice(ci * CHUNK, (ci + 1) * CHUNK)
        dt = _softplus(dt_raw[rows, :] + dtb_ref[...])
        dta = dt * a_row
        hi = dta.astype(_BF16)
        rest = dta - hi.astype(_F32)
        mid = rest.astype(_BF16)
        lo = (rest - mid.astype(_F32)).astype(_BF16)
        acs = _dot(tri_ref[...], jnp.concatenate([hi, mid, lo], axis=0))
        acs_t = acs.T
        last = acs[CHUNK - 1:CHUNK, :]
        dt_x, w_x, ex_x = expand([dt, dt * jnp.exp(last - acs), jnp.exp(acs)])
        x = xbc_buf[rows, 0:M_X_W]
        xdt = (x * dt_x).astype(_BF16)
        xd = (x * w_x).astype(_BF16)
        for g in range(M_GROUPS):
            lanes = slice(g * M_GW, (g + 1) * M_GW)
            bmat = xbc_buf[rows, M_X_W + g * M_N:M_X_W + (g + 1) * M_N].astype(_BF16)
            cmat = xbc_buf[rows, M_X_W + M_BC + g * M_N:M_X_W + M_BC + (g + 1) * M_N].astype(_BF16)
            cb = _dot_nt(cmat, bmat)
            state = state_ref[g]
            ex_g = ex_x[:, lanes]
            y = _dot(cmat, state.astype(_BF16)) * ex_g + x[:, lanes] * dskip_ref[:, lanes]
            xdt_g = xdt[:, lanes]
            for r in range(M_R):
                hd = g * M_R + r
                seg = acs[:, hd:hd + 1] - acs_t[hd:hd + 1, :]
                lmat = jnp.exp(jnp.where(causal, seg, -jnp.inf))
                x_h = jnp.where(head_of_lane == r, xdt_g, jnp.zeros_like(xdt_g))
                y = y + _dot((cb * lmat).astype(_BF16), x_h)
            state_ref[g] = state * ex_g[CHUNK - 1:CHUNK, :] + _dot_tn(bmat, xd[:, lanes])
            z = p[rows, M_CONV_W + g * M_GW:M_CONV_W + (g + 1) * M_GW].astype(_F32)
            yz = y * _silu(z)
            yz = yz * lax.rsqrt(jnp.mean(yz * yz, axis=-1, keepdims=True) + EPS)
            o[rows, lanes] = (yz * nw_ref[:, lanes]).astype(_BF16)


N_RET_T, N_SWA_T, N_POOL_T, N_SSD_T = 5, 2, 3, 8


def _mixers_kernel(x_ref, mod_ref, nw_ref, wr_ref, ws_ref, wp_ref, wm_ref, wdt_ref, *rest):
    consts, rest = rest[:N_RET_T + N_SWA_T + N_POOL_T + N_SSD_T], rest[N_RET_T + N_SWA_T + N_POOL_T + N_SSD_T:]
    ret_c, consts = consts[:N_RET_T], consts[N_RET_T:]
    swa_c, consts = consts[:N_SWA_T], consts[N_SWA_T:]
    pool_c, ssd_c = consts[:N_POOL_T], consts[N_POOL_T:]
    ret_o, att_o, pool_o, ssm_o, h_o = (r.at[0] for r in rest[:N_BRANCHES + 1])
    proj = rest[N_BRANCHES + 1:N_BRANCHES + 6]
    ret_state, kt_buf, v_buf, pool_hist, conv_hist, xbc_buf, ssd_state = rest[N_BRANCHES + 6:N_BRANCHES + 13]
    pool_levels = rest[N_BRANCHES + 13:]
    first_tile = pl.program_id(1) == 0

    @pl.when(first_tile)
    def _():
        ret_state[...] = jnp.zeros_like(ret_state)
        ssd_state[...] = jnp.zeros_like(ssd_state)
        kt_buf[:, :, 0:CHUNK] = jnp.zeros((SWA_HKV, LANES, CHUNK), _BF16)
        v_buf[0:CHUNK, :] = jnp.zeros((CHUNK, SWA_HKV * SWA_EXT_W), _BF16)
        pool_hist[0:CHUNK, :] = jnp.zeros((CHUNK, BRANCH), _F32)
        for level in pool_levels:
            level[0:POOL_PAD, :] = jnp.zeros((POOL_PAD, BRANCH), _F32)
        conv_hist[0:CONV_HIST, :] = jnp.zeros((CONV_HIST, M_CONV_W), _F32)

    h = _modulated_norm(x_ref[0], mod_ref, nw_ref).astype(_BF16)
    h_o[...] = h
    for ref, w_ref in zip(proj, (wr_ref.at[0], ws_ref.at[0], wp_ref.at[0], wm_ref.at[0], wdt_ref)):
        ref[...] = _dot(h, w_ref[...]).astype(ref.dtype)

    ret_p, swa_p, pool_p, ssd_p, dt_p = proj
    _retention_tile(ret_p, ret_o, *ret_c, ret_state)
    _swa_tile(first_tile, swa_p, att_o, *swa_c, kt_buf, v_buf)
    _pool_tile(first_tile, pool_p, pool_o, *pool_c, pool_hist, pool_levels)
    _ssd_tile(ssd_p, dt_p, ssm_o, *ssd_c, conv_hist, xbc_buf, ssd_state)


def _mixers(layer, x, mod, norm_w, w_in, w_dt, swa_sinks, pool_w, pool_scale, conv_w, conv_b, dt_bias, a_log,
            d_skip, ssm_norm_w):
    batch, seq, _ = x.shape
    windows = (RET_COLS, SWA_COLS, POOL_COLS, SSD_COLS)
    consts = list(_retention_tables())
    consts += [_swa_bias(), _swa_sink_rows(swa_sinks)]
    consts += [_pool_inverse_counts(), pool_w.astype(_BF16), pool_scale.reshape(1, BRANCH)]
    consts += [conv_w, conv_b.reshape(1, M_CONV_W), _pad_lanes(dt_bias), _pad_lanes(a_log),
               jnp.repeat(d_skip.astype(_F32), M_P).reshape(1, BRANCH), ssm_norm_w.reshape(1, BRANCH),
               jnp.asarray(np.tile(np.tril(np.ones((CHUNK, CHUNK), np.float32)), (1, 3)), _BF16),
               _expansion_matrix()]
    assert len(consts) == N_RET_T + N_SWA_T + N_POOL_T + N_SSD_T
    operands = [x, mod, norm_w] + [w_in] * len(windows) + [w_dt, *consts]
    in_specs = [_row_spec(D_MODEL), _mod_spec(), _const_spec(norm_w.shape)]
    in_specs += [_weight_window_spec(layer, cols) for cols in windows]
    in_specs += [_const_spec(a.shape) for a in [w_dt, *consts]]
    scratch = [pltpu.VMEM((ROW_TILE, w), _BF16) for w in MIXER_WIDTHS] + [pltpu.VMEM((ROW_TILE, DT_PAD), _F32)]
    scratch += [pltpu.VMEM((RET_QK_W, RET_V_W), _F32),
                pltpu.VMEM((SWA_HKV, LANES, CHUNK + ROW_TILE), _BF16),
                pltpu.VMEM((CHUNK + ROW_TILE, SWA_HKV * SWA_EXT_W), _BF16),
                pltpu.VMEM((CHUNK + ROW_TILE, BRANCH), _F32),
                pltpu.VMEM((CONV_HIST + ROW_TILE, M_CONV_W), _F32),
                pltpu.VMEM((ROW_TILE, M_CONV_W), _F32),
                pltpu.VMEM((M_GROUPS, M_N, M_GW), _F32)]
    scratch += [pltpu.VMEM((POOL_PAD + POOL_EXT + ROW_TILE, BRANCH), _F32)] * (len(POOL_WINDOWS) - 1)
    return pl.pallas_call(
        _mixers_kernel,
        grid=(batch, seq // ROW_TILE),
        in_specs=in_specs,
        out_specs=[_row_spec(BRANCH)] * N_BRANCHES + [_row_spec(D_MODEL)],
        out_shape=[jax.ShapeDtypeStruct((batch, seq, BRANCH), _BF16)] * N_BRANCHES
        + [jax.ShapeDtypeStruct((batch, seq, D_MODEL), _BF16)],
        scratch_shapes=scratch,
        compiler_params=_params(),
        name="norm_proj_mixers",
    )(*operands)


def _merge_kernel(final, x_ref, mod_ref, h_ref, wg_ref, ret_ref, att_ref, pool_ref, ssm_ref, wup_ref, wout_ref,
                  *rest):
    if final:
        fnw_ref, o_ref = rest
    else:
        (o_ref,) = rest
    h = h_ref[0]
    merged = None
    for i, br_ref in enumerate((ret_ref, att_ref, pool_ref, ssm_ref)):
        gate = _sigmoid(_dot(h, wg_ref[:, i * D_MODEL:(i + 1) * D_MODEL]))
        term = gate * _dot(br_ref[0], wup_ref[i])
        merged = term if merged is None else merged + term
    out = _dot(merged.astype(_BF16), wout_ref[...])
    y = x_ref[0] + mod_ref[0, :, 2 * D_MODEL:3 * D_MODEL] * out
    if final:
        y = y * lax.rsqrt(jnp.mean(y * y, axis=-1, keepdims=True) + EPS) * fnw_ref[...]
    o_ref[0] = y


def _merge(x, mod, h, wg, branches, w_up, w_out, final_norm_w):
    batch, seq, _ = x.shape
    final = final_norm_w is not None
    in_specs = [_row_spec(D_MODEL, MERGE_TILE), _mod_spec(), _row_spec(D_MODEL, MERGE_TILE), _const_spec(wg.shape)]
    in_specs += [_row_spec(BRANCH, MERGE_TILE)] * N_BRANCHES
    in_specs += [_const_spec(w_up.shape), _const_spec(w_out.shape)]
    args = [x, mod, h, wg, *branches, w_up, w_out]
    if final:
        in_specs.append(_const_spec((1, D_MODEL)))
        args.append(final_norm_w.reshape(1, D_MODEL))
    return pl.pallas_call(
        functools.partial(_merge_kernel, final),
        grid=(batch, seq // MERGE_TILE),
        in_specs=in_specs,
        out_specs=_row_spec(D_MODEL, MERGE_TILE),
        out_shape=jax.ShapeDtypeStruct(x.shape, _F32),
        compiler_params=_params(),
        name="merge_out_proj",
    )(*args)


def kernel(x, c, ada_w, ada_b, norm_w, w_in, swa_sinks, pool_w, pool_scale, conv_w, conv_b, dt_bias, a_log,
           d_skip, ssm_norm_w, w_up, w_out, final_norm_w):
    depth = ada_w.shape[0]
    batch = x.shape[0]
    mod_all = _modulation(c, ada_w, ada_b)
    w_in_bf = w_in.astype(_BF16)
    for l in range(depth):
        mod = mod_all[l].reshape(batch, 1, 3 * D_MODEL)
        nw = norm_w[l].reshape(1, D_MODEL)
        w_dt = jnp.pad(w_in_bf[l, :, DT_COLS[0]:DT_COLS[1]], ((0, 0), (0, DT_PAD - M_HEADS)))
        *branches, h = _mixers(l, x, mod, nw, w_in_bf, w_dt, swa_sinks[l], pool_w[l], pool_scale[l], conv_w[l],
                               conv_b[l], dt_bias[l], a_log[l], d_skip[l], ssm_norm_w[l])
        wg = w_in_bf[l, :, MG_COLS[0]:MG_COLS[1]]
        x = _merge(x, mod, h, wg, branches, w_up[l].astype(_BF16), w_out[l].astype(_BF16),
                   final_norm_w if l == depth - 1 else None)
    return x
```

```python
import functools

import jax
import jax.numpy as jnp
import numpy as np
from jax import lax
from jax.experimental import pallas as pl
from jax.experimental.pallas import tpu as pltpu

D_MODEL = 1024
BRANCH = 512
N_BRANCHES = 4
EPS = 1e-6
CHUNK = 128
LANES = 128

RET_HEADS, RET_DK, RET_DV = 4, 64, 128
RET_QK_W = RET_HEADS * RET_DK
RET_V_W = RET_HEADS * RET_DV
SWA_HQ, SWA_HKV, SWA_D = 8, 2, 64
SWA_GROUP = SWA_HQ // SWA_HKV
SWA_Q_W = SWA_HQ * SWA_D
SWA_KV_W = SWA_HKV * SWA_D
SWA_EXT_W = 2 * LANES
LOG2_E = 1.4426950408889634
POOL_WINDOWS = (2, 4, 8, 16)
POOL_GD = BRANCH // len(POOL_WINDOWS)
POOL_EXT = 16
POOL_PAD = 8
M_HEADS, M_P, M_GROUPS, M_N, M_CONV = 8, 64, 2, 128, 4
M_R = M_HEADS // M_GROUPS
M_GW = M_R * M_P
M_BC = M_GROUPS * M_N
M_X_W = M_HEADS * M_P
M_CONV_W = M_X_W + 2 * M_BC
CONV_HIST = 8
DT_PAD = LANES

_SIZES = (256, 256, 512, 512, 512, 128, 128, 512, 512, 512, 1024, 512, M_HEADS, N_BRANCHES * D_MODEL)
_OFFS = np.concatenate([[0], np.cumsum(_SIZES)]).tolist()
RET_COLS = (_OFFS[0], _OFFS[4])
SWA_COLS = (_OFFS[4], _OFFS[8])
POOL_COLS = (_OFFS[8], _OFFS[10])
SSD_COLS = (_OFFS[10], _OFFS[12])
DT_COLS = (_OFFS[12], _OFFS[13])
MG_COLS = (_OFFS[13], _OFFS[14])
MIXER_WIDTHS = tuple(b - a for a, b in (RET_COLS, SWA_COLS, POOL_COLS, SSD_COLS))

ROW_TILE = 512
MERGE_TILE = 1024
N_CHUNKS = ROW_TILE // CHUNK
V7X_VMEM_LIMIT = 56 * 1024 * 1024

_F32 = jnp.float32
_BF16 = jnp.bfloat16


def _sigmoid(x):
    return 0.5 * jnp.tanh(0.5 * x) + 0.5


def _silu(x):
    half = 0.5 * x
    return half + half * jnp.tanh(half)


def _softplus(x):
    return jnp.maximum(x, 0.0) + jnp.log(1.0 + jnp.exp(-jnp.abs(x)))


def _dot(a, b):
    return jnp.dot(a, b, preferred_element_type=_F32)


def _dot_nt(a, b):
    return lax.dot_general(a, b, (((1,), (1,)), ((), ())), preferred_element_type=_F32)


def _dot_tn(a, b):
    return lax.dot_general(a, b, (((0,), (0,)), ((), ())), preferred_element_type=_F32)


def _const_spec(shape):
    nd = len(shape)
    return pl.BlockSpec(shape, lambda *_: (0,) * nd, pipeline_mode=pl.Buffered(1))


def _weight_window_spec(layer, cols):
    return pl.BlockSpec((pl.Element(1), pl.Element(D_MODEL), pl.Element(cols[1] - cols[0])),
                        lambda *_: (layer, 0, cols[0]), pipeline_mode=pl.Buffered(1))


def _row_spec(width, tile=None):
    return pl.BlockSpec((1, tile or ROW_TILE, width), lambda b, t: (b, t, 0))


def _mod_spec():
    return pl.BlockSpec((1, 1, 3 * D_MODEL), lambda b, t: (b, 0, 0))


def _params():
    return pltpu.CompilerParams(dimension_semantics=("parallel", "arbitrary"),
                                vmem_limit_bytes=V7X_VMEM_LIMIT)


def _modulated_norm(x, mod_ref, nw_ref):
    shift = mod_ref[0, :, 0:D_MODEL]
    scale = mod_ref[0, :, D_MODEL:2 * D_MODEL]
    rs = lax.rsqrt(jnp.mean(x * x, axis=-1, keepdims=True) + EPS)
    return (x * rs) * nw_ref[...] * (1.0 + scale) + shift


def _mod_kernel(c_ref, w_ref, b_ref, o_ref):
    sc = _silu(c_ref[...]).astype(_BF16)
    o_ref[0] = _dot(sc, w_ref[0].astype(_BF16)) + b_ref[0]


def _modulation(c, ada_w, ada_b):
    depth, _, n = ada_w.shape
    bn = D_MODEL
    batch = c.shape[0]
    return pl.pallas_call(
        _mod_kernel,
        grid=(depth, n // bn),
        in_specs=[pl.BlockSpec((batch, D_MODEL), lambda l, j: (0, 0)),
                  pl.BlockSpec((1, D_MODEL, bn), lambda l, j: (l, 0, j)),
                  pl.BlockSpec((1, 1, bn), lambda l, j: (l, 0, j))],
        out_specs=pl.BlockSpec((1, batch, bn), lambda l, j: (l, 0, j)),
        out_shape=jax.ShapeDtypeStruct((depth, batch, n), _F32),
        name="adaln_modulation",
    )(c, ada_w, ada_b.reshape(depth, 1, n))


def _retention_tables():
    log_g = np.log(1.0 - 2.0 ** (-5.0 - np.arange(RET_HEADS, dtype=np.float64)))
    pos = np.arange(CHUNK, dtype=np.float64)
    diff = pos[:, None] - pos[None, :]
    inner = np.where(diff >= 0, np.exp(log_g[:, None, None] * np.where(diff >= 0, diff, 0.0)), 0.0)
    q_decay = np.repeat(np.exp(log_g[:, None] * (pos + 1.0)).T, RET_DV, axis=1)
    k_decay = np.repeat(np.exp(log_g[:, None] * (CHUNK - 1.0 - pos)).T, RET_DK, axis=1)
    chunk_decay = np.repeat(np.exp(log_g * CHUNK), RET_DV)[None, :]
    block_diag = (np.arange(RET_QK_W)[:, None] // RET_DK == np.arange(RET_V_W)[None, :] // RET_DV)
    return tuple(jnp.asarray(t, _F32) for t in (inner, q_decay, k_decay, chunk_decay, block_diag))


def _retention_tile(p, o, inner_ref, qd_ref, kd_ref, cd_ref, bd_ref, state_ref):
    head_of_lane = lax.broadcasted_iota(jnp.int32, (CHUNK, RET_QK_W), 1) // RET_DK
    for ci in range(N_CHUNKS):
        rows = slice(ci * CHUNK, (ci + 1) * CHUNK)
        q = p[rows, 0:RET_QK_W]
        k = p[rows, RET_QK_W:2 * RET_QK_W].astype(_F32) * (RET_DK ** -0.5)
        k_bf = k.astype(_BF16)
        v = p[rows, 2 * RET_QK_W:2 * RET_QK_W + RET_V_W]
        state = state_ref[...]
        cross = _dot(q, state.astype(_BF16)) * qd_ref[...]
        update = _dot_tn((k * kd_ref[...]).astype(_BF16), v)
        state_ref[...] = state * cd_ref[...] + update * bd_ref[...]
        q_heads = jnp.concatenate([jnp.where(head_of_lane == hd, q, jnp.zeros_like(q)) for hd in range(RET_HEADS)],
                                  axis=0)
        all_scores = _dot_nt(q_heads, k_bf)
        for hd in range(RET_HEADS):
            lanes = slice(hd * RET_DV, (hd + 1) * RET_DV)
            scores = all_scores[hd * CHUNK:(hd + 1) * CHUNK, :] * inner_ref[hd]
            out = _dot(scores.astype(_BF16), v[:, lanes]) + cross[:, lanes]
            out = out * lax.rsqrt(jnp.mean(out * out, axis=-1, keepdims=True) + EPS)
            g0 = 2 * RET_QK_W + RET_V_W + hd * RET_DV
            gate = p[rows, g0:g0 + RET_DV].astype(_F32)
            o[rows, lanes] = (out * _silu(gate)).astype(_BF16)


def _swa_bias():
    slopes = 2.0 ** (-8.0 * np.arange(1, SWA_HQ + 1, dtype=np.float64) / SWA_HQ)
    qi = np.arange(CHUNK)
    kj = np.arange(2 * CHUNK)
    delta = CHUNK + qi[:, None] - kj[None, :]
    valid = (delta >= 0) & (delta < CHUNK)
    variants = []
    for ok in (valid, valid & (kj[None, :] >= CHUNK)):
        bias = np.where(ok[None], -slopes[:, None, None] * delta[None].astype(np.float64), -np.inf)
        variants.append(bias.reshape(SWA_HKV, SWA_GROUP * CHUNK, 2 * CHUNK))
    return jnp.asarray(np.stack(variants) * LOG2_E, _F32)


def _swa_sink_rows(sinks):
    rows = jnp.repeat(sinks.astype(_F32) * LOG2_E, CHUNK)[:, None]
    return jnp.broadcast_to(rows, (SWA_HQ * CHUNK, LANES)).reshape(SWA_HKV, SWA_GROUP * CHUNK, LANES)


def _swa_tile(first_tile, p, o, bias_ref, sink_ref, kt_buf, v_buf):
    low_half = lax.broadcasted_iota(jnp.int32, (ROW_TILE, SWA_KV_W), 1) < SWA_D
    k = p[:, SWA_Q_W:SWA_Q_W + SWA_KV_W].astype(_F32)
    k_swapped = pltpu.roll(k, SWA_D, axis=1)
    for j, dup in enumerate((jnp.where(low_half, k, k_swapped), jnp.where(low_half, k_swapped, k))):
        for bi in range(N_CHUNKS):
            kt_buf[j, :, (bi + 1) * CHUNK:(bi + 2) * CHUNK] = dup[bi * CHUNK:(bi + 1) * CHUNK, :].T.astype(_BF16)
    v = p[:, SWA_Q_W + SWA_KV_W:SWA_Q_W + 2 * SWA_KV_W].astype(_F32)
    v_swapped = pltpu.roll(v, SWA_D, axis=1)
    for j, dup in enumerate((jnp.where(low_half, v, v_swapped), jnp.where(low_half, v_swapped, v))):
        v_buf[CHUNK:CHUNK + ROW_TILE, j * SWA_EXT_W:j * SWA_EXT_W + LANES] = dup.astype(_BF16)
        v_buf[CHUNK:CHUNK + ROW_TILE, j * SWA_EXT_W + LANES:(j + 1) * SWA_EXT_W] = jnp.ones((ROW_TILE, LANES), _BF16)

    low_lanes = lax.broadcasted_iota(jnp.int32, (CHUNK, LANES), 1) < SWA_D
    high_lanes = jnp.logical_not(low_lanes)
    first_variant = first_tile.astype(jnp.int32)
    gate0 = SWA_Q_W + 2 * SWA_KV_W

    for bi in range(N_CHUNKS):
        rows = slice(bi * CHUNK, (bi + 1) * CHUNK)
        keys = slice(bi * CHUNK, (bi + 2) * CHUNK)
        variant = first_variant if bi == 0 else 0
        for j in range(SWA_HKV):
            q_heads = []
            for pair in range(SWA_GROUP // 2):
                c0 = (j * SWA_GROUP // 2 + pair) * LANES
                qp = p[rows, c0:c0 + LANES].astype(_F32) * (SWA_D ** -0.5 * LOG2_E)
                q_heads += [jnp.where(low_lanes, qp, 0.0).astype(_BF16), jnp.where(high_lanes, qp, 0.0).astype(_BF16)]
            scores = _dot(jnp.concatenate(q_heads, axis=0), kt_buf[j, :, keys])
            weights, shifts = [], []
            for g in range(SWA_GROUP):
                g_rows = slice(g * CHUNK, (g + 1) * CHUNK)
                s = scores[g_rows, :] + bias_ref[variant, j, g_rows, :]
                s_prev, s_cur = s[:, 0:CHUNK], s[:, CHUNK:2 * CHUNK]
                sink = sink_ref[j, g_rows, :]
                m = jnp.maximum(jnp.max(jnp.maximum(s_prev, s_cur), axis=-1, keepdims=True), sink)
                weights.append(jnp.concatenate([jnp.exp2(s_prev - m), jnp.exp2(s_cur - m)], axis=1).astype(_BF16))
                shifts.append(jnp.exp2(sink - m))
            ov = _dot(jnp.concatenate(weights, axis=0), v_buf[keys, j * SWA_EXT_W:(j + 1) * SWA_EXT_W])
            for pair in range(SWA_GROUP // 2):
                c0 = (j * SWA_GROUP // 2 + pair) * LANES
                halves = []
                for odd in range(2):
                    g = 2 * pair + odd
                    g_rows = slice(g * CHUNK, (g + 1) * CHUNK)
                    halves.append(ov[g_rows, 0:LANES] / (ov[g_rows, LANES:2 * LANES] + shifts[g]))
                gate = p[rows, gate0 + c0:gate0 + c0 + LANES].astype(_F32)
                o[rows, c0:c0 + LANES] = (jnp.where(low_lanes, halves[0], halves[1]) * _silu(gate)).astype(_BF16)

    kt_buf[:, :, 0:CHUNK] = kt_buf[:, :, ROW_TILE:ROW_TILE + CHUNK]
    v_buf[0:CHUNK, :] = v_buf[ROW_TILE:ROW_TILE + CHUNK, :]


def _pool_inverse_counts():
    pos = np.arange(ROW_TILE, dtype=np.float64)[:, None]
    windows = np.repeat(np.asarray(POOL_WINDOWS, np.float64), POOL_GD)[None, :]
    steady = np.broadcast_to(1.0 / windows, (ROW_TILE, BRANCH))
    start = 1.0 / np.minimum(pos + 1.0, windows)
    return jnp.asarray(np.stack([steady, start]), _F32)


def _pool_tile(first_tile, p, o, inv_ref, w_ref, scale_ref, hist, levels):
    hist[CHUNK:CHUNK + ROW_TILE, :] = p[:, 0:BRANCH].astype(_F32)
    inv = inv_ref[first_tile.astype(jnp.int32)]
    ext_rows = POOL_EXT + ROW_TILE
    base = CHUNK - POOL_EXT
    sums = []
    for g, w in enumerate(POOL_WINDOWS):
        c0 = g * POOL_GD
        shift = w // 2
        if g == 0:
            total = hist[base:base + ext_rows, :] + hist[base - shift:base - shift + ext_rows, :]
        else:
            prev = levels[g - 1]
            total = (prev[POOL_PAD:POOL_PAD + ext_rows, c0:]
                     + prev[POOL_PAD - shift:POOL_PAD - shift + ext_rows, c0:])
        if g + 1 < len(POOL_WINDOWS):
            levels[g][POOL_PAD:POOL_PAD + ext_rows, c0:] = total
        sums.append(total)
    for g in range(len(POOL_WINDOWS)):
        lanes = slice(g * POOL_GD, (g + 1) * POOL_GD)
        u = hist[CHUNK:CHUNK + ROW_TILE, lanes]
        acc = sums[g][POOL_EXT:, 0:POOL_GD]
        diff = acc * inv[:, lanes] - u
        y = _dot(diff.astype(_BF16), w_ref[g])
        gate = p[:, BRANCH + g * POOL_GD:BRANCH + (g + 1) * POOL_GD].astype(_F32)
        o[:, lanes] = (y * scale_ref[:, lanes] * _silu(gate)).astype(_BF16)
    hist[0:CHUNK, :] = hist[ROW_TILE:ROW_TILE + CHUNK, :]


def _expansion_matrix():
    e = np.zeros((2 * LANES, M_X_W), np.float32)
    for hd in range(M_HEADS):
        e[hd, hd * M_P:(hd + 1) * M_P] = 1.0
        e[LANES + hd, hd * M_P:(hd + 1) * M_P] = 1.0
    return jnp.asarray(e, _BF16)


def _pad_lanes(v, width=DT_PAD):
    return jnp.pad(v.astype(_F32), (0, width - v.shape[0])).reshape(1, width)


def _ssd_tile(p, dt_raw, o, cw_ref, cb_ref, dtb_ref, alog_ref, dskip_ref, nw_ref, tri_ref, exp_ref,
              hist, xbc_buf, state_ref):
    hist[CONV_HIST:CONV_HIST + ROW_TILE, :] = p[:, 0:M_CONV_W].astype(_F32)
    acc = cb_ref[...] + cw_ref[M_CONV - 1:M_CONV, :] * hist[CONV_HIST:CONV_HIST + ROW_TILE, :]
    for j in range(1, M_CONV):
        acc = acc + cw_ref[M_CONV - 1 - j:M_CONV - j, :] * hist[CONV_HIST - j:CONV_HIST - j + ROW_TILE, :]
    xbc_buf[...] = _silu(acc)
    hist[0:CONV_HIST, :] = hist[ROW_TILE:ROW_TILE + CONV_HIST, :]

    a_row = -jnp.exp(alog_ref[...])
    causal = (lax.broadcasted_iota(jnp.int32, (CHUNK, CHUNK), 0)
              >= lax.broadcasted_iota(jnp.int32, (CHUNK, CHUNK), 1))
    head_of_lane = lax.broadcasted_iota(jnp.int32, (CHUNK, M_GW), 1) // M_P

    def expand(vals):
        stacked = jnp.concatenate(vals, axis=0)
        hi = stacked.astype(_BF16)
        lo = (stacked - hi.astype(_F32)).astype(_BF16)
        wide = _dot(jnp.concatenate([hi, lo], axis=1), exp_ref[...])
        return [wide[i * CHUNK:(i + 1) * CHUNK, :] for i in range(len(vals))]

    for ci in range(N_CHUNKS):
        rows = slice(ci * CHUNK, (ci + 1) * CHUNK)
        dt = _softplus(dt_raw[rows, :] + dtb_ref[...])
        acs = jnp.dot(tri_ref[...], dt * a_row, preferred_element_type=_F32, precision=lax.Precision.HIGHEST)
        acs_t = acs.T
        last = acs[CHUNK - 1:CHUNK, :]
        dt_x, w_x, ex_x = expand([dt, dt * jnp.exp(last - acs), jnp.exp(acs)])
        x = xbc_buf[rows, 0:M_X_W]
        xdt = (x * dt_x).astype(_BF16)
        xd = (x * w_x).astype(_BF16)
        for g in range(M_GROUPS):
            lanes = slice(g * M_GW, (g + 1) * M_GW)
            bmat = xbc_buf[rows, M_X_W + g * M_N:M_X_W + (g + 1) * M_N].astype(_BF16)
            cmat = xbc_buf[rows, M_X_W + M_BC + g * M_N:M_X_W + M_BC + (g + 1) * M_N].astype(_BF16)
            cb = _dot_nt(cmat, bmat)
            state = state_ref[g]
            ex_g = ex_x[:, lanes]
            y = _dot(cmat, state.astype(_BF16)) * ex_g + x[:, lanes] * dskip_ref[:, lanes]
            xdt_g = xdt[:, lanes]
            decays, sources = [], []
            for r in range(M_R):
                hd = g * M_R + r
                seg = acs[:, hd:hd + 1] - acs_t[hd:hd + 1, :]
                lmat = jnp.exp(jnp.where(causal, seg, -jnp.inf))
                decays.append((cb * lmat).astype(_BF16))
                sources.append(jnp.where(head_of_lane == r, xdt_g, jnp.zeros_like(xdt_g)))
            y = y + _dot(jnp.concatenate(decays, axis=1), jnp.concatenate(sources, axis=0))
            state_ref[g] = state * ex_g[CHUNK - 1:CHUNK, :] + _dot_tn(bmat, xd[:, lanes])
            z = p[rows, M_CONV_W + g * M_GW:M_CONV_W + (g + 1) * M_GW].astype(_F32)
            yz = y * _silu(z)
            yz = yz * lax.rsqrt(jnp.mean(yz * yz, axis=-1, keepdims=True) + EPS)
            o[rows, lanes] = (yz * nw_ref[:, lanes]).astype(_BF16)


N_RET_T, N_SWA_T, N_POOL_T, N_SSD_T = 5, 2, 3, 8


def _mixers_kernel(x_ref, mod_ref, nw_ref, wr_ref, ws_ref, wp_ref, wm_ref, wdt_ref, *rest):
    consts, rest = rest[:N_RET_T + N_SWA_T + N_POOL_T + N_SSD_T], rest[N_RET_T + N_SWA_T + N_POOL_T + N_SSD_T:]
    ret_c, consts = consts[:N_RET_T], consts[N_RET_T:]
    swa_c, consts = consts[:N_SWA_T], consts[N_SWA_T:]
    pool_c, ssd_c = consts[:N_POOL_T], consts[N_POOL_T:]
    ret_o, att_o, pool_o, ssm_o, h_o = (r.at[0] for r in rest[:N_BRANCHES + 1])
    proj = rest[N_BRANCHES + 1:N_BRANCHES + 6]
    ret_state, kt_buf, v_buf, pool_hist, conv_hist, xbc_buf, ssd_state = rest[N_BRANCHES + 6:N_BRANCHES + 13]
    pool_levels = rest[N_BRANCHES + 13:]
    first_tile = pl.program_id(1) == 0

    @pl.when(first_tile)
    def _():
        ret_state[...] = jnp.zeros_like(ret_state)
        ssd_state[...] = jnp.zeros_like(ssd_state)
        kt_buf[:, :, 0:CHUNK] = jnp.zeros((SWA_HKV, LANES, CHUNK), _BF16)
        v_buf[0:CHUNK, :] = jnp.zeros((CHUNK, SWA_HKV * SWA_EXT_W), _BF16)
        pool_hist[0:CHUNK, :] = jnp.zeros((CHUNK, BRANCH), _F32)
        for level in pool_levels:
            level[0:POOL_PAD, :] = jnp.zeros((POOL_PAD, BRANCH), _F32)
        conv_hist[0:CONV_HIST, :] = jnp.zeros((CONV_HIST, M_CONV_W), _F32)

    h = _modulated_norm(x_ref[0], mod_ref, nw_ref).astype(_BF16)
    h_o[...] = h
    for ref, w_ref in zip(proj, (wr_ref.at[0], ws_ref.at[0], wp_ref.at[0], wm_ref.at[0], wdt_ref)):
        ref[...] = _dot(h, w_ref[...]).astype(ref.dtype)

    ret_p, swa_p, pool_p, ssd_p, dt_p = proj
    _retention_tile(ret_p, ret_o, *ret_c, ret_state)
    _swa_tile(first_tile, swa_p, att_o, *swa_c, kt_buf, v_buf)
    _pool_tile(first_tile, pool_p, pool_o, *pool_c, pool_hist, pool_levels)
    _ssd_tile(ssd_p, dt_p, ssm_o, *ssd_c, conv_hist, xbc_buf, ssd_state)


def _mixers(layer, x, mod, norm_w, w_in, w_dt, swa_sinks, pool_w, pool_scale, conv_w, conv_b, dt_bias, a_log,
            d_skip, ssm_norm_w):
    batch, seq, _ = x.shape
    windows = (RET_COLS, SWA_COLS, POOL_COLS, SSD_COLS)
    consts = list(_retention_tables())
    consts += [_swa_bias(), _swa_sink_rows(swa_sinks)]
    consts += [_pool_inverse_counts(), pool_w.astype(_BF16), pool_scale.reshape(1, BRANCH)]
    consts += [conv_w, conv_b.reshape(1, M_CONV_W), _pad_lanes(dt_bias), _pad_lanes(a_log),
               jnp.repeat(d_skip.astype(_F32), M_P).reshape(1, BRANCH), ssm_norm_w.reshape(1, BRANCH),
               jnp.asarray(np.tril(np.ones((CHUNK, CHUNK), np.float32))), _expansion_matrix()]
    assert len(consts) == N_RET_T + N_SWA_T + N_POOL_T + N_SSD_T
    operands = [x, mod, norm_w] + [w_in] * len(windows) + [w_dt, *consts]
    in_specs = [_row_spec(D_MODEL), _mod_spec(), _const_spec(norm_w.shape)]
    in_specs += [_weight_window_spec(layer, cols) for cols in windows]
    in_specs += [_const_spec(a.shape) for a in [w_dt, *consts]]
    scratch = [pltpu.VMEM((ROW_TILE, w), _BF16) for w in MIXER_WIDTHS] + [pltpu.VMEM((ROW_TILE, DT_PAD), _F32)]
    scratch += [pltpu.VMEM((RET_QK_W, RET_V_W), _F32),
                pltpu.VMEM((SWA_HKV, LANES, CHUNK + ROW_TILE), _BF16),
                pltpu.VMEM((CHUNK + ROW_TILE, SWA_HKV * SWA_EXT_W), _BF16),
                pltpu.VMEM((CHUNK + ROW_TILE, BRANCH), _F32),
                pltpu.VMEM((CONV_HIST + ROW_TILE, M_CONV_W), _F32),
                pltpu.VMEM((ROW_TILE, M_CONV_W), _F32),
                pltpu.VMEM((M_GROUPS, M_N, M_GW), _F32)]
    scratch += [pltpu.VMEM((POOL_PAD + POOL_EXT + ROW_TILE, BRANCH), _F32)] * (len(POOL_WINDOWS) - 1)
    return pl.pallas_call(
        _mixers_kernel,
        grid=(batch, seq // ROW_TILE),
        in_specs=in_specs,
        out_specs=[_row_spec(BRANCH)] * N_BRANCHES + [_row_spec(D_MODEL)],
        out_shape=[jax.ShapeDtypeStruct((batch, seq, BRANCH), _BF16)] * N_BRANCHES
        + [jax.ShapeDtypeStruct((batch, seq, D_MODEL), _BF16)],
        scratch_shapes=scratch,
        compiler_params=_params(),
        name="norm_proj_mixers",
    )(*operands)


def _merge_kernel(final, x_ref, mod_ref, h_ref, wg_ref, ret_ref, att_ref, pool_ref, ssm_ref, wup_ref, wout_ref,
                  *rest):
    if final:
        fnw_ref, o_ref = rest
    else:
        (o_ref,) = rest
    h = h_ref[0]
    merged = None
    for i, br_ref in enumerate((ret_ref, att_ref, pool_ref, ssm_ref)):
        gate = _sigmoid(_dot(h, wg_ref[:, i * D_MODEL:(i + 1) * D_MODEL]))
        term = gate * _dot(br_ref[0], wup_ref[i])
        merged = term if merged is None else merged + term
    out = _dot(merged.astype(_BF16), wout_ref[...])
    y = x_ref[0] + mod_ref[0, :, 2 * D_MODEL:3 * D_MODEL] * out
    if final:
        y = y * lax.rsqrt(jnp.mean(y * y, axis=-1, keepdims=True) + EPS) * fnw_ref[...]
    o_ref[0] = y


def _merge(x, mod, h, wg, branches, w_up, w_out, final_norm_w):
    batch, seq, _ = x.shape
    final = final_norm_w is not None
    in_specs = [_row_spec(D_MODEL, MERGE_TILE), _mod_spec(), _row_spec(D_MODEL, MERGE_TILE), _const_spec(wg.shape)]
    in_specs += [_row_spec(BRANCH, MERGE_TILE)] * N_BRANCHES
    in_specs += [_const_spec(w_up.shape), _const_spec(w_out.shape)]
    args = [x, mod, h, wg, *branches, w_up, w_out]
    if final:
        in_specs.append(_const_spec((1, D_MODEL)))
        args.append(final_norm_w.reshape(1, D_MODEL))
    return pl.pallas_call(
        functools.partial(_merge_kernel, final),
        grid=(batch, seq // MERGE_TILE),
        in_specs=in_specs,
        out_specs=_row_spec(D_MODEL, MERGE_TILE),
        out_shape=jax.ShapeDtypeStruct(x.shape, _F32),
        compiler_params=_params(),
        name="merge_out_proj",
    )(*args)


def kernel(x, c, ada_w, ada_b, norm_w, w_in, swa_sinks, pool_w, pool_scale, conv_w, conv_b, dt_bias, a_log,
           d_skip, ssm_norm_w, w_up, w_out, final_norm_w):
    depth = ada_w.shape[0]
    batch = x.shape[0]
    mod_all = _modulation(c, ada_w, ada_b)
    w_in_bf = w_in.astype(_BF16)
    for l in range(depth):
        mod = mod_all[l].reshape(batch, 1, 3 * D_MODEL)
        nw = norm_w[l].reshape(1, D_MODEL)
        w_dt = jnp.pad(w_in_bf[l, :, DT_COLS[0]:DT_COLS[1]], ((0, 0), (0, DT_PAD - M_HEADS)))
        *branches, h = _mixers(l, x, mod, nw, w_in_bf, w_dt, swa_sinks[l], pool_w[l], pool_scale[l], conv_w[l],
                               conv_b[l], dt_bias[l], a_log[l], d_skip[l], ssm_norm_w[l])
        wg = w_in_bf[l, :, MG_COLS[0]:MG_COLS[1]]
        x = _merge(x, mod, h, wg, branches, w_up[l].astype(_BF16), w_out[l].astype(_BF16),
                   final_norm_w if l == depth - 1 else None)
    return x
```

```python
import functools

import jax
import jax.numpy as jnp
import numpy as np
from jax import lax
from jax.experimental import pallas as pl
from jax.experimental.pallas import tpu as pltpu

D_MODEL = 1024
BRANCH = 512
N_BRANCHES = 4
EPS = 1e-6
CHUNK = 128
LANES = 128

RET_HEADS, RET_DK, RET_DV = 4, 64, 128
RET_QK_W = RET_HEADS * RET_DK
RET_V_W = RET_HEADS * RET_DV
SWA_HQ, SWA_HKV, SWA_D = 8, 2, 64
SWA_GROUP = SWA_HQ // SWA_HKV
SWA_Q_W = SWA_HQ * SWA_D
SWA_KV_W = SWA_HKV * SWA_D
SWA_EXT_W = 2 * LANES
LOG2_E = 1.4426950408889634
POOL_WINDOWS = (2, 4, 8, 16)
POOL_GD = BRANCH // len(POOL_WINDOWS)
POOL_EXT = 16
POOL_PAD = 8
M_HEADS, M_P, M_GROUPS, M_N, M_CONV = 8, 64, 2, 128, 4
M_R = M_HEADS // M_GROUPS
M_GW = M_R * M_P
M_BC = M_GROUPS * M_N
M_X_W = M_HEADS * M_P
M_CONV_W = M_X_W + 2 * M_BC
CONV_HIST = 8
DT_PAD = LANES

_SIZES = (256, 256, 512, 512, 512, 128, 128, 512, 512, 512, 1024, 512, M_HEADS, N_BRANCHES * D_MODEL)
_OFFS = np.concatenate([[0], np.cumsum(_SIZES)]).tolist()
RET_COLS = (_OFFS[0], _OFFS[4])
SWA_COLS = (_OFFS[4], _OFFS[8])
POOL_COLS = (_OFFS[8], _OFFS[10])
SSD_COLS = (_OFFS[10], _OFFS[12])
DT_COLS = (_OFFS[12], _OFFS[13])
MG_COLS = (_OFFS[13], _OFFS[14])
MIXER_WIDTHS = tuple(b - a for a, b in (RET_COLS, SWA_COLS, POOL_COLS, SSD_COLS))

ROW_TILE = 512
MERGE_TILE = 1024
N_CHUNKS = ROW_TILE // CHUNK
V7X_VMEM_LIMIT = 56 * 1024 * 1024

_F32 = jnp.float32
_BF16 = jnp.bfloat16


def _sigmoid(x):
    return 0.5 * jnp.tanh(0.5 * x) + 0.5


def _silu(x):
    half = 0.5 * x
    return half + half * jnp.tanh(half)


def _softplus(x):
    return jnp.maximum(x, 0.0) + jnp.log(1.0 + jnp.exp(-jnp.abs(x)))


def _dot(a, b):
    return jnp.dot(a, b, preferred_element_type=_F32)


def _dot_nt(a, b):
    return lax.dot_general(a, b, (((1,), (1,)), ((), ())), preferred_element_type=_F32)


def _dot_tn(a, b):
    return lax.dot_general(a, b, (((0,), (0,)), ((), ())), preferred_element_type=_F32)


def _const_spec(shape):
    nd = len(shape)
    return pl.BlockSpec(shape, lambda *_: (0,) * nd, pipeline_mode=pl.Buffered(1))


def _weight_window_spec(layer, cols):
    return pl.BlockSpec((pl.Element(1), pl.Element(D_MODEL), pl.Element(cols[1] - cols[0])),
                        lambda *_: (layer, 0, cols[0]), pipeline_mode=pl.Buffered(1))


def _row_spec(width, tile=None):
    return pl.BlockSpec((1, tile or ROW_TILE, width), lambda b, t: (b, t, 0))


def _mod_spec():
    return pl.BlockSpec((1, 1, 3 * D_MODEL), lambda b, t: (b, 0, 0))


def _params():
    return pltpu.CompilerParams(dimension_semantics=("parallel", "arbitrary"),
                                vmem_limit_bytes=V7X_VMEM_LIMIT)


def _modulated_norm(x, mod_ref, nw_ref):
    shift = mod_ref[0, :, 0:D_MODEL]
    scale = mod_ref[0, :, D_MODEL:2 * D_MODEL]
    rs = lax.rsqrt(jnp.mean(x * x, axis=-1, keepdims=True) + EPS)
    return (x * rs) * nw_ref[...] * (1.0 + scale) + shift


def _mod_kernel(c_ref, w_ref, b_ref, o_ref):
    sc = _silu(c_ref[...]).astype(_BF16)
    o_ref[0] = _dot(sc, w_ref[0].astype(_BF16)) + b_ref[0]


def _modulation(c, ada_w, ada_b):
    depth, _, n = ada_w.shape
    bn = D_MODEL
    batch = c.shape[0]
    return pl.pallas_call(
        _mod_kernel,
        grid=(depth, n // bn),
        in_specs=[pl.BlockSpec((batch, D_MODEL), lambda l, j: (0, 0)),
                  pl.BlockSpec((1, D_MODEL, bn), lambda l, j: (l, 0, j)),
                  pl.BlockSpec((1, 1, bn), lambda l, j: (l, 0, j))],
        out_specs=pl.BlockSpec((1, batch, bn), lambda l, j: (l, 0, j)),
        out_shape=jax.ShapeDtypeStruct((depth, batch, n), _F32),
        name="adaln_modulation",
    )(c, ada_w, ada_b.reshape(depth, 1, n))


def _retention_tables():
    log_g = np.log(1.0 - 2.0 ** (-5.0 - np.arange(RET_HEADS, dtype=np.float64)))
    pos = np.arange(CHUNK, dtype=np.float64)
    diff = pos[:, None] - pos[None, :]
    inner = np.where(diff >= 0, np.exp(log_g[:, None, None] * np.where(diff >= 0, diff, 0.0)), 0.0)
    q_decay = np.repeat(np.exp(log_g[:, None] * (pos + 1.0)).T, RET_DV, axis=1)
    k_decay = np.repeat(np.exp(log_g[:, None] * (CHUNK - 1.0 - pos)).T, RET_DK, axis=1)
    chunk_decay = np.repeat(np.exp(log_g * CHUNK), RET_DV)[None, :]
    block_diag = (np.arange(RET_QK_W)[:, None] // RET_DK == np.arange(RET_V_W)[None, :] // RET_DV)
    return tuple(jnp.asarray(t, _F32) for t in (inner, q_decay, k_decay, chunk_decay, block_diag))


def _retention_tile(p, o, inner_ref, qd_ref, kd_ref, cd_ref, bd_ref, state_ref):
    head_of_lane = lax.broadcasted_iota(jnp.int32, (CHUNK, RET_QK_W), 1) // RET_DK
    for ci in range(N_CHUNKS):
        rows = slice(ci * CHUNK, (ci + 1) * CHUNK)
        q = p[rows, 0:RET_QK_W]
        k = p[rows, RET_QK_W:2 * RET_QK_W].astype(_F32) * (RET_DK ** -0.5)
        k_bf = k.astype(_BF16)
        v = p[rows, 2 * RET_QK_W:2 * RET_QK_W + RET_V_W]
        state = state_ref[...]
        cross = _dot(q, state.astype(_BF16)) * qd_ref[...]
        update = _dot_tn((k * kd_ref[...]).astype(_BF16), v)
        state_ref[...] = state * cd_ref[...] + update * bd_ref[...]
        q_heads = jnp.concatenate([jnp.where(head_of_lane == hd, q, jnp.zeros_like(q)) for hd in range(RET_HEADS)],
                                  axis=0)
        all_scores = _dot_nt(q_heads, k_bf)
        for hd in range(RET_HEADS):
            lanes = slice(hd * RET_DV, (hd + 1) * RET_DV)
            scores = all_scores[hd * CHUNK:(hd + 1) * CHUNK, :] * inner_ref[hd]
            out = _dot(scores.astype(_BF16), v[:, lanes]) + cross[:, lanes]
            out = out * lax.rsqrt(jnp.mean(out * out, axis=-1, keepdims=True) + EPS)
            g0 = 2 * RET_QK_W + RET_V_W + hd * RET_DV
            gate = p[rows, g0:g0 + RET_DV].astype(_F32)
            o[rows, lanes] = (out * _silu(gate)).astype(_BF16)


def _swa_bias():
    slopes = 2.0 ** (-8.0 * np.arange(1, SWA_HQ + 1, dtype=np.float64) / SWA_HQ)
    qi = np.arange(CHUNK)
    kj = np.arange(2 * CHUNK)
    delta = CHUNK + qi[:, None] - kj[None, :]
    valid = (delta >= 0) & (delta < CHUNK)
    variants = []
    for ok in (valid, valid & (kj[None, :] >= CHUNK)):
        bias = np.where(ok[None], -slopes[:, None, None] * delta[None].astype(np.float64), -np.inf)
        variants.append(bias.reshape(SWA_HKV, SWA_GROUP * CHUNK, 2 * CHUNK))
    return jnp.asarray(np.stack(variants) * LOG2_E, _F32)


def _swa_sink_rows(sinks):
    rows = jnp.repeat(sinks.astype(_F32) * LOG2_E, CHUNK)[:, None]
    return jnp.broadcast_to(rows, (SWA_HQ * CHUNK, LANES)).reshape(SWA_HKV, SWA_GROUP * CHUNK, LANES)


def _swa_tile(first_tile, p, o, bias_ref, sink_ref, kt_buf, v_buf):
    low_half = lax.broadcasted_iota(jnp.int32, (ROW_TILE, SWA_KV_W), 1) < SWA_D
    k = p[:, SWA_Q_W:SWA_Q_W + SWA_KV_W].astype(_F32)
    k_swapped = pltpu.roll(k, SWA_D, axis=1)
    for j, dup in enumerate((jnp.where(low_half, k, k_swapped), jnp.where(low_half, k_swapped, k))):
        for bi in range(N_CHUNKS):
            kt_buf[j, :, (bi + 1) * CHUNK:(bi + 2) * CHUNK] = dup[bi * CHUNK:(bi + 1) * CHUNK, :].T.astype(_BF16)
    v = p[:, SWA_Q_W + SWA_KV_W:SWA_Q_W + 2 * SWA_KV_W].astype(_F32)
    v_swapped = pltpu.roll(v, SWA_D, axis=1)
    for j, dup in enumerate((jnp.where(low_half, v, v_swapped), jnp.where(low_half, v_swapped, v))):
        v_buf[CHUNK:CHUNK + ROW_TILE, j * SWA_EXT_W:j * SWA_EXT_W + LANES] = dup.astype(_BF16)
        v_buf[CHUNK:CHUNK + ROW_TILE, j * SWA_EXT_W + LANES:(j + 1) * SWA_EXT_W] = jnp.ones((ROW_TILE, LANES), _BF16)

    low_lanes = lax.broadcasted_iota(jnp.int32, (CHUNK, LANES), 1) < SWA_D
    high_lanes = jnp.logical_not(low_lanes)
    first_variant = first_tile.astype(jnp.int32)
    gate0 = SWA_Q_W + 2 * SWA_KV_W

    for bi in range(N_CHUNKS):
        rows = slice(bi * CHUNK, (bi + 1) * CHUNK)
        keys = slice(bi * CHUNK, (bi + 2) * CHUNK)
        variant = first_variant if bi == 0 else 0
        for j in range(SWA_HKV):
            q_heads = []
            for pair in range(SWA_GROUP // 2):
                c0 = (j * SWA_GROUP // 2 + pair) * LANES
                qp = p[rows, c0:c0 + LANES].astype(_F32) * (SWA_D ** -0.5 * LOG2_E)
                q_heads += [jnp.where(low_lanes, qp, 0.0).astype(_BF16), jnp.where(high_lanes, qp, 0.0).astype(_BF16)]
            scores = _dot(jnp.concatenate(q_heads, axis=0), kt_buf[j, :, keys])
            weights, shifts = [], []
            for g in range(SWA_GROUP):
                g_rows = slice(g * CHUNK, (g + 1) * CHUNK)
                s = scores[g_rows, :] + bias_ref[variant, j, g_rows, :]
                s_prev, s_cur = s[:, 0:CHUNK], s[:, CHUNK:2 * CHUNK]
                sink = sink_ref[j, g_rows, :]
                m = jnp.maximum(jnp.max(jnp.maximum(s_prev, s_cur), axis=-1, keepdims=True), sink)
                weights.append(jnp.concatenate([jnp.exp2(s_prev - m), jnp.exp2(s_cur - m)], axis=1).astype(_BF16))
                shifts.append(jnp.exp2(sink - m))
            ov = _dot(jnp.concatenate(weights, axis=0), v_buf[keys, j * SWA_EXT_W:(j + 1) * SWA_EXT_W])
            for pair in range(SWA_GROUP // 2):
                c0 = (j * SWA_GROUP // 2 + pair) * LANES
                halves = []
                for odd in range(2):
                    g = 2 * pair + odd
                    g_rows = slice(g * CHUNK, (g + 1) * CHUNK)
                    halves.append(ov[g_rows, 0:LANES] / (ov[g_rows, LANES:2 * LANES] + shifts[g]))
                gate = p[rows, gate0 + c0:gate0 + c0 + LANES].astype(_F32)
                o[rows, c0:c0 + LANES] = (jnp.where(low_lanes, halves[0], halves[1]) * _silu(gate)).astype(_BF16)

    kt_buf[:, :, 0:CHUNK] = kt_buf[:, :, ROW_TILE:ROW_TILE + CHUNK]
    v_buf[0:CHUNK, :] = v_buf[ROW_TILE:ROW_TILE + CHUNK, :]


def _pool_inverse_counts():
    pos = np.arange(ROW_TILE, dtype=np.float64)[:, None]
    windows = np.repeat(np.asarray(POOL_WINDOWS, np.float64), POOL_GD)[None, :]
    steady = np.broadcast_to(1.0 / windows, (ROW_TILE, BRANCH))
    start = 1.0 / np.minimum(pos + 1.0, windows)
    return jnp.asarray(np.stack([steady, start]), _F32)


def _pool_tile(first_tile, p, o, inv_ref, w_ref, scale_ref, hist, levels):
    hist[CHUNK:CHUNK + ROW_TILE, :] = p[:, 0:BRANCH].astype(_F32)
    inv = inv_ref[first_tile.astype(jnp.int32)]
    ext_rows = POOL_EXT + ROW_TILE
    base = CHUNK - POOL_EXT
    sums = []
    for g, w in enumerate(POOL_WINDOWS):
        c0 = g * POOL_GD
        shift = w // 2
        if g == 0:
            total = hist[base:base + ext_rows, :] + hist[base - shift:base - shift + ext_rows, :]
        else:
            prev = levels[g - 1]
            total = (prev[POOL_PAD:POOL_PAD + ext_rows, c0:]
                     + prev[POOL_PAD - shift:POOL_PAD - shift + ext_rows, c0:])
        if g + 1 < len(POOL_WINDOWS):
            levels[g][POOL_PAD:POOL_PAD + ext_rows, c0:] = total
        sums.append(total)
    for g in range(len(POOL_WINDOWS)):
        lanes = slice(g * POOL_GD, (g + 1) * POOL_GD)
        u = hist[CHUNK:CHUNK + ROW_TILE, lanes]
        acc = sums[g][POOL_EXT:, 0:POOL_GD]
        diff = acc * inv[:, lanes] - u
        y = _dot(diff.astype(_BF16), w_ref[g])
        gate = p[:, BRANCH + g * POOL_GD:BRANCH + (g + 1) * POOL_GD].astype(_F32)
        o[:, lanes] = (y * scale_ref[:, lanes] * _silu(gate)).astype(_BF16)
    hist[0:CHUNK, :] = hist[ROW_TILE:ROW_TILE + CHUNK, :]


def _expansion_matrix():
    e = np.zeros((2 * LANES, M_X_W), np.float32)
    for hd in range(M_HEADS):
        e[hd, hd * M_P:(hd + 1) * M_P] = 1.0
        e[LANES + hd, hd * M_P:(hd + 1) * M_P] = 1.0
    return jnp.asarray(e, _BF16)


def _pad_lanes(v, width=DT_PAD):
    return jnp.pad(v.astype(_F32), (0, width - v.shape[0])).reshape(1, width)


def _ssd_tile(p, dt_raw, o, cw_ref, cb_ref, dtb_ref, alog_ref, dskip_ref, nw_ref, tri_ref, exp_ref,
              hist, xbc_buf, state_ref):
    hist[CONV_HIST:CONV_HIST + ROW_TILE, :] = p[:, 0:M_CONV_W].astype(_F32)
    acc = cb_ref[...] + cw_ref[M_CONV - 1:M_CONV, :] * hist[CONV_HIST:CONV_HIST + ROW_TILE, :]
    for j in range(1, M_CONV):
        acc = acc + cw_ref[M_CONV - 1 - j:M_CONV - j, :] * hist[CONV_HIST - j:CONV_HIST - j + ROW_TILE, :]
    xbc_buf[...] = _silu(acc)
    hist[0:CONV_HIST, :] = hist[ROW_TILE:ROW_TILE + CONV_HIST, :]

    a_row = -jnp.exp(alog_ref[...])
    causal = (lax.broadcasted_iota(jnp.int32, (CHUNK, CHUNK), 0)
              >= lax.broadcasted_iota(jnp.int32, (CHUNK, CHUNK), 1))
    head_of_lane = lax.broadcasted_iota(jnp.int32, (CHUNK, M_GW), 1) // M_P

    def expand(vals):
        stacked = jnp.concatenate(vals, axis=0)
        hi = stacked.astype(_BF16)
        lo = (stacked - hi.astype(_F32)).astype(_BF16)
        wide = _dot(jnp.concatenate([hi, lo], axis=1), exp_ref[...])
        return [wide[i * CHUNK:(i + 1) * CHUNK, :] for i in range(len(vals))]

    for ci in range(N_CHUNKS):
        rows = slice(ci * CHUNK, (ci + 1) * CHUNK)
        dt = _softplus(dt_raw[rows, :] + dtb_ref[...])
        dta = dt * a_row
        hi = dta.astype(_BF16)
        rest = dta - hi.astype(_F32)
        mid = rest.astype(_BF16)
        lo = (rest - mid.astype(_F32)).astype(_BF16)
        acs = _dot(tri_ref[...], jnp.concatenate([hi, mid, lo], axis=0))
        acs_t = acs.T
        last = acs[CHUNK - 1:CHUNK, :]
        dt_x, w_x, ex_x = expand([dt, dt * jnp.exp(last - acs), jnp.exp(acs)])
        x = xbc_buf[rows, 0:M_X_W]
        xdt = (x * dt_x).astype(_BF16)
        xd = (x * w_x).astype(_BF16)
        for g in range(M_GROUPS):
            lanes = slice(g * M_GW, (g + 1) * M_GW)
            bmat = xbc_buf[rows, M_X_W + g * M_N:M_X_W + (g + 1) * M_N].astype(_BF16)
            cmat = xbc_buf[rows, M_X_W + M_BC + g * M_N:M_X_W + M_BC + (g + 1) * M_N].astype(_BF16)
            cb = _dot_nt(cmat, bmat)
            state = state_ref[g]
            ex_g = ex_x[:, lanes]
            y = _dot(cmat, state.astype(_BF16)) * ex_g + x[:, lanes] * dskip_ref[:, lanes]
            xdt_g = xdt[:, lanes]
            decays, sources = [], []
            for r in range(M_R):
                hd = g * M_R + r
                seg = acs[:, hd:hd + 1] - acs_t[hd:hd + 1, :]
                lmat = jnp.exp(jnp.where(causal, seg, -jnp.inf))
                decays.append((cb * lmat).astype(_BF16))
                sources.append(jnp.where(head_of_lane == r, xdt_g, jnp.zeros_like(xdt_g)))
            y = y + _dot(jnp.concatenate(decays, axis=1), jnp.concatenate(sources, axis=0))
            state_ref[g] = state * ex_g[CHUNK - 1:CHUNK, :] + _dot_tn(bmat, xd[:, lanes])
            z = p[rows, M_CONV_W + g * M_GW:M_CONV_W + (g + 1) * M_GW].astype(_F32)
            yz = y * _silu(z)
            yz = yz * lax.rsqrt(jnp.mean(yz * yz, axis=-1, keepdims=True) + EPS)
            o[rows, lanes] = (yz * nw_ref[:, lanes]).astype(_BF16)


N_RET_T, N_SWA_T, N_POOL_T, N_SSD_T = 5, 2, 3, 8


def _mixers_kernel(x_ref, mod_ref, nw_ref, wr_ref, ws_ref, wp_ref, wm_ref, wdt_ref, *rest):
    consts, rest = rest[:N_RET_T + N_SWA_T + N_POOL_T + N_SSD_T], rest[N_RET_T + N_SWA_T + N_POOL_T + N_SSD_T:]
    ret_c, consts = consts[:N_RET_T], consts[N_RET_T:]
    swa_c, consts = consts[:N_SWA_T], consts[N_SWA_T:]
    pool_c, ssd_c = consts[:N_POOL_T], consts[N_POOL_T:]
    ret_o, att_o, pool_o, ssm_o, h_o = (r.at[0] for r in rest[:N_BRANCHES + 1])
    proj = rest[N_BRANCHES + 1:N_BRANCHES + 6]
    ret_state, kt_buf, v_buf, pool_hist, conv_hist, xbc_buf, ssd_state = rest[N_BRANCHES + 6:N_BRANCHES + 13]
    pool_levels = rest[N_BRANCHES + 13:]
    first_tile = pl.program_id(1) == 0

    @pl.when(first_tile)
    def _():
        ret_state[...] = jnp.zeros_like(ret_state)
        ssd_state[...] = jnp.zeros_like(ssd_state)
        kt_buf[:, :, 0:CHUNK] = jnp.zeros((SWA_HKV, LANES, CHUNK), _BF16)
        v_buf[0:CHUNK, :] = jnp.zeros((CHUNK, SWA_HKV * SWA_EXT_W), _BF16)
        pool_hist[0:CHUNK, :] = jnp.zeros((CHUNK, BRANCH), _F32)
        for level in pool_levels:
            level[0:POOL_PAD, :] = jnp.zeros((POOL_PAD, BRANCH), _F32)
        conv_hist[0:CONV_HIST, :] = jnp.zeros((CONV_HIST, M_CONV_W), _F32)

    h = _modulated_norm(x_ref[0], mod_ref, nw_ref).astype(_BF16)
    h_o[...] = h
    for ref, w_ref in zip(proj, (wr_ref.at[0], ws_ref.at[0], wp_ref.at[0], wm_ref.at[0], wdt_ref)):
        ref[...] = _dot(h, w_ref[...]).astype(ref.dtype)

    ret_p, swa_p, pool_p, ssd_p, dt_p = proj
    _retention_tile(ret_p, ret_o, *ret_c, ret_state)
    _swa_tile(first_tile, swa_p, att_o, *swa_c, kt_buf, v_buf)
    _pool_tile(first_tile, pool_p, pool_o, *pool_c, pool_hist, pool_levels)
    _ssd_tile(ssd_p, dt_p, ssm_o, *ssd_c, conv_hist, xbc_buf, ssd_state)


def _mixers(layer, x, mod, norm_w, w_in, w_dt, swa_sinks, pool_w, pool_scale, conv_w, conv_b, dt_bias, a_log,
            d_skip, ssm_norm_w):
    batch, seq, _ = x.shape
    windows = (RET_COLS, SWA_COLS, POOL_COLS, SSD_COLS)
    consts = list(_retention_tables())
    consts += [_swa_bias(), _swa_sink_rows(swa_sinks)]
    consts += [_pool_inverse_counts(), pool_w.astype(_BF16), pool_scale.reshape(1, BRANCH)]
    consts += [conv_w, conv_b.reshape(1, M_CONV_W), _pad_lanes(dt_bias), _pad_lanes(a_log),
               jnp.repeat(d_skip.astype(_F32), M_P).reshape(1, BRANCH), ssm_norm_w.reshape(1, BRANCH),
               jnp.asarray(np.tile(np.tril(np.ones((CHUNK, CHUNK), np.float32)), (1, 3)), _BF16),
               _expansion_matrix()]
    assert len(consts) == N_RET_T + N_SWA_T + N_POOL_T + N_SSD_T
    operands = [x, mod, norm_w] + [w_in] * len(windows) + [w_dt, *consts]
    in_specs = [_row_spec(D_MODEL), _mod_spec(), _const_spec(norm_w.shape)]
    in_specs += [_weight_window_spec(layer, cols) for cols in windows]
    in_specs += [_const_spec(a.shape) for a in [w_dt, *consts]]
    scratch = [pltpu.VMEM((ROW_TILE, w), _BF16) for w in MIXER_WIDTHS] + [pltpu.VMEM((ROW_TILE, DT_PAD), _F32)]
    scratch += [pltpu.VMEM((RET_QK_W, RET_V_W), _F32),
                pltpu.VMEM((SWA_HKV, LANES, CHUNK + ROW_TILE), _BF16),
                pltpu.VMEM((CHUNK + ROW_TILE, SWA_HKV * SWA_EXT_W), _BF16),
                pltpu.VMEM((CHUNK + ROW_TILE, BRANCH), _F32),
                pltpu.VMEM((CONV_HIST + ROW_TILE, M_CONV_W), _F32),
                pltpu.VMEM((ROW_TILE, M_CONV_W), _F32),
                pltpu.VMEM((M_GROUPS, M_N, M_GW), _F32)]
    scratch += [pltpu.VMEM((POOL_PAD + POOL_EXT + ROW_TILE, BRANCH), _F32)] * (len(POOL_WINDOWS) - 1)
    return pl.pallas_call(
        _mixers_kernel,
        grid=(batch, seq // ROW_TILE),
        in_specs=in_specs,
        out_specs=[_row_spec(BRANCH)] * N_BRANCHES + [_row_spec(D_MODEL)],
        out_shape=[jax.ShapeDtypeStruct((batch, seq, BRANCH), _BF16)] * N_BRANCHES
        + [jax.ShapeDtypeStruct((batch, seq, D_MODEL), _BF16)],
        scratch_shapes=scratch,
        compiler_params=_params(),
        name="norm_proj_mixers",
    )(*operands)


def _merge_kernel(final, x_ref, mod_ref, h_ref, wg_ref, ret_ref, att_ref, pool_ref, ssm_ref, wup_ref, wout_ref,
                  *rest):
    if final:
        fnw_ref, o_ref = rest
    else:
        (o_ref,) = rest
    h = h_ref[0]
    merged = None
    for i, br_ref in enumerate((ret_ref, att_ref, pool_ref, ssm_ref)):
        gate = _sigmoid(_dot(h, wg_ref[:, i * D_MODEL:(i + 1) * D_MODEL]))
        term = gate * _dot(br_ref[0], wup_ref[i])
        merged = term if merged is None else merged + term
    out = _dot(merged.astype(_BF16), wout_ref[...])
    y = x_ref[0] + mod_ref[0, :, 2 * D_MODEL:3 * D_MODEL] * out
    if final:
        y = y * lax.rsqrt(jnp.mean(y * y, axis=-1, keepdims=True) + EPS) * fnw_ref[...]
    o_ref[0] = y


def _merge(x, mod, h, wg, branches, w_up, w_out, final_norm_w):
    batch, seq, _ = x.shape
    final = final_norm_w is not None
    in_specs = [_row_spec(D_MODEL, MERGE_TILE), _mod_spec(), _row_spec(D_MODEL, MERGE_TILE), _const_spec(wg.shape)]
    in_specs += [_row_spec(BRANCH, MERGE_TILE)] * N_BRANCHES
    in_specs += [_const_spec(w_up.shape), _const_spec(w_out.shape)]
    args = [x, mod, h, wg, *branches, w_up, w_out]
    if final:
        in_specs.append(_const_spec((1, D_MODEL)))
        args.append(final_norm_w.reshape(1, D_MODEL))
    return pl.pallas_call(
        functools.partial(_merge_kernel, final),
        grid=(batch, seq // MERGE_TILE),
        in_specs=in_specs,
        out_specs=_row_spec(D_MODEL, MERGE_TILE),
        out_shape=jax.ShapeDtypeStruct(x.shape, _F32),
        compiler_params=_params(),
        name="merge_out_proj",
    )(*args)


def kernel(x, c, ada_w, ada_b, norm_w, w_in, swa_sinks, pool_w, pool_scale, conv_w, conv_b, dt_bias, a_log,
           d_skip, ssm_norm_w, w_up, w_out, final_norm_w):
    depth = ada_w.shape[0]
    batch = x.shape[0]
    mod_all = _modulation(c, ada_w, ada_b)
    w_in_bf = w_in.astype(_BF16)
    for l in range(depth):
        mod = mod_all[l].reshape(batch, 1, 3 * D_MODEL)
        nw = norm_w[l].reshape(1, D_MODEL)
        w_dt = jnp.pad(w_in_bf[l, :, DT_COLS[0]:DT_COLS[1]], ((0, 0), (0, DT_PAD - M_HEADS)))
        *branches, h = _mixers(l, x, mod, nw, w_in_bf, w_dt, swa_sinks[l], pool_w[l], pool_scale[l], conv_w[l],
                               conv_b[l], dt_bias[l], a_log[l], d_skip[l], ssm_norm_w[l])
        wg = w_in_bf[l, :, MG_COLS[0]:MG_COLS[1]]
        x = _merge(x, mod, h, wg, branches, w_up[l].astype(_BF16), w_out[l].astype(_BF16),
                   final_norm_w if l == depth - 1 else None)
    return x
```

```python
import functools

import jax
import jax.numpy as jnp
import numpy as np
from jax import lax
from jax.experimental import pallas as pl
from jax.experimental.pallas import tpu as pltpu

D_MODEL = 1024
BRANCH = 512
N_BRANCHES = 4
EPS = 1e-6
CHUNK = 128
LANES = 128

RET_HEADS, RET_DK, RET_DV = 4, 64, 128
RET_QK_W = RET_HEADS * RET_DK
RET_V_W = RET_HEADS * RET_DV
SWA_HQ, SWA_HKV, SWA_D = 8, 2, 64
SWA_GROUP = SWA_HQ // SWA_HKV
SWA_Q_W = SWA_HQ * SWA_D
SWA_KV_W = SWA_HKV * SWA_D
SWA_EXT_W = 2 * LANES
LOG2_E = 1.4426950408889634
POOL_WINDOWS = (2, 4, 8, 16)
POOL_GD = BRANCH // len(POOL_WINDOWS)
POOL_EXT = 16
POOL_PAD = 8
M_HEADS, M_P, M_GROUPS, M_N, M_CONV = 8, 64, 2, 128, 4
M_R = M_HEADS // M_GROUPS
M_GW = M_R * M_P
M_BC = M_GROUPS * M_N
M_X_W = M_HEADS * M_P
M_CONV_W = M_X_W + 2 * M_BC
CONV_HIST = 8
DT_PAD = LANES

_SIZES = (256, 256, 512, 512, 512, 128, 128, 512, 512, 512, 1024, 512, M_HEADS, N_BRANCHES * D_MODEL)
_OFFS = np.concatenate([[0], np.cumsum(_SIZES)]).tolist()
RET_COLS = (_OFFS[0], _OFFS[4])
SWA_COLS = (_OFFS[4], _OFFS[8])
POOL_COLS = (_OFFS[8], _OFFS[10])
SSD_COLS = (_OFFS[10], _OFFS[12])
DT_COLS = (_OFFS[12], _OFFS[13])
MG_COLS = (_OFFS[13], _OFFS[14])
MIXER_WIDTHS = tuple(b - a for a, b in (RET_COLS, SWA_COLS, POOL_COLS, SSD_COLS))

ROW_TILE = 512
MERGE_TILE = 1024
N_CHUNKS = ROW_TILE // CHUNK
V7X_VMEM_LIMIT = 56 * 1024 * 1024

_F32 = jnp.float32
_BF16 = jnp.bfloat16


def _sigmoid(x):
    return 0.5 * jnp.tanh(0.5 * x) + 0.5


def _silu(x):
    half = 0.5 * x
    return half + half * jnp.tanh(half)


def _softplus(x):
    return jnp.maximum(x, 0.0) + jnp.log(1.0 + jnp.exp(-jnp.abs(x)))


def _dot(a, b):
    return jnp.dot(a, b, preferred_element_type=_F32)


def _dot_nt(a, b):
    return lax.dot_general(a, b, (((1,), (1,)), ((), ())), preferred_element_type=_F32)


def _dot_tn(a, b):
    return lax.dot_general(a, b, (((0,), (0,)), ((), ())), preferred_element_type=_F32)


def _const_spec(shape):
    nd = len(shape)
    return pl.BlockSpec(shape, lambda *_: (0,) * nd, pipeline_mode=pl.Buffered(1))


def _weight_window_spec(layer, cols):
    return pl.BlockSpec((pl.Element(1), pl.Element(D_MODEL), pl.Element(cols[1] - cols[0])),
                        lambda *_: (layer, 0, cols[0]), pipeline_mode=pl.Buffered(1))


def _row_spec(width, tile=None):
    return pl.BlockSpec((1, tile or ROW_TILE, width), lambda b, t: (b, t, 0))


def _mod_spec():
    return pl.BlockSpec((1, 1, 3 * D_MODEL), lambda b, t: (b, 0, 0))


def _params():
    return pltpu.CompilerParams(dimension_semantics=("parallel", "arbitrary"),
                                vmem_limit_bytes=V7X_VMEM_LIMIT)


def _modulated_norm(x, mod_ref, nw_ref):
    shift = mod_ref[0, :, 0:D_MODEL]
    scale = mod_ref[0, :, D_MODEL:2 * D_MODEL]
    rs = lax.rsqrt(jnp.mean(x * x, axis=-1, keepdims=True) + EPS)
    return (x * rs) * nw_ref[...] * (1.0 + scale) + shift


def _mod_kernel(c_ref, w_ref, b_ref, o_ref):
    sc = _silu(c_ref[...]).astype(_BF16)
    o_ref[0] = _dot(sc, w_ref[0].astype(_BF16)) + b_ref[0]


def _modulation(c, ada_w, ada_b):
    depth, _, n = ada_w.shape
    bn = D_MODEL
    batch = c.shape[0]
    return pl.pallas_call(
        _mod_kernel,
        grid=(depth, n // bn),
        in_specs=[pl.BlockSpec((batch, D_MODEL), lambda l, j: (0, 0)),
                  pl.BlockSpec((1, D_MODEL, bn), lambda l, j: (l, 0, j)),
                  pl.BlockSpec((1, 1, bn), lambda l, j: (l, 0, j))],
        out_specs=pl.BlockSpec((1, batch, bn), lambda l, j: (l, 0, j)),
        out_shape=jax.ShapeDtypeStruct((depth, batch, n), _F32),
        name="adaln_modulation",
    )(c, ada_w, ada_b.reshape(depth, 1, n))


def _retention_tables():
    log_g = np.log(1.0 - 2.0 ** (-5.0 - np.arange(RET_HEADS, dtype=np.float64)))
    pos = np.arange(CHUNK, dtype=np.float64)
    diff = pos[:, None] - pos[None, :]
    inner = np.where(diff >= 0, np.exp(log_g[:, None, None] * np.where(diff >= 0, diff, 0.0)), 0.0)
    q_decay = np.repeat(np.exp(log_g[:, None] * (pos + 1.0)).T, RET_DV, axis=1)
    k_decay = np.repeat(np.exp(log_g[:, None] * (CHUNK - 1.0 - pos)).T, RET_DK, axis=1)
    chunk_decay = np.repeat(np.exp(log_g * CHUNK), RET_DV)[None, :]
    block_diag = (np.arange(RET_QK_W)[:, None] // RET_DK == np.arange(RET_V_W)[None, :] // RET_DV)
    return tuple(jnp.asarray(t, _F32) for t in (inner, q_decay, k_decay, chunk_decay, block_diag))


def _retention_tile(p, o, inner_ref, qd_ref, kd_ref, cd_ref, bd_ref, state_ref):
    head_of_lane = lax.broadcasted_iota(jnp.int32, (CHUNK, RET_QK_W), 1) // RET_DK
    for ci in range(N_CHUNKS):
        rows = slice(ci * CHUNK, (ci + 1) * CHUNK)
        q = p[rows, 0:RET_QK_W]
        k = p[rows, RET_QK_W:2 * RET_QK_W].astype(_F32) * (RET_DK ** -0.5)
        k_bf = k.astype(_BF16)
        v = p[rows, 2 * RET_QK_W:2 * RET_QK_W + RET_V_W]
        state = state_ref[...]
        cross = _dot(q, state.astype(_BF16)) * qd_ref[...]
        update = _dot_tn((k * kd_ref[...]).astype(_BF16), v)
        state_ref[...] = state * cd_ref[...] + update * bd_ref[...]
        q_heads = jnp.concatenate([jnp.where(head_of_lane == hd, q, jnp.zeros_like(q)) for hd in range(RET_HEADS)],
                                  axis=0)
        all_scores = _dot_nt(q_heads, k_bf)
        for hd in range(RET_HEADS):
            lanes = slice(hd * RET_DV, (hd + 1) * RET_DV)
            scores = all_scores[hd * CHUNK:(hd + 1) * CHUNK, :] * inner_ref[hd]
            out = _dot(scores.astype(_BF16), v[:, lanes]) + cross[:, lanes]
            out = out * lax.rsqrt(jnp.mean(out * out, axis=-1, keepdims=True) + EPS)
            g0 = 2 * RET_QK_W + RET_V_W + hd * RET_DV
            gate = p[rows, g0:g0 + RET_DV].astype(_F32)
            o[rows, lanes] = (out * _silu(gate)).astype(_BF16)


def _swa_bias():
    slopes = 2.0 ** (-8.0 * np.arange(1, SWA_HQ + 1, dtype=np.float64) / SWA_HQ)
    qi = np.arange(CHUNK)
    kj = np.arange(2 * CHUNK)
    delta = CHUNK + qi[:, None] - kj[None, :]
    valid = (delta >= 0) & (delta < CHUNK)
    variants = []
    for ok in (valid, valid & (kj[None, :] >= CHUNK)):
        bias = np.where(ok[None], -slopes[:, None, None] * delta[None].astype(np.float64), -np.inf)
        variants.append(bias.reshape(SWA_HKV, SWA_GROUP * CHUNK, 2 * CHUNK))
    return jnp.asarray(np.stack(variants) * LOG2_E, _F32)


def _swa_sink_rows(sinks):
    rows = jnp.repeat(sinks.astype(_F32) * LOG2_E, CHUNK)[:, None]
    return jnp.broadcast_to(rows, (SWA_HQ * CHUNK, LANES)).reshape(SWA_HKV, SWA_GROUP * CHUNK, LANES)


def _swa_tile(first_tile, p, o, bias_ref, sink_ref, kt_buf, v_buf):
    low_half = lax.broadcasted_iota(jnp.int32, (ROW_TILE, SWA_KV_W), 1) < SWA_D
    k = p[:, SWA_Q_W:SWA_Q_W + SWA_KV_W].astype(_F32)
    k_swapped = pltpu.roll(k, SWA_D, axis=1)
    for j, dup in enumerate((jnp.where(low_half, k, k_swapped), jnp.where(low_half, k_swapped, k))):
        for bi in range(N_CHUNKS):
            kt_buf[j, :, (bi + 1) * CHUNK:(bi + 2) * CHUNK] = dup[bi * CHUNK:(bi + 1) * CHUNK, :].T.astype(_BF16)
    v = p[:, SWA_Q_W + SWA_KV_W:SWA_Q_W + 2 * SWA_KV_W].astype(_F32)
    v_swapped = pltpu.roll(v, SWA_D, axis=1)
    for j, dup in enumerate((jnp.where(low_half, v, v_swapped), jnp.where(low_half, v_swapped, v))):
        v_buf[CHUNK:CHUNK + ROW_TILE, j * SWA_EXT_W:j * SWA_EXT_W + LANES] = dup.astype(_BF16)
        v_buf[CHUNK:CHUNK + ROW_TILE, j * SWA_EXT_W + LANES:(j + 1) * SWA_EXT_W] = jnp.ones((ROW_TILE, LANES), _BF16)

    low_lanes = lax.broadcasted_iota(jnp.int32, (CHUNK, LANES), 1) < SWA_D
    high_lanes = jnp.logical_not(low_lanes)
    first_variant = first_tile.astype(jnp.int32)
    gate0 = SWA_Q_W + 2 * SWA_KV_W

    for bi in range(N_CHUNKS):
        rows = slice(bi * CHUNK, (bi + 1) * CHUNK)
        keys = slice(bi * CHUNK, (bi + 2) * CHUNK)
        variant = first_variant if bi == 0 else 0
        for j in range(SWA_HKV):
            q_heads = []
            for pair in range(SWA_GROUP // 2):
                c0 = (j * SWA_GROUP // 2 + pair) * LANES
                qp = p[rows, c0:c0 + LANES].astype(_F32) * (SWA_D ** -0.5 * LOG2_E)
                q_heads += [jnp.where(low_lanes, qp, 0.0).astype(_BF16), jnp.where(high_lanes, qp, 0.0).astype(_BF16)]
            scores = _dot(jnp.concatenate(q_heads, axis=0), kt_buf[j, :, keys])
            weights, shifts = [], []
            for g in range(SWA_GROUP):
                g_rows = slice(g * CHUNK, (g + 1) * CHUNK)
                s = scores[g_rows, :] + bias_ref[variant, j, g_rows, :]
                s_prev, s_cur = s[:, 0:CHUNK], s[:, CHUNK:2 * CHUNK]
                sink = sink_ref[j, g_rows, :]
                m = jnp.maximum(jnp.max(jnp.maximum(s_prev, s_cur), axis=-1, keepdims=True), sink)
                weights.append(jnp.concatenate([jnp.exp2(s_prev - m), jnp.exp2(s_cur - m)], axis=1).astype(_BF16))
                shifts.append(jnp.exp2(sink - m))
            ov = _dot(jnp.concatenate(weights, axis=0), v_buf[keys, j * SWA_EXT_W:(j + 1) * SWA_EXT_W])
            for pair in range(SWA_GROUP // 2):
                c0 = (j * SWA_GROUP // 2 + pair) * LANES
                halves = []
                for odd in range(2):
                    g = 2 * pair + odd
                    g_rows = slice(g * CHUNK, (g + 1) * CHUNK)
                    halves.append(ov[g_rows, 0:LANES] / (ov[g_rows, LANES:2 * LANES] + shifts[g]))
                gate = p[rows, gate0 + c0:gate0 + c0 + LANES].astype(_F32)
                o[rows, c0:c0 + LANES] = (jnp.where(low_lanes, halves[0], halves[1]) * _silu(gate)).astype(_BF16)

    kt_buf[:, :, 0:CHUNK] = kt_buf[:, :, ROW_TILE:ROW_TILE + CHUNK]
    v_buf[0:CHUNK, :] = v_buf[ROW_TILE:ROW_TILE + CHUNK, :]


def _pool_inverse_counts():
    pos = np.arange(ROW_TILE, dtype=np.float64)[:, None]
    windows = np.repeat(np.asarray(POOL_WINDOWS, np.float64), POOL_GD)[None, :]
    steady = np.broadcast_to(1.0 / windows, (ROW_TILE, BRANCH))
    start = 1.0 / np.minimum(pos + 1.0, windows)
    return jnp.asarray(np.stack([steady, start]), _F32)


def _pool_tile(first_tile, p, o, inv_ref, w_ref, scale_ref, hist, levels):
    hist[CHUNK:CHUNK + ROW_TILE, :] = p[:, 0:BRANCH].astype(_F32)
    inv = inv_ref[first_tile.astype(jnp.int32)]
    ext_rows = POOL_EXT + ROW_TILE
    base = CHUNK - POOL_EXT
    sums = []
    for g, w in enumerate(POOL_WINDOWS):
        c0 = g * POOL_GD
        shift = w // 2
        if g == 0:
            total = hist[base:base + ext_rows, :] + hist[base - shift:base - shift + ext_rows, :]
        else:
            prev = levels[g - 1]
            total = (prev[POOL_PAD:POOL_PAD + ext_rows, c0:]
                     + prev[POOL_PAD - shift:POOL_PAD - shift + ext_rows, c0:])
        if g + 1 < len(POOL_WINDOWS):
            levels[g][POOL_PAD:POOL_PAD + ext_rows, c0:] = total
        sums.append(total)
    for g in range(len(POOL_WINDOWS)):
        lanes = slice(g * POOL_GD, (g + 1) * POOL_GD)
        u = hist[CHUNK:CHUNK + ROW_TILE, lanes]
        acc = sums[g][POOL_EXT:, 0:POOL_GD]
        diff = acc * inv[:, lanes] - u
        y = _dot(diff.astype(_BF16), w_ref[g])
        gate = p[:, BRANCH + g * POOL_GD:BRANCH + (g + 1) * POOL_GD].astype(_F32)
        o[:, lanes] = (y * scale_ref[:, lanes] * _silu(gate)).astype(_BF16)
    hist[0:CHUNK, :] = hist[ROW_TILE:ROW_TILE + CHUNK, :]


def _expansion_matrix():
    e = np.zeros((2 * LANES, M_X_W), np.float32)
    for hd in range(M_HEADS):
        e[hd, hd * M_P:(hd + 1) * M_P] = 1.0
        e[LANES + hd, hd * M_P:(hd + 1) * M_P] = 1.0
    return jnp.asarray(e, _BF16)


def _pad_lanes(v, width=DT_PAD):
    return jnp.pad(v.astype(_F32), (0, width - v.shape[0])).reshape(1, width)


def _ssd_tile(p, dt_raw, o, cw_ref, cb_ref, dtb_ref, alog_ref, dskip_ref, nw_ref, tri_ref, exp_ref,
              hist, xbc_buf, state_ref):
    hist[CONV_HIST:CONV_HIST + ROW_TILE, :] = p[:, 0:M_CONV_W].astype(_F32)
    acc = cb_ref[...] + cw_ref[M_CONV - 1:M_CONV, :] * hist[CONV_HIST:CONV_HIST + ROW_TILE, :]
    for j in range(1, M_CONV):
        acc = acc + cw_ref[M_CONV - 1 - j:M_CONV - j, :] * hist[CONV_HIST - j:CONV_HIST - j + ROW_TILE, :]
    xbc_buf[...] = _silu(acc)
    hist[0:CONV_HIST, :] = hist[ROW_TILE:ROW_TILE + CONV_HIST, :]

    a_row = -jnp.exp(alog_ref[...])
    causal = (lax.broadcasted_iota(jnp.int32, (CHUNK, CHUNK), 0)
              >= lax.broadcasted_iota(jnp.int32, (CHUNK, CHUNK), 1))
    head_of_lane = lax.broadcasted_iota(jnp.int32, (CHUNK, M_GW), 1) // M_P

    def expand(vals):
        stacked = jnp.concatenate(vals, axis=0)
        hi = stacked.astype(_BF16)
        lo = (stacked - hi.astype(_F32)).astype(_BF16)
        wide = _dot(jnp.concatenate([hi, lo], axis=1), exp_ref[...])
        return [wide[i * CHUNK:(i + 1) * CHUNK, :] for i in range(len(vals))]

    tables, log_decays = [], []
    for ci in range(N_CHUNKS):
        rows = slice(ci * CHUNK, (ci + 1) * CHUNK)
        dt = _softplus(dt_raw[rows, :] + dtb_ref[...])
        dta = dt * a_row
        hi = dta.astype(_BF16)
        rest = dta - hi.astype(_F32)
        mid = rest.astype(_BF16)
        lo = (rest - mid.astype(_F32)).astype(_BF16)
        acs = _dot(tri_ref[...], jnp.concatenate([hi, mid, lo], axis=0))
        log_decays.append(acs)
        tables += [dt, dt * jnp.exp(acs[CHUNK - 1:CHUNK, :] - acs), jnp.exp(acs)]
    wide = expand(tables)

    for ci in range(N_CHUNKS):
        rows = slice(ci * CHUNK, (ci + 1) * CHUNK)
        acs = log_decays[ci]
        acs_t = acs.T
        dt_x, w_x, ex_x = wide[3 * ci:3 * ci + 3]
        x = xbc_buf[rows, 0:M_X_W]
        xdt = (x * dt_x).astype(_BF16)
        xd = (x * w_x).astype(_BF16)
        for g in range(M_GROUPS):
            lanes = slice(g * M_GW, (g + 1) * M_GW)
            bmat = xbc_buf[rows, M_X_W + g * M_N:M_X_W + (g + 1) * M_N].astype(_BF16)
            cmat = xbc_buf[rows, M_X_W + M_BC + g * M_N:M_X_W + M_BC + (g + 1) * M_N].astype(_BF16)
            cb = _dot_nt(cmat, bmat)
            state = state_ref[g]
            ex_g = ex_x[:, lanes]
            y = _dot(cmat, state.astype(_BF16)) * ex_g + x[:, lanes] * dskip_ref[:, lanes]
            xdt_g = xdt[:, lanes]
            decays, sources = [], []
            for r in range(M_R):
                hd = g * M_R + r
                seg = acs[:, hd:hd + 1] - acs_t[hd:hd + 1, :]
                lmat = jnp.exp(jnp.where(causal, seg, -jnp.inf))
                decays.append((cb * lmat).astype(_BF16))
                sources.append(jnp.where(head_of_lane == r, xdt_g, jnp.zeros_like(xdt_g)))
            y = y + _dot(jnp.concatenate(decays, axis=1), jnp.concatenate(sources, axis=0))
            state_ref[g] = state * ex_g[CHUNK - 1:CHUNK, :] + _dot_tn(bmat, xd[:, lanes])
            z = p[rows, M_CONV_W + g * M_GW:M_CONV_W + (g + 1) * M_GW].astype(_F32)
            yz = y * _silu(z)
            yz = yz * lax.rsqrt(jnp.mean(yz * yz, axis=-1, keepdims=True) + EPS)
            o[rows, lanes] = (yz * nw_ref[:, lanes]).astype(_BF16)


N_RET_T, N_SWA_T, N_POOL_T, N_SSD_T = 5, 2, 3, 8


def _mixers_kernel(x_ref, mod_ref, nw_ref, wr_ref, ws_ref, wp_ref, wm_ref, wdt_ref, *rest):
    consts, rest = rest[:N_RET_T + N_SWA_T + N_POOL_T + N_SSD_T], rest[N_RET_T + N_SWA_T + N_POOL_T + N_SSD_T:]
    ret_c, consts = consts[:N_RET_T], consts[N_RET_T:]
    swa_c, consts = consts[:N_SWA_T], consts[N_SWA_T:]
    pool_c, ssd_c = consts[:N_POOL_T], consts[N_POOL_T:]
    ret_o, att_o, pool_o, ssm_o, h_o = (r.at[0] for r in rest[:N_BRANCHES + 1])
    proj = rest[N_BRANCHES + 1:N_BRANCHES + 6]
    ret_state, kt_buf, v_buf, pool_hist, conv_hist, xbc_buf, ssd_state = rest[N_BRANCHES + 6:N_BRANCHES + 13]
    pool_levels = rest[N_BRANCHES + 13:]
    first_tile = pl.program_id(1) == 0

    @pl.when(first_tile)
    def _():
        ret_state[...] = jnp.zeros_like(ret_state)
        ssd_state[...] = jnp.zeros_like(ssd_state)
        kt_buf[:, :, 0:CHUNK] = jnp.zeros((SWA_HKV, LANES, CHUNK), _BF16)
        v_buf[0:CHUNK, :] = jnp.zeros((CHUNK, SWA_HKV * SWA_EXT_W), _BF16)
        pool_hist[0:CHUNK, :] = jnp.zeros((CHUNK, BRANCH), _F32)
        for level in pool_levels:
            level[0:POOL_PAD, :] = jnp.zeros((POOL_PAD, BRANCH), _F32)
        conv_hist[0:CONV_HIST, :] = jnp.zeros((CONV_HIST, M_CONV_W), _F32)

    h = _modulated_norm(x_ref[0], mod_ref, nw_ref).astype(_BF16)
    h_o[...] = h
    for ref, w_ref in zip(proj, (wr_ref.at[0], ws_ref.at[0], wp_ref.at[0], wm_ref.at[0], wdt_ref)):
        ref[...] = _dot(h, w_ref[...]).astype(ref.dtype)

    ret_p, swa_p, pool_p, ssd_p, dt_p = proj
    _retention_tile(ret_p, ret_o, *ret_c, ret_state)
    _swa_tile(first_tile, swa_p, att_o, *swa_c, kt_buf, v_buf)
    _pool_tile(first_tile, pool_p, pool_o, *pool_c, pool_hist, pool_levels)
    _ssd_tile(ssd_p, dt_p, ssm_o, *ssd_c, conv_hist, xbc_buf, ssd_state)


def _mixers(layer, x, mod, norm_w, w_in, w_dt, swa_sinks, pool_w, pool_scale, conv_w, conv_b, dt_bias, a_log,
            d_skip, ssm_norm_w):
    batch, seq, _ = x.shape
    windows = (RET_COLS, SWA_COLS, POOL_COLS, SSD_COLS)
    consts = list(_retention_tables())
    consts += [_swa_bias(), _swa_sink_rows(swa_sinks)]
    consts += [_pool_inverse_counts(), pool_w.astype(_BF16), pool_scale.reshape(1, BRANCH)]
    consts += [conv_w, conv_b.reshape(1, M_CONV_W), _pad_lanes(dt_bias), _pad_lanes(a_log),
               jnp.repeat(d_skip.astype(_F32), M_P).reshape(1, BRANCH), ssm_norm_w.reshape(1, BRANCH),
               jnp.asarray(np.tile(np.tril(np.ones((CHUNK, CHUNK), np.float32)), (1, 3)), _BF16),
               _expansion_matrix()]
    assert len(consts) == N_RET_T + N_SWA_T + N_POOL_T + N_SSD_T
    operands = [x, mod, norm_w] + [w_in] * len(windows) + [w_dt, *consts]
    in_specs = [_row_spec(D_MODEL), _mod_spec(), _const_spec(norm_w.shape)]
    in_specs += [_weight_window_spec(layer, cols) for cols in windows]
    in_specs += [_const_spec(a.shape) for a in [w_dt, *consts]]
    scratch = [pltpu.VMEM((ROW_TILE, w), _BF16) for w in MIXER_WIDTHS] + [pltpu.VMEM((ROW_TILE, DT_PAD), _F32)]
    scratch += [pltpu.VMEM((RET_QK_W, RET_V_W), _F32),
                pltpu.VMEM((SWA_HKV, LANES, CHUNK + ROW_TILE), _BF16),
                pltpu.VMEM((CHUNK + ROW_TILE, SWA_HKV * SWA_EXT_W), _BF16),
                pltpu.VMEM((CHUNK + ROW_TILE, BRANCH), _F32),
                pltpu.VMEM((CONV_HIST + ROW_TILE, M_CONV_W), _F32),
                pltpu.VMEM((ROW_TILE, M_CONV_W), _F32),
                pltpu.VMEM((M_GROUPS, M_N, M_GW), _F32)]
    scratch += [pltpu.VMEM((POOL_PAD + POOL_EXT + ROW_TILE, BRANCH), _F32)] * (len(POOL_WINDOWS) - 1)
    return pl.pallas_call(
        _mixers_kernel,
        grid=(batch, seq // ROW_TILE),
        in_specs=in_specs,
        out_specs=[_row_spec(BRANCH)] * N_BRANCHES + [_row_spec(D_MODEL)],
        out_shape=[jax.ShapeDtypeStruct((batch, seq, BRANCH), _BF16)] * N_BRANCHES
        + [jax.ShapeDtypeStruct((batch, seq, D_MODEL), _BF16)],
        scratch_shapes=scratch,
        compiler_params=_params(),
        name="norm_proj_mixers",
    )(*operands)


def _merge_kernel(final, x_ref, mod_ref, h_ref, wg_ref, ret_ref, att_ref, pool_ref, ssm_ref, wup_ref, wout_ref,
                  *rest):
    if final:
        fnw_ref, o_ref = rest
    else:
        (o_ref,) = rest
    h = h_ref[0]
    merged = None
    for i, br_ref in enumerate((ret_ref, att_ref, pool_ref, ssm_ref)):
        gate = _sigmoid(_dot(h, wg_ref[:, i * D_MODEL:(i + 1) * D_MODEL]))
        term = gate * _dot(br_ref[0], wup_ref[i])
        merged = term if merged is None else merged + term
    out = _dot(merged.astype(_BF16), wout_ref[...])
    y = x_ref[0] + mod_ref[0, :, 2 * D_MODEL:3 * D_MODEL] * out
    if final:
        y = y * lax.rsqrt(jnp.mean(y * y, axis=-1, keepdims=True) + EPS) * fnw_ref[...]
    o_ref[0] = y


def _merge(x, mod, h, wg, branches, w_up, w_out, final_norm_w):
    batch, seq, _ = x.shape
    final = final_norm_w is not None
    in_specs = [_row_spec(D_MODEL, MERGE_TILE), _mod_spec(), _row_spec(D_MODEL, MERGE_TILE), _const_spec(wg.shape)]
    in_specs += [_row_spec(BRANCH, MERGE_TILE)] * N_BRANCHES
    in_specs += [_const_spec(w_up.shape), _const_spec(w_out.shape)]
    args = [x, mod, h, wg, *branches, w_up, w_out]
    if final:
        in_specs.append(_const_spec((1, D_MODEL)))
        args.append(final_norm_w.reshape(1, D_MODEL))
    return pl.pallas_call(
        functools.partial(_merge_kernel, final),
        grid=(batch, seq // MERGE_TILE),
        in_specs=in_specs,
        out_specs=_row_spec(D_MODEL, MERGE_TILE),
        out_shape=jax.ShapeDtypeStruct(x.shape, _F32),
        compiler_params=_params(),
        name="merge_out_proj",
    )(*args)


def kernel(x, c, ada_w, ada_b, norm_w, w_in, swa_sinks, pool_w, pool_scale, conv_w, conv_b, dt_bias, a_log,
           d_skip, ssm_norm_w, w_up, w_out, final_norm_w):
    depth = ada_w.shape[0]
    batch = x.shape[0]
    mod_all = _modulation(c, ada_w, ada_b)
    w_in_bf = w_in.astype(_BF16)
    for l in range(depth):
        mod = mod_all[l].reshape(batch, 1, 3 * D_MODEL)
        nw = norm_w[l].reshape(1, D_MODEL)
        w_dt = jnp.pad(w_in_bf[l, :, DT_COLS[0]:DT_COLS[1]], ((0, 0), (0, DT_PAD - M_HEADS)))
        *branches, h = _mixers(l, x, mod, nw, w_in_bf, w_dt, swa_sinks[l], pool_w[l], pool_scale[l], conv_w[l],
                               conv_b[l], dt_bias[l], a_log[l], d_skip[l], ssm_norm_w[l])
        wg = w_in_bf[l, :, MG_COLS[0]:MG_COLS[1]]
        x = _merge(x, mod, h, wg, branches, w_up[l].astype(_BF16), w_out[l].astype(_BF16),
                   final_norm_w if l == depth - 1 else None)
    return x
```
